```python
import jax, jax.numpy as jnp
from jax import lax
import numpy as np

D_MODEL = 1024
BATCH = 8
SEQ = 4096
DEPTH = 2

N_EVEN = (DEPTH + 1) // 2
N_ODD = DEPTH // 2
PLE_DIM = 256
NORM_EPS = 1e-6
NEG_INF = -1e30

CONV_CH = D_MODEL // 2
CONV_WIDTH = 31

HEAD_DIM = 64
NSA_HEADS = (D_MODEL // 2) // HEAD_DIM
NSA_KV_HEADS = 2
NSA_HPG = NSA_HEADS // NSA_KV_HEADS
KV_WIDTH = NSA_KV_HEADS * HEAD_DIM
CMP_LEN = 32
CMP_STRIDE = 16
CMP_HIDDEN = 256
SLC_BLOCK = 64
SLC_TOP = 16
WINDOW = 512
Q_BLOCK = 128
FORCE_SCORE = 1e9

MIX_WIDTH = CONV_CH + NSA_HEADS * HEAD_DIM
IN_WIDTHS = (CONV_CH, CONV_CH, NSA_HEADS * HEAD_DIM) + (KV_WIDTH,) * 6 + (3 * NSA_HEADS,)
IN_COLS = sum(IN_WIDTHS)
IN_SPLITS = [int(v) for v in np.cumsum(IN_WIDTHS)[:-1]]

POOL_WINDOWS = (2, 4, 8, 16)
POOL_GROUP = D_MODEL // len(POOL_WINDOWS)

PEER_HEADS = 8
PEER_QDIM = 128
N_KEYS = 128
N_EXPERTS = N_KEYS * N_KEYS
PEER_TOPK = 16
PEER_CHUNK = 128

kernel_name = "hybrid_conv_nsa_pool_peer"


def rmsnorm(x, g):
    xf = x.astype(jnp.float32)
    y = xf * lax.rsqrt(jnp.mean(xf * xf, axis=-1, keepdims=True) + NORM_EPS)
    return (y * g.astype(jnp.float32)).astype(x.dtype)


def masked_softmax(s, mask):
    s = jnp.where(mask, s.astype(jnp.float32), NEG_INF)
    p = jax.nn.softmax(s, axis=-1)
    return jnp.where(mask, p, 0.0)


def conformer_conv(val, gate, conv_w, conv_b, ln_g, ln_b):
    a = val * jax.nn.sigmoid(gate)
    y = lax.conv_general_dilated(
        a, conv_w[:, None, :].astype(a.dtype), window_strides=(1,),
        padding=[(CONV_WIDTH - 1, 0)], dimension_numbers=('NWC', 'WIO', 'NWC'),
        feature_group_count=CONV_CH)
    y = (y + conv_b).astype(jnp.float32)
    mu = jnp.mean(y, axis=-1, keepdims=True)
    var = jnp.mean(jnp.square(y - mu), axis=-1, keepdims=True)
    y = (y - mu) * lax.rsqrt(var + NORM_EPS) * ln_g.astype(jnp.float32) + ln_b.astype(jnp.float32)
    return jax.nn.silu(y).astype(val.dtype)


def compress_blocks(src, pos, w1, w2):
    B, T = src.shape[:2]
    n_cmp = (T - CMP_LEN) // CMP_STRIDE + 1
    idx = jnp.arange(n_cmp)[:, None] * CMP_STRIDE + jnp.arange(CMP_LEN)[None, :]
    blk = src[:, idx] + pos[None, None, :, None, :]
    blk = blk.transpose(0, 3, 1, 2, 4).reshape(B, NSA_KV_HEADS, n_cmp, CMP_LEN * HEAD_DIM)
    return jax.nn.gelu(blk @ w1) @ w2


def nsa_attention(q, kc, vc, ks, vs, kw, vw, gates, cmp_pos, cmp_w1, cmp_w2):
    B, T = q.shape[:2]
    n_qb = T // Q_BLOCK
    n_slc = T // SLC_BLOCK
    n_top = min(SLC_TOP, n_slc)
    n_cmp = (T - CMP_LEN) // CMP_STRIDE + 1
    scale = HEAD_DIM ** -0.5

    k_cmp = compress_blocks(kc, cmp_pos[0], cmp_w1[0], cmp_w2[0])
    v_cmp = compress_blocks(vc, cmp_pos[1], cmp_w1[1], cmp_w2[1])
    cmp_start = jnp.arange(n_cmp) * CMP_STRIDE
    cmp_end = cmp_start + CMP_LEN - 1
    slc_idx = jnp.arange(n_slc)
    slc_start = slc_idx * SLC_BLOCK
    overlap = ((cmp_start[:, None] < slc_start[None, :] + SLC_BLOCK)
               & (cmp_start[:, None] + CMP_LEN > slc_start[None, :])).astype(jnp.float32)

    ks_blk = ks.transpose(0, 2, 1, 3).reshape(B, NSA_KV_HEADS, n_slc, SLC_BLOCK, HEAD_DIM)
    vs_blk = vs.transpose(0, 2, 1, 3).reshape(B, NSA_KV_HEADS, n_slc, SLC_BLOCK, HEAD_DIM)
    pad = ((0, 0), (0, 0), (WINDOW, 0), (0, 0))
    kw_pad = jnp.pad(kw.transpose(0, 2, 1, 3), pad)
    vw_pad = jnp.pad(vw.transpose(0, 2, 1, 3), pad)
    bi = jnp.arange(B)[:, None, None, None]
    gi = jnp.arange(NSA_KV_HEADS)[None, :, None, None]
    in_blk = jnp.arange(SLC_BLOCK)
    win_off = jnp.arange(WINDOW + Q_BLOCK) - WINDOW

    q_blocks = q.reshape(B, n_qb, Q_BLOCK, NSA_KV_HEADS, NSA_HPG, HEAD_DIM).transpose(1, 0, 3, 4, 2, 5)
    g_blocks = gates.reshape(B, n_qb, Q_BLOCK, NSA_KV_HEADS, NSA_HPG, 3).transpose(1, 0, 3, 4, 2, 5)

    def block_fn(args):
        c, qc, gc = args
        t = c * Q_BLOCK + jnp.arange(Q_BLOCK)
        s = jnp.einsum('bghqd,bgnd->bghqn', qc, k_cmp) * scale
        p_cmp = masked_softmax(s, cmp_end[None, :] <= t[:, None])
        o_cmp = jnp.einsum('bghqn,bgnd->bghqd', p_cmp.astype(qc.dtype), v_cmp)
        imp = jnp.einsum('bghqn,ns->bgqs', p_cmp, overlap)
        cur = t // SLC_BLOCK
        forced = ((slc_idx[None, :] == 0) | (slc_idx[None, :] == cur[:, None])
                  | (slc_idx[None, :] == cur[:, None] - 1))
        imp = jnp.where(forced, FORCE_SCORE, imp)
        imp = jnp.where(slc_start[None, :] <= t[:, None], imp, NEG_INF)
        _, sel = lax.top_k(imp, n_top)
        k_sel = ks_blk[bi, gi, sel]
        v_sel = vs_blk[bi, gi, sel]
        kpos = sel[..., None] * SLC_BLOCK + in_blk
        mask = (kpos <= t[:, None, None]).reshape(B, NSA_KV_HEADS, 1, Q_BLOCK, n_top * SLC_BLOCK)
        s = jnp.einsum('bghqd,bgqnkd->bghqnk', qc, k_sel) * scale
        p = masked_softmax(s.reshape(B, NSA_KV_HEADS, NSA_HPG, Q_BLOCK, n_top * SLC_BLOCK), mask)
        p = p.reshape(B, NSA_KV_HEADS, NSA_HPG, Q_BLOCK, n_top, SLC_BLOCK).astype(qc.dtype)
        o_slc = jnp.einsum('bghqnk,bgqnkd->bghqd', p, v_sel)
        kwin = lax.dynamic_slice_in_dim(kw_pad, c * Q_BLOCK, WINDOW + Q_BLOCK, axis=2)
        vwin = lax.dynamic_slice_in_dim(vw_pad, c * Q_BLOCK, WINDOW + Q_BLOCK, axis=2)
        wpos = c * Q_BLOCK + win_off
        dist = t[:, None] - wpos[None, :]
        wmask = (dist >= 0) & (dist < WINDOW) & (wpos[None, :] >= 0)
        s = jnp.einsum('bghqd,bgkd->bghqk', qc, kwin) * scale
        p = masked_softmax(s, wmask).astype(qc.dtype)
        o_win = jnp.einsum('bghqk,bgkd->bghqd', p, vwin)
        gt = jax.nn.sigmoid(gc.astype(jnp.float32)).astype(qc.dtype)
        return gt[..., 0:1] * o_cmp + gt[..., 1:2] * o_slc + gt[..., 2:3] * o_win

    o = lax.map(block_fn, (jnp.arange(n_qb), q_blocks, g_blocks))
    return o.transpose(1, 0, 4, 2, 3, 5).reshape(B, T, NSA_HEADS * HEAD_DIM)


def mixer_conv_nsa(hn, w_in, conv_w, conv_b, ln_g, ln_b, cmp_pos, cmp_w1, cmp_w2, w_out):
    B, T, _ = hn.shape
    z = hn @ w_in
    val, gate, q, kc, vc, ks, vs, kw, vw, g = jnp.split(z, IN_SPLITS, axis=-1)
    a_out = conformer_conv(val, gate, conv_w, conv_b, ln_g, ln_b)
    kv = lambda u: u.reshape(B, T, NSA_KV_HEADS, HEAD_DIM)
    b_out = nsa_attention(q, kv(kc), kv(vc), kv(ks), kv(vs), kv(kw), kv(vw), g,
                          cmp_pos, cmp_w1, cmp_w2)
    return jnp.concatenate([a_out, b_out], axis=-1) @ w_out


def pool_mixer(hn, pool_w, pool_scale):
    B, T, D = hn.shape
    hf = hn.astype(jnp.float32).reshape(B, T, len(POOL_WINDOWS), POOL_GROUP)
    cs = jnp.pad(jnp.cumsum(hf, axis=1), ((0, 0), (1, 0), (0, 0), (0, 0)))
    pos = jnp.arange(T)
    outs = []
    for gi, w in enumerate(POOL_WINDOWS):
        lo = jnp.maximum(pos + 1 - w, 0)
        cnt = (pos + 1 - lo).astype(jnp.float32)
        mean = (cs[:, pos + 1, gi] - cs[:, lo, gi]) / cnt[None, :, None]
        outs.append(mean - hf[:, :, gi])
    d = jnp.stack(outs, axis=2)
    y = jnp.einsum('btgc,gce->btge', d, pool_w.astype(jnp.float32)).reshape(B, T, D)
    return (y * pool_scale.astype(jnp.float32)).astype(hn.dtype)


def peer_ffn(hn, wq, subkeys, u_tab, v_tab):
    B, T, D = hn.shape
    q = (hn @ wq).reshape(B, T, PEER_HEADS, 2, PEER_QDIM // 2)
    s = jnp.einsum('bthsd,hskd->bthsk', q, subkeys).astype(jnp.float32)
    s_top, i_top = lax.top_k(s, PEER_TOPK)
    cand = s_top[..., 0, :, None] + s_top[..., 1, None, :]
    c_top, c_idx = lax.top_k(cand.reshape(B, T, PEER_HEADS, PEER_TOPK * PEER_TOPK), PEER_TOPK)
    i1 = jnp.take_along_axis(i_top[..., 0, :], c_idx // PEER_TOPK, axis=-1)
    i2 = jnp.take_along_axis(i_top[..., 1, :], c_idx % PEER_TOPK, axis=-1)
    expert = i1 * N_KEYS + i2
    gate = jax.nn.softmax(c_top, axis=-1).astype(hn.dtype)
    n_ch = (B * T) // PEER_CHUNK
    kk = PEER_HEADS * PEER_TOPK
    xs = (hn.reshape(n_ch, PEER_CHUNK, D), expert.reshape(n_ch, PEER_CHUNK, kk),
          gate.reshape(n_ch, PEER_CHUNK, kk))

    def chunk_fn(args):
        xc, ec, gc = args
        u = u_tab[ec]
        v = v_tab[ec]
        act = jax.nn.gelu(jnp.einsum('cd,ckd->ck', xc, u))
        return jnp.einsum('ck,ckd->cd', gc * act, v)

    return lax.map(chunk_fn, xs).reshape(B, T, D)


def setup_inputs(seed: int = 0) -> dict:
    key = jax.random.key(seed)
    ks = jax.random.split(key, 24)
    f32 = jnp.float32
    nrm = lambda k, shape, sc: jax.random.normal(k, shape, f32) * sc
    return {
        "x": nrm(ks[0], (BATCH, SEQ, D_MODEL), 1.0),
        "p": nrm(ks[1], (DEPTH, BATCH, SEQ, PLE_DIM), 1.0),
        "mix_norm": 1.0 + nrm(ks[2], (DEPTH, D_MODEL), 0.02),
        "ab_w_in": nrm(ks[3], (N_EVEN, D_MODEL, IN_COLS), D_MODEL ** -0.5),
        "ab_conv_w": nrm(ks[4], (N_EVEN, CONV_WIDTH, CONV_CH), CONV_WIDTH ** -0.5),
        "ab_conv_b": nrm(ks[5], (N_EVEN, CONV_CH), 0.02),
        "ab_conv_ln_g": 1.0 + nrm(ks[6], (N_EVEN, CONV_CH), 0.02),
        "ab_conv_ln_b": nrm(ks[7], (N_EVEN, CONV_CH), 0.02),
        "ab_cmp_pos": nrm(ks[8], (N_EVEN, 2, CMP_LEN, HEAD_DIM), 0.02),
        "ab_cmp_w1": nrm(ks[9], (N_EVEN, 2, CMP_LEN * HEAD_DIM, CMP_HIDDEN), (CMP_LEN * HEAD_DIM) ** -0.5),
        "ab_cmp_w2": nrm(ks[10], (N_EVEN, 2, CMP_HIDDEN, HEAD_DIM), CMP_HIDDEN ** -0.5),
        "ab_w_out": nrm(ks[11], (N_EVEN, MIX_WIDTH, D_MODEL), MIX_WIDTH ** -0.5),
        "pool_w": nrm(ks[12], (N_ODD, len(POOL_WINDOWS), POOL_GROUP, POOL_GROUP), POOL_GROUP ** -0.5),
        "pool_scale": 1.0 + nrm(ks[13], (N_ODD, D_MODEL), 0.1),
        "ffn_norm": 1.0 + nrm(ks[14], (DEPTH, D_MODEL), 0.02),
        "peer_wq": nrm(ks[15], (DEPTH, D_MODEL, PEER_HEADS * PEER_QDIM), D_MODEL ** -0.5),
        "peer_subkeys": nrm(ks[16], (DEPTH, PEER_HEADS, 2, N_KEYS, PEER_QDIM // 2), (PEER_QDIM // 2) ** -0.5),
        "peer_u": nrm(ks[17], (DEPTH, N_EXPERTS, D_MODEL), D_MODEL ** -0.5),
        "peer_v": nrm(ks[18], (DEPTH, N_EXPERTS, D_MODEL), 0.1),
        "ple_norm": 1.0 + nrm(ks[19], (DEPTH, D_MODEL), 0.02),
        "ple_gate_w": nrm(ks[20], (DEPTH, D_MODEL, D_MODEL), D_MODEL ** -0.5),
        "ple_proj": nrm(ks[21], (DEPTH, PLE_DIM, D_MODEL), PLE_DIM ** -0.5),
        "final_norm": 1.0 + nrm(ks[22], (D_MODEL,), 0.02),
    }


def reference(x, p, mix_norm, ab_w_in, ab_conv_w, ab_conv_b, ab_conv_ln_g, ab_conv_ln_b,
              ab_cmp_pos, ab_cmp_w1, ab_cmp_w2, ab_w_out, pool_w, pool_scale, ffn_norm,
              peer_wq, peer_subkeys, peer_u, peer_v, ple_norm, ple_gate_w, ple_proj, final_norm):
    h = x
    for i in range(DEPTH):
        j = i // 2
        hn = rmsnorm(h, mix_norm[i])
        if i % 2 == 0:
            h = h + mixer_conv_nsa(hn, ab_w_in[j], ab_conv_w[j], ab_conv_b[j], ab_conv_ln_g[j],
                                   ab_conv_ln_b[j], ab_cmp_pos[j], ab_cmp_w1[j], ab_cmp_w2[j],
                                   ab_w_out[j])
        else:
            h = h + pool_mixer(hn, pool_w[j], pool_scale[j])
        hn = rmsnorm(h, ffn_norm[i])
        h = h + peer_ffn(hn, peer_wq[i], peer_subkeys[i], peer_u[i], peer_v[i])
        gate = jax.nn.sigmoid(rmsnorm(h, ple_norm[i]) @ ple_gate_w[i])
        h = h + (p[i] @ ple_proj[i]) * gate
    return rmsnorm(h, final_norm)
```

```python
import functools

import jax
import jax.numpy as jnp
import numpy as np
from jax import lax
from jax.experimental import pallas as pl
from jax.experimental.pallas import tpu as pltpu

F32 = jnp.float32
BF16 = jnp.bfloat16
HIGHEST = lax.Precision.HIGHEST

D_MODEL = 1024
NORM_EPS = 1e-6
NEG_INF = -1e30

CONV_CH = 512
CONV_WIDTH = 31
CONV_HALO = 32

HEAD_DIM = 64
NSA_KV_HEADS = 2
NSA_HPG = 4
CMP_LEN = 32
CMP_STRIDE = 16
CMP_HIDDEN = 256
SLC_BLOCK = 64
SLC_SHIFT = 6
SLC_TOP = 16
WINDOW = 512
FORCE_SCORE = 1e9

POOL_WINDOWS = (2, 4, 8, 16)
POOL_GROUP = 256
POOL_HALO = 16

PEER_HEADS = 8
N_KEYS = 128
PEER_TOPK = 16
PEER_KK = PEER_HEADS * PEER_TOPK
HALF = D_MODEL // 2
ROW_SUB = HALF // 128

VMEM_LIMIT = 56 * 1024 * 1024


def _cparams(sem, vmem=None):
    return pltpu.CompilerParams(dimension_semantics=sem, vmem_limit_bytes=vmem)


def _rms(x, g):
    return x * lax.rsqrt(jnp.mean(x * x, axis=-1, keepdims=True) + NORM_EPS) * g


def _gelu(x):
    return 0.5 * x * (1.0 + jnp.tanh(0.7978845608028654 * (x + 0.044715 * (x * x * x))))


def _sigmoid(x):
    return 1.0 / (1.0 + jnp.exp(-x))


def _norm_mm_kernel(x_ref, g_ref, *refs, n_w, emit_hn):
    w_refs = refs[:n_w]
    o_refs = refs[n_w:]
    y = _rms(x_ref[...], g_ref[...])
    yb = y.astype(BF16)
    for w_ref, o_ref in zip(w_refs, o_refs[:n_w]):
        o_ref[...] = jnp.dot(yb, w_ref[...], preferred_element_type=F32)
    if emit_hn:
        o_refs[n_w][...] = y[:, :HALF]
        o_refs[n_w + 1][...] = y[:, HALF:]


def norm_matmul(x, g, ws, *, emit_hn=False, tm=512):
    m, d = x.shape
    n_w = len(ws)
    in_specs = [pl.BlockSpec((tm, d), lambda i: (i, 0)), pl.BlockSpec((1, d), lambda i: (0, 0))]
    in_specs += [pl.BlockSpec(w.shape, lambda i: (0, 0)) for w in ws]
    out_shape = [jax.ShapeDtypeStruct((m, w.shape[1]), F32) for w in ws]
    out_specs = [pl.BlockSpec((tm, w.shape[1]), lambda i: (i, 0)) for w in ws]
    if emit_hn:
        out_shape += [jax.ShapeDtypeStruct((m, HALF), F32)] * 2
        out_specs += [pl.BlockSpec((tm, HALF), lambda i: (i, 0))] * 2
    return pl.pallas_call(
        functools.partial(_norm_mm_kernel, n_w=n_w, emit_hn=emit_hn),
        grid=(m // tm,),
        in_specs=in_specs, out_specs=out_specs, out_shape=out_shape,
        compiler_params=_cparams(("parallel",), VMEM_LIMIT),
        name="norm_matmul",
    )(x, g.reshape(1, d), *ws)


def _conv_kernel(cur_ref, halo_ref, w_ref, b_ref, g_ref, beta_ref, o_ref, buf, *, tt):
    t = pl.program_id(1)
    cur = cur_ref[0]
    halo = halo_ref[0]
    a_halo = halo[:, :CONV_CH] * _sigmoid(halo[:, CONV_CH:])
    buf[0:CONV_HALO, :] = jnp.where(t > 0, a_halo, 0.0)
    buf[CONV_HALO:, :] = cur[:, :CONV_CH] * _sigmoid(cur[:, CONV_CH:])
    acc = jnp.zeros((tt, CONV_CH), F32)
    first = CONV_HALO - (CONV_WIDTH - 1)
    for j in range(CONV_WIDTH):
        acc = acc + buf[pl.ds(first + j, tt), :] * w_ref[j:j + 1, :]
    y = acc + b_ref[...]
    mu = jnp.mean(y, axis=-1, keepdims=True)
    yc = y - mu
    var = jnp.mean(yc * yc, axis=-1, keepdims=True)
    y = yc * lax.rsqrt(var + NORM_EPS) * g_ref[...] + beta_ref[...]
    o_ref[0] = y * _sigmoid(y)


def conformer_conv(vg, conv_w, conv_b, ln_g, ln_b, *, tt=512):
    b, t, _ = vg.shape
    tt = min(tt, t)
    per = tt // CONV_HALO
    vec = lambda v: v.reshape(1, CONV_CH)
    vspec = pl.BlockSpec((1, CONV_CH), lambda bi, ti: (0, 0))
    return pl.pallas_call(
        functools.partial(_conv_kernel, tt=tt),
        grid=(b, t // tt),
        in_specs=[
            pl.BlockSpec((1, tt, 2 * CONV_CH), lambda bi, ti: (bi, ti, 0)),
            pl.BlockSpec((1, CONV_HALO, 2 * CONV_CH), lambda bi, ti: (bi, jnp.maximum(ti * per - 1, 0), 0)),
            pl.BlockSpec((CONV_WIDTH, CONV_CH), lambda bi, ti: (0, 0)),
            vspec, vspec, vspec,
        ],
        out_specs=pl.BlockSpec((1, tt, CONV_CH), lambda bi, ti: (bi, ti, 0)),
        out_shape=jax.ShapeDtypeStruct((b, t, CONV_CH), F32),
        scratch_shapes=[pltpu.VMEM((tt + CONV_HALO, CONV_CH), F32)],
        compiler_params=_cparams(("parallel", "parallel"), VMEM_LIMIT),
        name="conformer_conv",
    )(vg, vg, conv_w, vec(conv_b), vec(ln_g), vec(ln_b))


def _compress_kernel(c_ref, pos_ref, w1_ref, w2_ref, o_ref):
    c = c_ref[0, 0]
    pos = pos_ref[0]
    w1 = w1_ref[0]
    half = CMP_STRIDE * HEAD_DIM
    ua = jnp.dot(c + pos[:, :half], w1[:half], preferred_element_type=F32, precision=HIGHEST)
    ub = jnp.dot(c + pos[:, half:], w1[half:], preferred_element_type=F32, precision=HIGHEST)
    n = c.shape[0]
    hid = _gelu(ua + pltpu.roll(ub, n - 1, axis=0))
    o_ref[0, 0] = jnp.dot(hid, w2_ref[0], preferred_element_type=F32, precision=HIGHEST)


def compress_blocks(src, pos, w1, w2):
    _, bg, n_chunk, width = src.shape
    return pl.pallas_call(
        _compress_kernel,
        grid=(2, bg),
        in_specs=[
            pl.BlockSpec((1, 1, n_chunk, width), lambda k, i: (k, i, 0, 0)),
            pl.BlockSpec((1, 1, 2 * width), lambda k, i: (k, 0, 0)),
            pl.BlockSpec((1, 2 * width, CMP_HIDDEN), lambda k, i: (k, 0, 0)),
            pl.BlockSpec((1, CMP_HIDDEN, HEAD_DIM), lambda k, i: (k, 0, 0)),
        ],
        out_specs=pl.BlockSpec((1, 1, n_chunk, HEAD_DIM), lambda k, i: (k, i, 0, 0)),
        out_shape=jax.ShapeDtypeStruct((2, bg, n_chunk, HEAD_DIM), F32),
        compiler_params=_cparams(("parallel", "parallel"), VMEM_LIMIT),
        name="nsa_compress",
    )(src, pos.reshape(2, 1, 2 * width), w1, w2)


def _topk_rows(work, k):
    n = work.shape[0]
    row = lax.broadcasted_iota(jnp.int32, work.shape, 0).astype(F32)
    vals, idxs = [], []
    for _ in range(k):
        m = jnp.max(work, axis=0, keepdims=True)
        first = jnp.min(jnp.where(work == m, row, float(n)), axis=0, keepdims=True)
        vals.append(m)
        idxs.append(first)
        work = jnp.where(row == first, -jnp.inf, work)
    return vals, idxs


def _masked_flash(qb, k_ref, v_ref, lo, hi, tk, mask_fn):
    rows = qb.shape[0]
    tq = rows // NSA_HPG
    scale = HEAD_DIM ** -0.5

    def body(j, carry):
        m, l, acc = carry
        off = pl.multiple_of(j * tk, tk)
        kt = k_ref[0, 0, pl.ds(off, tk), :]
        vt = v_ref[0, 0, pl.ds(off, tk), :]
        s = lax.dot_general(qb, kt, (((1,), (1,)), ((), ())), preferred_element_type=F32) * scale
        ok = mask_fn(j)[None]
        s = jnp.where(ok, s.reshape(NSA_HPG, tq, tk), NEG_INF).reshape(rows, tk)
        m_new = jnp.maximum(m, jnp.max(s, axis=-1, keepdims=True))
        alpha = jnp.exp(m - m_new)
        p = jnp.where(ok, jnp.exp(s - m_new).reshape(NSA_HPG, tq, tk), 0.0).reshape(rows, tk)
        l = alpha * l + jnp.sum(p, axis=-1, keepdims=True)
        acc = alpha * acc + jnp.dot(p.astype(BF16), vt, preferred_element_type=F32)
        return m_new, l, acc

    init = (jnp.full((rows, 1), NEG_INF, F32), jnp.zeros((rows, 1), F32), jnp.zeros((rows, HEAD_DIM), F32))
    _, l, acc = lax.fori_loop(lo, hi, body, init)
    return acc / l


def _nsa_kernel(q_ref, kc_ref, vc_ref, ks_ref, vs_ref, kw_ref, vw_ref, g_ref, ov_ref, o_ref, *, tq, tks, tkw):
    i = pl.program_id(2)
    t0 = i * tq
    rows = NSA_HPG * tq
    scale = HEAD_DIM ** -0.5
    q = q_ref[0, 0].reshape(rows, HEAD_DIM)
    qb = q.astype(BF16)
    t_q = t0 + lax.broadcasted_iota(jnp.int32, (tq, 1), 0)
    t_all = jnp.concatenate([t_q] * NSA_HPG, axis=0)

    kc = kc_ref[0, 0]
    n_cmp = kc.shape[0]
    s = lax.dot_general(q, kc, (((1,), (1,)), ((), ())), preferred_element_type=F32, precision=HIGHEST) * scale
    cmp_end = lax.broadcasted_iota(jnp.int32, (1, n_cmp), 1) * CMP_STRIDE + (CMP_LEN - 1)
    ok = cmp_end <= t_all
    s = jnp.where(ok, s, NEG_INF)
    e = jnp.where(ok, jnp.exp(s - jnp.max(s, axis=-1, keepdims=True)), 0.0)
    den = jnp.sum(e, axis=-1, keepdims=True)
    p_cmp = e / jnp.where(den > 0.0, den, 1.0)
    o_cmp = jnp.dot(p_cmp.astype(BF16), vc_ref[0, 0].astype(BF16), preferred_element_type=F32)

    p_sum = p_cmp[0:tq]
    for h in range(1, NSA_HPG):
        p_sum = p_sum + p_cmp[h * tq:(h + 1) * tq]
    imp = jnp.dot(p_sum, ov_ref[...], preferred_element_type=F32, precision=HIGHEST)
    n_slc = ks_ref.shape[2] // SLC_BLOCK
    imp_t = imp.T[0:n_slc]
    blk = lax.broadcasted_iota(jnp.int32, (n_slc, tq), 0)
    t_lane = t0 + lax.broadcasted_iota(jnp.int32, (n_slc, tq), 1)
    cur = t_lane >> SLC_SHIFT
    forced = (blk == 0) | (blk == cur) | (blk == cur - 1)
    imp_t = jnp.where(forced, FORCE_SCORE, imp_t)
    imp_t = jnp.where(blk * SLC_BLOCK <= t_lane, imp_t, NEG_INF)
    _, picks = _topk_rows(imp_t, min(SLC_TOP, n_slc))
    blk_f = blk.astype(F32)
    member = jnp.zeros((n_slc, tq), F32)
    for pk in picks:
        member = jnp.where(blk_f == pk, 1.0, member)
    if n_slc < 128:
        member = jnp.concatenate([member, jnp.zeros((128 - n_slc, tq), F32)], axis=0)
    member_q = member.T.astype(BF16)

    blocks_per_tile = tks // SLC_BLOCK

    def slc_mask(j):
        sel_row = lax.broadcasted_iota(jnp.int32, (128, tks), 0)
        key_blk = j * blocks_per_tile + (lax.broadcasted_iota(jnp.int32, (128, tks), 1) >> SLC_SHIFT)
        expand = jnp.where(sel_row == key_blk, 1.0, 0.0).astype(BF16)
        sel = jnp.dot(member_q, expand, preferred_element_type=F32) > 0.5
        kpos = j * tks + lax.broadcasted_iota(jnp.int32, (1, tks), 1)
        return sel & (kpos <= t_q)

    o_slc = _masked_flash(qb, ks_ref, vs_ref, 0, (t0 + tq + tks - 1) // tks, tks, slc_mask)

    def win_mask(j):
        kpos = j * tkw + lax.broadcasted_iota(jnp.int32, (1, tkw), 1)
        dist = t_q - kpos
        return (dist >= 0) & (dist < WINDOW)

    lo = jnp.maximum(t0 - (WINDOW - 1), 0) // tkw
    o_win = _masked_flash(qb, kw_ref, vw_ref, lo, (t0 + tq + tkw - 1) // tkw, tkw, win_mask)

    gate = _sigmoid(g_ref[0, 0])
    for h in range(NSA_HPG):
        r = slice(h * tq, (h + 1) * tq)
        o_ref[0, 0, h] = (gate[:, 3 * h:3 * h + 1] * o_cmp[r] + gate[:, 3 * h + 1:3 * h + 2] * o_slc[r]
                          + gate[:, 3 * h + 2:3 * h + 3] * o_win[r])


def nsa_attention(qh, kc, vc, ks, vs, kw, vw, gates, *, tq=128, tks=256, tkw=128):
    b, g, _, t, _ = qh.shape
    n_chunk = kc.shape[2]
    n_slc = t // SLC_BLOCK
    n_cmp = (t - CMP_LEN) // CMP_STRIDE + 1
    tks = min(tks, t)
    cmp_start = np.arange(n_chunk) * CMP_STRIDE
    slc_start = np.arange(128) * SLC_BLOCK
    overlap = ((cmp_start[:, None] < slc_start[None, :] + SLC_BLOCK)
               & (cmp_start[:, None] + CMP_LEN > slc_start[None, :])
               & (np.arange(n_chunk)[:, None] < n_cmp) & (np.arange(128)[None, :] < n_slc))
    overlap = jnp.asarray(overlap.astype(np.float32))
    full = lambda rows: pl.BlockSpec((1, 1, rows, HEAD_DIM), lambda bi, gi, i: (bi, gi, 0, 0))
    return pl.pallas_call(
        functools.partial(_nsa_kernel, tq=tq, tks=tks, tkw=tkw),
        grid=(b, g, t // tq),
        in_specs=[
            pl.BlockSpec((1, 1, NSA_HPG, tq, HEAD_DIM), lambda bi, gi, i: (bi, gi, 0, i, 0)),
            full(n_chunk), full(n_chunk), full(t), full(t), full(t), full(t),
            pl.BlockSpec((1, 1, tq, 3 * NSA_HPG), lambda bi, gi, i: (bi, gi, i, 0)),
            pl.BlockSpec((n_chunk, 128), lambda bi, gi, i: (0, 0)),
        ],
        out_specs=pl.BlockSpec((1, 1, NSA_HPG, tq, HEAD_DIM), lambda bi, gi, i: (bi, gi, 0, i, 0)),
        out_shape=jax.ShapeDtypeStruct(qh.shape, F32),
        compiler_params=_cparams(("parallel", "parallel", "arbitrary"), VMEM_LIMIT),
        name="nsa_attention",
    )(qh, kc, vc, ks, vs, kw, vw, gates, overlap)


def _mm2_res_kernel(h_ref, a_ref, b_ref, wa_ref, wb_ref, o_ref):
    acc = jnp.dot(a_ref[...].astype(BF16), wa_ref[...], preferred_element_type=F32)
    acc = acc + jnp.dot(b_ref[...].astype(BF16), wb_ref[...], preferred_element_type=F32)
    o_ref[...] = h_ref[...] + acc


def mm2_residual(h, a, b, wa, wb, *, tm=512):
    m, d = h.shape
    row = lambda w: pl.BlockSpec((tm, w), lambda i: (i, 0))
    whole = lambda w: pl.BlockSpec(w.shape, lambda i: (0, 0))
    return pl.pallas_call(
        _mm2_res_kernel,
        grid=(m // tm,),
        in_specs=[row(d), row(a.shape[1]), row(b.shape[1]), whole(wa), whole(wb)],
        out_specs=row(d),
        out_shape=jax.ShapeDtypeStruct((m, d), F32),
        compiler_params=_cparams(("parallel",), VMEM_LIMIT),
        name="out_proj_residual",
    )(h, a, b, wa, wb)


def _pool_kernel(cur_ref, halo_ref, g_ref, w_ref, sc_ref, o_ref, buf, *, tt):
    t = pl.program_id(1)
    g = g_ref[...]
    cur = cur_ref[0]
    hn = _rms(cur, g)
    buf[0:POOL_HALO, :] = jnp.where(t > 0, _rms(halo_ref[0], g), 0.0)
    buf[POOL_HALO:, :] = hn
    pos = t * tt + lax.broadcasted_iota(jnp.int32, (tt, 1), 0)
    outs = []
    for gi, w in enumerate(POOL_WINDOWS):
        cols = slice(gi * POOL_GROUP, (gi + 1) * POOL_GROUP)
        tot = hn[:, cols]
        for j in range(1, w):
            tot = tot + buf[pl.ds(POOL_HALO - j, tt), cols]
        cnt = jnp.minimum(pos + 1, w).astype(F32)
        d = tot / cnt - hn[:, cols]
        outs.append(jnp.dot(d.astype(BF16), w_ref[gi], preferred_element_type=F32))
    y = jnp.concatenate(outs, axis=-1) * sc_ref[...]
    o_ref[0] = cur + y


def pool_mixer_residual(h, g, pool_w, pool_scale, *, tt=512):
    b, t, d = h.shape
    tt = min(tt, t)
    per = tt // POOL_HALO
    return pl.pallas_call(
        functools.partial(_pool_kernel, tt=tt),
        grid=(b, t // tt),
        in_specs=[
            pl.BlockSpec((1, tt, d), lambda bi, ti: (bi, ti, 0)),
            pl.BlockSpec((1, POOL_HALO, d), lambda bi, ti: (bi, jnp.maximum(ti * per - 1, 0), 0)),
            pl.BlockSpec((1, d), lambda bi, ti: (0, 0)),
            pl.BlockSpec(pool_w.shape, lambda bi, ti: (0, 0, 0)),
            pl.BlockSpec((1, d), lambda bi, ti: (0, 0)),
        ],
        out_specs=pl.BlockSpec((1, tt, d), lambda bi, ti: (bi, ti, 0)),
        out_shape=jax.ShapeDtypeStruct(h.shape, F32),
        scratch_shapes=[pltpu.VMEM((tt + POOL_HALO, d), F32)],
        compiler_params=_cparams(("parallel", "parallel"), VMEM_LIMIT),
        name="pool_mixer",
    )(h, h, g.reshape(1, d), pool_w.astype(BF16), pool_scale.reshape(1, d))


def _peer_route_kernel(q_ref, key_ref, idx_ref, gate_ref):
    q = q_ref[...]
    s = lax.dot_general(key_ref[0], q, (((1,), (1,)), ((), ())), preferred_element_type=F32,
                        precision=HIGHEST)
    v0, i0 = _topk_rows(s[0:N_KEYS], PEER_TOPK)
    v1, i1 = _topk_rows(s[N_KEYS:], PEER_TOPK)
    s1 = jnp.concatenate(v1, axis=0)
    cand = jnp.concatenate([v0[a] + s1 for a in range(PEER_TOPK)], axis=0)
    cv, ci = _topk_rows(cand, PEER_TOPK)
    c_top = jnp.concatenate(cv, axis=0)
    c_idx = jnp.concatenate(ci, axis=0).astype(jnp.int32)
    a_sel = c_idx >> 4
    b_sel = c_idx & (PEER_TOPK - 1)
    k1 = jnp.zeros_like(c_top)
    k2 = jnp.zeros_like(c_top)
    for r in range(PEER_TOPK):
        k1 = jnp.where(a_sel == r, i0[r], k1)
        k2 = jnp.where(b_sel == r, i1[r], k2)
    idx_ref[0] = (k1 * N_KEYS + k2).astype(jnp.int32)
    e = jnp.exp(c_top - jnp.max(c_top, axis=0, keepdims=True))
    gate_ref[0] = e / jnp.sum(e, axis=0, keepdims=True)


def peer_route(q, keys_bd, *, tb=256):
    m = q.shape[0]
    shp = (PEER_HEADS, PEER_TOPK, m)
    ospec = pl.BlockSpec((1, PEER_TOPK, tb), lambda i, h: (h, 0, i))
    return pl.pallas_call(
        _peer_route_kernel,
        grid=(m // tb, PEER_HEADS),
        in_specs=[pl.BlockSpec((tb, 2 * HEAD_DIM), lambda i, h: (i, h)),
                  pl.BlockSpec((1, 2 * N_KEYS, 2 * HEAD_DIM), lambda i, h: (h, 0, 0))],
        out_specs=[ospec, ospec],
        out_shape=[jax.ShapeDtypeStruct(shp, jnp.int32), jax.ShapeDtypeStruct(shp, F32)],
        compiler_params=_cparams(("parallel", "parallel"), VMEM_LIMIT),
        name="peer_route",
    )(q, keys_bd)


def pack_table(tab):
    bits = lax.bitcast_convert_type(tab.astype(BF16), jnp.uint16).astype(jnp.uint32)
    packed = (bits[:, :HALF] << 16) | bits[:, HALF:]
    return packed.reshape(tab.shape[0], ROW_SUB, 128)


def _unpack(w):
    hi = lax.bitcast_convert_type(w & jnp.uint32(0xFFFF0000), F32)
    lo = lax.bitcast_convert_type(w << 16, F32)
    return hi, lo


def _load_table_and_rows(tab_hbm, tab_vmem, sems, smem_pairs, tb):
    i = pl.program_id(0)

    @pl.when(i == 0)
    def _():
        cp = pltpu.make_async_copy(tab_hbm, tab_vmem, sems.at[0])
        cp.start()
        cp.wait()

    row0 = pl.multiple_of(i * tb, tb)
    copies = [pltpu.make_async_copy(src.at[pl.ds(row0, tb)], dst, sems.at[1 + n])
              for n, (src, dst) in enumerate(smem_pairs)]
    for cp in copies:
        cp.start()
    for cp in copies:
        cp.wait()


def _peer_act_kernel(idx_hbm, tab_hbm, xa_ref, xb_ref, gate_ref, coef_ref, tab_vmem, idx_smem, zbuf, sems, *, tb):
    _load_table_and_rows(tab_hbm, tab_vmem, sems, [(idx_hbm, idx_smem)], tb)

    def token(t, carry):
        xa = xa_ref[t]
        xb = xb_ref[t]
        for k in range(PEER_KK):
            hi, lo = _unpack(tab_vmem[idx_smem[t, k]])
            zbuf[k:k + 1, :] = jnp.sum(hi * xa + lo * xb, axis=0, keepdims=True)
        coef_ref[t] = jnp.sum(zbuf[...].T, axis=0, keepdims=True)
        return carry

    lax.fori_loop(0, tb, token, 0)
    coef_ref[...] = gate_ref[...] * _gelu(coef_ref[...])


def peer_coefficients(idx, tab, xa, xb, gate, *, tb=256):
    m = idx.shape[0]
    tok = lambda last: pl.BlockSpec((tb,) + last, lambda i: (i, 0, 0))
    return pl.pallas_call(
        functools.partial(_peer_act_kernel, tb=tb),
        grid=(m // tb,),
        in_specs=[pl.BlockSpec(memory_space=pl.ANY), pl.BlockSpec(memory_space=pl.ANY),
                  tok((ROW_SUB, 128)), tok((ROW_SUB, 128)), tok((1, PEER_KK))],
        out_specs=tok((1, PEER_KK)),
        out_shape=jax.ShapeDtypeStruct((m, 1, PEER_KK), F32),
        scratch_shapes=[pltpu.VMEM(tab.shape, jnp.uint32), pltpu.SMEM((tb, PEER_KK), jnp.int32),
                        pltpu.VMEM((PEER_KK, 128), F32), pltpu.SemaphoreType.DMA((2,))],
        compiler_params=_cparams(("arbitrary",), VMEM_LIMIT),
        name="peer_coefficients",
    )(idx, tab, xa.reshape(m, ROW_SUB, 128), xb.reshape(m, ROW_SUB, 128), gate.reshape(m, 1, PEER_KK))


def _peer_mix_kernel(idx_hbm, coef_hbm, tab_hbm, ya_ref, yb_ref, tab_vmem, idx_smem, coef_smem, sems, *, tb):
    _load_table_and_rows(tab_hbm, tab_vmem, sems, [(idx_hbm, idx_smem), (coef_hbm, coef_smem)], tb)
    n_acc = 4

    def token(t, carry):
        acc_a = [jnp.zeros((ROW_SUB, 128), F32) for _ in range(n_acc)]
        acc_b = [jnp.zeros((ROW_SUB, 128), F32) for _ in range(n_acc)]
        for k in range(PEER_KK):
            hi, lo = _unpack(tab_vmem[idx_smem[t, k]])
            c = coef_smem[t, k]
            acc_a[k % n_acc] = acc_a[k % n_acc] + c * hi
            acc_b[k % n_acc] = acc_b[k % n_acc] + c * lo
        ya_ref[t] = (acc_a[0] + acc_a[1]) + (acc_a[2] + acc_a[3])
        yb_ref[t] = (acc_b[0] + acc_b[1]) + (acc_b[2] + acc_b[3])
        return carry

    lax.fori_loop(0, tb, token, 0)


def peer_mix(idx, coef, tab, *, tb=256):
    m = idx.shape[0]
    tok = pl.BlockSpec((tb, ROW_SUB, 128), lambda i: (i, 0, 0))
    shp = jax.ShapeDtypeStruct((m, ROW_SUB, 128), F32)
    ya, yb = pl.pallas_call(
        functools.partial(_peer_mix_kernel, tb=tb),
        grid=(m // tb,),
        in_specs=[pl.BlockSpec(memory_space=pl.ANY)] * 3,
        out_specs=[tok, tok],
        out_shape=[shp, shp],
        scratch_shapes=[pltpu.VMEM(tab.shape, jnp.uint32), pltpu.SMEM((tb, PEER_KK), jnp.int32),
                        pltpu.SMEM((tb, PEER_KK), F32), pltpu.SemaphoreType.DMA((3,))],
        compiler_params=_cparams(("arbitrary",), VMEM_LIMIT),
        name="peer_mix",
    )(idx, coef, tab)
    return ya.reshape(m, HALF), yb.reshape(m, HALF)


def _ple_kernel(h_ref, ya_ref, yb_ref, p_ref, g_ref, wg_ref, wp_ref, fg_ref, o_ref, *, final_norm):
    h = h_ref[...] + jnp.concatenate([ya_ref[...], yb_ref[...]], axis=-1)
    gate = _sigmoid(jnp.dot(_rms(h, g_ref[...]).astype(BF16), wg_ref[...], preferred_element_type=F32))
    h = h + jnp.dot(p_ref[...].astype(BF16), wp_ref[...], preferred_element_type=F32) * gate
    if final_norm:
        h = _rms(h, fg_ref[...])
    o_ref[...] = h


def ple_residual(h, ya, yb, p, g, wg, wp, fg, *, final_norm, tm=512):
    m, d = h.shape
    row = lambda w: pl.BlockSpec((tm, w), lambda i: (i, 0))
    whole = lambda a: pl.BlockSpec(a.shape, lambda i: (0, 0))
    g2, fg2 = g.reshape(1, d), fg.reshape(1, d)
    return pl.pallas_call(
        functools.partial(_ple_kernel, final_norm=final_norm),
        grid=(m // tm,),
        in_specs=[row(d), row(HALF), row(HALF), row(p.shape[1]), whole(g2), whole(wg), whole(wp), whole(fg2)],
        out_specs=row(d),
        out_shape=jax.ShapeDtypeStruct((m, d), F32),
        compiler_params=_cparams(("parallel",), VMEM_LIMIT),
        name="ple_residual",
    )(h, ya, yb, p, g2, wg, wp, fg2)


def _mixer_conv_nsa(h, b, t, norm_g, w_in, conv_w, conv_b, ln_g, ln_b, cmp_pos, cmp_w1, cmp_w2, w_out):
    m = b * t
    n_q = NSA_KV_HEADS * NSA_HPG * HEAD_DIM
    kv_w = NSA_KV_HEADS * HEAD_DIM
    c0, c1, c2 = 2 * CONV_CH, 2 * CONV_CH + n_q, 2 * CONV_CH + n_q + 6 * kv_w
    wb = w_in.astype(BF16)
    vg, q, kv, gt = norm_matmul(h, norm_g, [wb[:, :c0], wb[:, c0:c1], wb[:, c1:c2], wb[:, c2:]])
    a_out = conformer_conv(vg.reshape(b, t, c0), conv_w, conv_b, ln_g, ln_b)

    kv = kv.reshape(b, t, 6, NSA_KV_HEADS, HEAD_DIM).transpose(2, 0, 3, 1, 4)
    n_chunk = t // CMP_STRIDE
    cmp_src = kv[0:2].reshape(2, b * NSA_KV_HEADS, n_chunk, CMP_STRIDE * HEAD_DIM)
    cmp = compress_blocks(cmp_src, cmp_pos.reshape(2, CMP_LEN * HEAD_DIM), cmp_w1, cmp_w2)
    cmp = cmp.reshape(2, b, NSA_KV_HEADS, n_chunk, HEAD_DIM)
    kvb = kv[2:].astype(BF16)
    qh = q.reshape(b, t, NSA_KV_HEADS, NSA_HPG, HEAD_DIM).transpose(0, 2, 3, 1, 4)
    gates = gt.reshape(b, t, NSA_KV_HEADS, 3 * NSA_HPG).transpose(0, 2, 1, 3)
    o = nsa_attention(qh, cmp[0], cmp[1], kvb[0], kvb[1], kvb[2], kvb[3], gates)
    b_out = o.transpose(0, 3, 1, 2, 4).reshape(m, n_q)
    wo = w_out.astype(BF16)
    return mm2_residual(h, a_out.reshape(m, CONV_CH), b_out, wo[:CONV_CH], wo[CONV_CH:])


def _peer_ffn(h, norm_g, wq, subkeys, u_tab, v_tab):
    q, xa, xb = norm_matmul(h, norm_g, [wq.astype(BF16)], emit_hn=True)
    zeros = jnp.zeros_like(subkeys[:, 0])
    keys_bd = jnp.concatenate([jnp.concatenate([subkeys[:, 0], zeros], axis=-1),
                               jnp.concatenate([zeros, subkeys[:, 1]], axis=-1)], axis=1)
    idx_t, gate_t = peer_route(q, keys_bd)
    m = h.shape[0]
    idx = idx_t.reshape(PEER_KK, m).T
    gate = gate_t.reshape(PEER_KK, m).T
    coef = peer_coefficients(idx, pack_table(u_tab), xa, xb, gate)
    return peer_mix(idx, coef.reshape(m, PEER_KK), pack_table(v_tab))


def kernel(x, p, mix_norm, ab_w_in, ab_conv_w, ab_conv_b, ab_conv_ln_g, ab_conv_ln_b, ab_cmp_pos, ab_cmp_w1,
           ab_cmp_w2, ab_w_out, pool_w, pool_scale, ffn_norm, peer_wq, peer_subkeys, peer_u, peer_v, ple_norm,
           ple_gate_w, ple_proj, final_norm):
    b, t, d = x.shape
    m = b * t
    depth = p.shape[0]
    h = x.reshape(m, d)
    for i in range(depth):
        j = i // 2
        if i % 2 == 0:
            h = _mixer_conv_nsa(h, b, t, mix_norm[i], ab_w_in[j], ab_conv_w[j], ab_conv_b[j], ab_conv_ln_g[j],
                                ab_conv_ln_b[j], ab_cmp_pos[j], ab_cmp_w1[j], ab_cmp_w2[j], ab_w_out[j])
        else:
            h = pool_mixer_residual(h.reshape(b, t, d), mix_norm[i], pool_w[j], pool_scale[j]).reshape(m, d)
        ya, yb = _peer_ffn(h, ffn_norm[i], peer_wq[i], peer_subkeys[i], peer_u[i], peer_v[i])
        h = ple_residual(h, ya, yb, p[i].reshape(m, -1), ple_norm[i], ple_gate_w[i].astype(BF16),
                         ple_proj[i].astype(BF16), final_norm, final_norm=(i == depth - 1))
    return h.reshape(b, t, d)
```

```python
import functools

import jax
import jax.numpy as jnp
import numpy as np
from jax import lax
from jax.experimental import pallas as pl
from jax.experimental.pallas import tpu as pltpu

F32 = jnp.float32
BF16 = jnp.bfloat16
HIGHEST = lax.Precision.HIGHEST

D_MODEL = 1024
NORM_EPS = 1e-6
NEG_INF = -1e30

CONV_CH = 512
CONV_WIDTH = 31
CONV_HALO = 32

HEAD_DIM = 64
NSA_KV_HEADS = 2
NSA_HPG = 4
CMP_LEN = 32
CMP_STRIDE = 16
CMP_HIDDEN = 256
SLC_BLOCK = 64
SLC_SHIFT = 6
SLC_TOP = 16
WINDOW = 512
FORCE_SCORE = 1e9

POOL_WINDOWS = (2, 4, 8, 16)
POOL_GROUP = 256
POOL_HALO = 16

PEER_HEADS = 8
N_KEYS = 128
PEER_TOPK = 16
PEER_KK = PEER_HEADS * PEER_TOPK
HALF = D_MODEL // 2
ROW_SUB = HALF // 128

VMEM_LIMIT = 56 * 1024 * 1024


def _cparams(sem, vmem=None):
    return pltpu.CompilerParams(dimension_semantics=sem, vmem_limit_bytes=vmem)


def _rms(x, g):
    return x * lax.rsqrt(jnp.mean(x * x, axis=-1, keepdims=True) + NORM_EPS) * g


def _gelu(x):
    return 0.5 * x * (1.0 + jnp.tanh(0.7978845608028654 * (x + 0.044715 * (x * x * x))))


def _sigmoid(x):
    return 1.0 / (1.0 + jnp.exp(-x))


def _norm_mm_kernel(x_ref, g_ref, *refs, n_w, emit_hn):
    w_refs = refs[:n_w]
    o_refs = refs[n_w:]
    y = _rms(x_ref[...], g_ref[...])
    yb = y.astype(BF16)
    for w_ref, o_ref in zip(w_refs, o_refs[:n_w]):
        o_ref[...] = jnp.dot(yb, w_ref[...], preferred_element_type=F32)
    if emit_hn:
        o_refs[n_w][...] = y[:, :HALF]
        o_refs[n_w + 1][...] = y[:, HALF:]


def norm_matmul(x, g, ws, *, emit_hn=False, tm=512):
    m, d = x.shape
    n_w = len(ws)
    in_specs = [pl.BlockSpec((tm, d), lambda i: (i, 0)), pl.BlockSpec((1, d), lambda i: (0, 0))]
    in_specs += [pl.BlockSpec(w.shape, lambda i: (0, 0)) for w in ws]
    out_shape = [jax.ShapeDtypeStruct((m, w.shape[1]), F32) for w in ws]
    out_specs = [pl.BlockSpec((tm, w.shape[1]), lambda i: (i, 0)) for w in ws]
    if emit_hn:
        out_shape += [jax.ShapeDtypeStruct((m, HALF), F32)] * 2
        out_specs += [pl.BlockSpec((tm, HALF), lambda i: (i, 0))] * 2
    return pl.pallas_call(
        functools.partial(_norm_mm_kernel, n_w=n_w, emit_hn=emit_hn),
        grid=(m // tm,),
        in_specs=in_specs, out_specs=out_specs, out_shape=out_shape,
        compiler_params=_cparams(("parallel",), VMEM_LIMIT),
        name="norm_matmul",
    )(x, g.reshape(1, d), *ws)


def _conv_kernel(cur_ref, halo_ref, w_ref, b_ref, g_ref, beta_ref, o_ref, buf, *, tt):
    t = pl.program_id(1)
    cur = cur_ref[0]
    halo = halo_ref[0]
    a_halo = halo[:, :CONV_CH] * _sigmoid(halo[:, CONV_CH:])
    buf[0:CONV_HALO, :] = jnp.where(t > 0, a_halo, 0.0)
    buf[CONV_HALO:, :] = cur[:, :CONV_CH] * _sigmoid(cur[:, CONV_CH:])
    acc = jnp.zeros((tt, CONV_CH), F32)
    first = CONV_HALO - (CONV_WIDTH - 1)
    for j in range(CONV_WIDTH):
        acc = acc + buf[pl.ds(first + j, tt), :] * w_ref[j:j + 1, :]
    y = acc + b_ref[...]
    mu = jnp.mean(y, axis=-1, keepdims=True)
    yc = y - mu
    var = jnp.mean(yc * yc, axis=-1, keepdims=True)
    y = yc * lax.rsqrt(var + NORM_EPS) * g_ref[...] + beta_ref[...]
    o_ref[0] = y * _sigmoid(y)


def conformer_conv(vg, conv_w, conv_b, ln_g, ln_b, *, tt=512):
    b, t, _ = vg.shape
    tt = min(tt, t)
    per = tt // CONV_HALO
    vec = lambda v: v.reshape(1, CONV_CH)
    vspec = pl.BlockSpec((1, CONV_CH), lambda bi, ti: (0, 0))
    return pl.pallas_call(
        functools.partial(_conv_kernel, tt=tt),
        grid=(b, t // tt),
        in_specs=[
            pl.BlockSpec((1, tt, 2 * CONV_CH), lambda bi, ti: (bi, ti, 0)),
            pl.BlockSpec((1, CONV_HALO, 2 * CONV_CH), lambda bi, ti: (bi, jnp.maximum(ti * per - 1, 0), 0)),
            pl.BlockSpec((CONV_WIDTH, CONV_CH), lambda bi, ti: (0, 0)),
            vspec, vspec, vspec,
        ],
        out_specs=pl.BlockSpec((1, tt, CONV_CH), lambda bi, ti: (bi, ti, 0)),
        out_shape=jax.ShapeDtypeStruct((b, t, CONV_CH), F32),
        scratch_shapes=[pltpu.VMEM((tt + CONV_HALO, CONV_CH), F32)],
        compiler_params=_cparams(("parallel", "parallel"), VMEM_LIMIT),
        name="conformer_conv",
    )(vg, vg, conv_w, vec(conv_b), vec(ln_g), vec(ln_b))


def _compress_kernel(c_ref, pos_ref, w1_ref, w2_ref, o_ref):
    c = c_ref[0, 0]
    pos = pos_ref[0]
    w1 = w1_ref[0]
    half = CMP_STRIDE * HEAD_DIM
    ua = jnp.dot(c + pos[:, :half], w1[:half], preferred_element_type=F32, precision=HIGHEST)
    ub = jnp.dot(c + pos[:, half:], w1[half:], preferred_element_type=F32, precision=HIGHEST)
    n = c.shape[0]
    hid = _gelu(ua + pltpu.roll(ub, n - 1, axis=0))
    o_ref[0, 0] = jnp.dot(hid, w2_ref[0], preferred_element_type=F32, precision=HIGHEST)


def compress_blocks(src, pos, w1, w2):
    _, bg, n_chunk, width = src.shape
    return pl.pallas_call(
        _compress_kernel,
        grid=(2, bg),
        in_specs=[
            pl.BlockSpec((1, 1, n_chunk, width), lambda k, i: (k, i, 0, 0)),
            pl.BlockSpec((1, 1, 2 * width), lambda k, i: (k, 0, 0)),
            pl.BlockSpec((1, 2 * width, CMP_HIDDEN), lambda k, i: (k, 0, 0)),
            pl.BlockSpec((1, CMP_HIDDEN, HEAD_DIM), lambda k, i: (k, 0, 0)),
        ],
        out_specs=pl.BlockSpec((1, 1, n_chunk, HEAD_DIM), lambda k, i: (k, i, 0, 0)),
        out_shape=jax.ShapeDtypeStruct((2, bg, n_chunk, HEAD_DIM), F32),
        compiler_params=_cparams(("parallel", "parallel"), VMEM_LIMIT),
        name="nsa_compress",
    )(src, pos.reshape(2, 1, 2 * width), w1, w2)


def _topk_rows(work, k):
    n = work.shape[0]
    row = lax.broadcasted_iota(jnp.int32, work.shape, 0).astype(F32)
    vals, idxs = [], []
    for _ in range(k):
        m = jnp.max(work, axis=0, keepdims=True)
        first = jnp.min(jnp.where(work == m, row, float(n)), axis=0, keepdims=True)
        vals.append(m)
        idxs.append(first)
        work = jnp.where(row == first, -jnp.inf, work)
    return vals, idxs


def _masked_flash(qb, k_ref, v_ref, lo, hi, tk, mask_fn):
    rows = qb.shape[0]
    tq = rows // NSA_HPG
    scale = HEAD_DIM ** -0.5

    def body(j, carry):
        m, l, acc = carry
        off = pl.multiple_of(j * tk, tk)
        kt = k_ref[0, 0, pl.ds(off, tk), :]
        vt = v_ref[0, 0, pl.ds(off, tk), :]
        s = lax.dot_general(qb, kt, (((1,), (1,)), ((), ())), preferred_element_type=F32) * scale
        ok = mask_fn(j)[None]
        s = jnp.where(ok, s.reshape(NSA_HPG, tq, tk), NEG_INF).reshape(rows, tk)
        m_new = jnp.maximum(m, jnp.max(s, axis=-1, keepdims=True))
        alpha = jnp.exp(m - m_new)
        p = jnp.where(ok, jnp.exp(s - m_new).reshape(NSA_HPG, tq, tk), 0.0).reshape(rows, tk)
        l = alpha * l + jnp.sum(p, axis=-1, keepdims=True)
        acc = alpha * acc + jnp.dot(p.astype(BF16), vt, preferred_element_type=F32)
        return m_new, l, acc

    init = (jnp.full((rows, 1), NEG_INF, F32), jnp.zeros((rows, 1), F32), jnp.zeros((rows, HEAD_DIM), F32))
    _, l, acc = lax.fori_loop(lo, hi, body, init)
    return acc / l


def _nsa_kernel(q_ref, kc_ref, vc_ref, ks_ref, vs_ref, kw_ref, vw_ref, g_ref, ov_ref, o_ref, *, tq, tks, tkw):
    i = pl.program_id(2)
    t0 = i * tq
    rows = NSA_HPG * tq
    scale = HEAD_DIM ** -0.5
    q = q_ref[0, 0].reshape(rows, HEAD_DIM)
    qb = q.astype(BF16)
    t_q = t0 + lax.broadcasted_iota(jnp.int32, (tq, 1), 0)
    t_all = jnp.concatenate([t_q] * NSA_HPG, axis=0)

    kc = kc_ref[0, 0]
    n_cmp = kc.shape[0]
    s = lax.dot_general(q, kc, (((1,), (1,)), ((), ())), preferred_element_type=F32, precision=HIGHEST) * scale
    cmp_end = lax.broadcasted_iota(jnp.int32, (1, n_cmp), 1) * CMP_STRIDE + (CMP_LEN - 1)
    ok = cmp_end <= t_all
    s = jnp.where(ok, s, NEG_INF)
    e = jnp.where(ok, jnp.exp(s - jnp.max(s, axis=-1, keepdims=True)), 0.0)
    den = jnp.sum(e, axis=-1, keepdims=True)
    p_cmp = e / jnp.where(den > 0.0, den, 1.0)
    o_cmp = jnp.dot(p_cmp.astype(BF16), vc_ref[0, 0].astype(BF16), preferred_element_type=F32)

    p_sum = p_cmp[0:tq]
    for h in range(1, NSA_HPG):
        p_sum = p_sum + p_cmp[h * tq:(h + 1) * tq]
    imp = jnp.dot(p_sum, ov_ref[...], preferred_element_type=F32, precision=HIGHEST)
    n_slc = ks_ref.shape[2] // SLC_BLOCK
    imp_t = imp.T[0:n_slc]
    blk = lax.broadcasted_iota(jnp.int32, (n_slc, tq), 0)
    t_lane = t0 + lax.broadcasted_iota(jnp.int32, (n_slc, tq), 1)
    cur = t_lane >> SLC_SHIFT
    forced = (blk == 0) | (blk == cur) | (blk == cur - 1)
    imp_t = jnp.where(forced, FORCE_SCORE, imp_t)
    imp_t = jnp.where(blk * SLC_BLOCK <= t_lane, imp_t, NEG_INF)
    _, picks = _topk_rows(imp_t, min(SLC_TOP, n_slc))
    blk_f = blk.astype(F32)
    member = jnp.zeros((n_slc, tq), F32)
    for pk in picks:
        member = jnp.where(blk_f == pk, 1.0, member)
    if n_slc < 128:
        member = jnp.concatenate([member, jnp.zeros((128 - n_slc, tq), F32)], axis=0)
    member_q = member.T.astype(BF16)

    blocks_per_tile = tks // SLC_BLOCK

    def slc_mask(j):
        sel_row = lax.broadcasted_iota(jnp.int32, (128, tks), 0)
        key_blk = j * blocks_per_tile + (lax.broadcasted_iota(jnp.int32, (128, tks), 1) >> SLC_SHIFT)
        expand = jnp.where(sel_row == key_blk, 1.0, 0.0).astype(BF16)
        sel = jnp.dot(member_q, expand, preferred_element_type=F32) > 0.5
        kpos = j * tks + lax.broadcasted_iota(jnp.int32, (1, tks), 1)
        return sel & (kpos <= t_q)

    o_slc = _masked_flash(qb, ks_ref, vs_ref, 0, (t0 + tq + tks - 1) // tks, tks, slc_mask)

    def win_mask(j):
        kpos = j * tkw + lax.broadcasted_iota(jnp.int32, (1, tkw), 1)
        dist = t_q - kpos
        return (dist >= 0) & (dist < WINDOW)

    lo = jnp.maximum(t0 - (WINDOW - 1), 0) // tkw
    o_win = _masked_flash(qb, kw_ref, vw_ref, lo, (t0 + tq + tkw - 1) // tkw, tkw, win_mask)

    gate = _sigmoid(g_ref[0, 0])
    for h in range(NSA_HPG):
        r = slice(h * tq, (h + 1) * tq)
        o_ref[0, 0, h] = (gate[:, 3 * h:3 * h + 1] * o_cmp[r] + gate[:, 3 * h + 1:3 * h + 2] * o_slc[r]
                          + gate[:, 3 * h + 2:3 * h + 3] * o_win[r])


def nsa_attention(qh, kc, vc, ks, vs, kw, vw, gates, *, tq=128, tks=256, tkw=128):
    b, g, _, t, _ = qh.shape
    n_chunk = kc.shape[2]
    n_slc = t // SLC_BLOCK
    n_cmp = (t - CMP_LEN) // CMP_STRIDE + 1
    tks = min(tks, t)
    cmp_start = np.arange(n_chunk) * CMP_STRIDE
    slc_start = np.arange(128) * SLC_BLOCK
    overlap = ((cmp_start[:, None] < slc_start[None, :] + SLC_BLOCK)
               & (cmp_start[:, None] + CMP_LEN > slc_start[None, :])
               & (np.arange(n_chunk)[:, None] < n_cmp) & (np.arange(128)[None, :] < n_slc))
    overlap = jnp.asarray(overlap.astype(np.float32))
    full = lambda rows: pl.BlockSpec((1, 1, rows, HEAD_DIM), lambda bi, gi, i: (bi, gi, 0, 0))
    return pl.pallas_call(
        functools.partial(_nsa_kernel, tq=tq, tks=tks, tkw=tkw),
        grid=(b, g, t // tq),
        in_specs=[
            pl.BlockSpec((1, 1, NSA_HPG, tq, HEAD_DIM), lambda bi, gi, i: (bi, gi, 0, i, 0)),
            full(n_chunk), full(n_chunk), full(t), full(t), full(t), full(t),
            pl.BlockSpec((1, 1, tq, 3 * NSA_HPG), lambda bi, gi, i: (bi, gi, i, 0)),
            pl.BlockSpec((n_chunk, 128), lambda bi, gi, i: (0, 0)),
        ],
        out_specs=pl.BlockSpec((1, 1, NSA_HPG, tq, HEAD_DIM), lambda bi, gi, i: (bi, gi, 0, i, 0)),
        out_shape=jax.ShapeDtypeStruct(qh.shape, F32),
        compiler_params=_cparams(("parallel", "parallel", "arbitrary"), VMEM_LIMIT),
        name="nsa_attention",
    )(qh, kc, vc, ks, vs, kw, vw, gates, overlap)


def _mm2_res_kernel(h_ref, a_ref, b_ref, wa_ref, wb_ref, o_ref):
    acc = jnp.dot(a_ref[...].astype(BF16), wa_ref[...], preferred_element_type=F32)
    acc = acc + jnp.dot(b_ref[...].astype(BF16), wb_ref[...], preferred_element_type=F32)
    o_ref[...] = h_ref[...] + acc


def mm2_residual(h, a, b, wa, wb, *, tm=512):
    m, d = h.shape
    row = lambda w: pl.BlockSpec((tm, w), lambda i: (i, 0))
    whole = lambda w: pl.BlockSpec(w.shape, lambda i: (0, 0))
    return pl.pallas_call(
        _mm2_res_kernel,
        grid=(m // tm,),
        in_specs=[row(d), row(a.shape[1]), row(b.shape[1]), whole(wa), whole(wb)],
        out_specs=row(d),
        out_shape=jax.ShapeDtypeStruct((m, d), F32),
        compiler_params=_cparams(("parallel",), VMEM_LIMIT),
        name="out_proj_residual",
    )(h, a, b, wa, wb)


def _pool_kernel(cur_ref, halo_ref, g_ref, w_ref, sc_ref, o_ref, buf, *, tt):
    t = pl.program_id(1)
    g = g_ref[...]
    cur = cur_ref[0]
    hn = _rms(cur, g)
    buf[0:POOL_HALO, :] = jnp.where(t > 0, _rms(halo_ref[0], g), 0.0)
    buf[POOL_HALO:, :] = hn
    pos = t * tt + lax.broadcasted_iota(jnp.int32, (tt, 1), 0)
    outs = []
    for gi, w in enumerate(POOL_WINDOWS):
        cols = slice(gi * POOL_GROUP, (gi + 1) * POOL_GROUP)
        tot = hn[:, cols]
        for j in range(1, w):
            tot = tot + buf[pl.ds(POOL_HALO - j, tt), cols]
        cnt = jnp.minimum(pos + 1, w).astype(F32)
        d = tot / cnt - hn[:, cols]
        outs.append(jnp.dot(d.astype(BF16), w_ref[gi], preferred_element_type=F32))
    y = jnp.concatenate(outs, axis=-1) * sc_ref[...]
    o_ref[0] = cur + y


def pool_mixer_residual(h, g, pool_w, pool_scale, *, tt=512):
    b, t, d = h.shape
    tt = min(tt, t)
    per = tt // POOL_HALO
    return pl.pallas_call(
        functools.partial(_pool_kernel, tt=tt),
        grid=(b, t // tt),
        in_specs=[
            pl.BlockSpec((1, tt, d), lambda bi, ti: (bi, ti, 0)),
            pl.BlockSpec((1, POOL_HALO, d), lambda bi, ti: (bi, jnp.maximum(ti * per - 1, 0), 0)),
            pl.BlockSpec((1, d), lambda bi, ti: (0, 0)),
            pl.BlockSpec(pool_w.shape, lambda bi, ti: (0, 0, 0)),
            pl.BlockSpec((1, d), lambda bi, ti: (0, 0)),
        ],
        out_specs=pl.BlockSpec((1, tt, d), lambda bi, ti: (bi, ti, 0)),
        out_shape=jax.ShapeDtypeStruct(h.shape, F32),
        scratch_shapes=[pltpu.VMEM((tt + POOL_HALO, d), F32)],
        compiler_params=_cparams(("parallel", "parallel"), VMEM_LIMIT),
        name="pool_mixer",
    )(h, h, g.reshape(1, d), pool_w.astype(BF16), pool_scale.reshape(1, d))


def _peer_route_kernel(q_ref, key_ref, idx_ref, gate_ref):
    q = q_ref[...]
    s = lax.dot_general(key_ref[0], q, (((1,), (1,)), ((), ())), preferred_element_type=F32,
                        precision=HIGHEST)
    v0, i0 = _topk_rows(s[0:N_KEYS], PEER_TOPK)
    v1, i1 = _topk_rows(s[N_KEYS:], PEER_TOPK)
    s1 = jnp.concatenate(v1, axis=0)
    cand = jnp.concatenate([v0[a] + s1 for a in range(PEER_TOPK)], axis=0)
    cv, ci = _topk_rows(cand, PEER_TOPK)
    c_top = jnp.concatenate(cv, axis=0)
    c_idx = jnp.concatenate(ci, axis=0).astype(jnp.int32)
    a_sel = c_idx >> 4
    b_sel = c_idx & (PEER_TOPK - 1)
    k1 = jnp.zeros_like(c_top)
    k2 = jnp.zeros_like(c_top)
    for r in range(PEER_TOPK):
        k1 = jnp.where(a_sel == r, i0[r], k1)
        k2 = jnp.where(b_sel == r, i1[r], k2)
    idx_ref[0] = (k1 * N_KEYS + k2).astype(jnp.int32)
    e = jnp.exp(c_top - jnp.max(c_top, axis=0, keepdims=True))
    gate_ref[0] = e / jnp.sum(e, axis=0, keepdims=True)


def peer_route(q, keys_bd, *, tb=256):
    m = q.shape[0]
    shp = (PEER_HEADS, PEER_TOPK, m)
    ospec = pl.BlockSpec((1, PEER_TOPK, tb), lambda i, h: (h, 0, i))
    return pl.pallas_call(
        _peer_route_kernel,
        grid=(m // tb, PEER_HEADS),
        in_specs=[pl.BlockSpec((tb, 2 * HEAD_DIM), lambda i, h: (i, h)),
                  pl.BlockSpec((1, 2 * N_KEYS, 2 * HEAD_DIM), lambda i, h: (h, 0, 0))],
        out_specs=[ospec, ospec],
        out_shape=[jax.ShapeDtypeStruct(shp, jnp.int32), jax.ShapeDtypeStruct(shp, F32)],
        compiler_params=_cparams(("parallel", "parallel"), VMEM_LIMIT),
        name="peer_route",
    )(q, keys_bd)


def pack_table(tab):
    bits = lax.bitcast_convert_type(tab.astype(BF16), jnp.uint16).astype(jnp.uint32)
    packed = (bits[:, :HALF] << 16) | bits[:, HALF:]
    return packed.reshape(tab.shape[0], ROW_SUB, 128)


STAGE_ROWS = PEER_KK * ROW_SUB
SLOT = 2 * ROW_SUB


def split_slots(a):
    m = a.shape[0]
    hi = a[:, :HALF].reshape(m, ROW_SUB, 1, 128)
    lo = a[:, HALF:].reshape(m, ROW_SUB, 1, 128)
    return jnp.concatenate([lo, hi], axis=2).reshape(m, SLOT, 128)


def merge_slots(y):
    m = y.shape[0]
    y = y.reshape(m, ROW_SUB, 2, 128)
    return y[:, :, 1].reshape(m, HALF), y[:, :, 0].reshape(m, HALF)


def _two_bf16(a, axis):
    hi = a.astype(BF16)
    lo = (a - hi.astype(F32)).astype(BF16)
    return jnp.concatenate([hi, lo], axis=axis)


def _slot_mask():
    j = lax.broadcasted_iota(jnp.int32, (SLOT, 2 * STAGE_ROWS), 1)
    r = lax.broadcasted_iota(jnp.int32, (SLOT, 2 * STAGE_ROWS), 0)
    return (j & (SLOT - 1)) == r


def _load_table_and_ids(tab_hbm, tab_vmem, idx_hbm, idx_smem, sems, tb):
    i = pl.program_id(0)

    @pl.when(i == 0)
    def _():
        cp = pltpu.make_async_copy(tab_hbm, tab_vmem, sems.at[0])
        cp.start()
        cp.wait()

    cp = pltpu.make_async_copy(idx_hbm.at[:, pl.ds(pl.multiple_of(i * tb, tb), tb)], idx_smem, sems.at[1])
    cp.start()
    cp.wait()


def _for_each_token(tab_vmem, idx_smem, stage_a, stage_b, tb, consume):
    def gather(stage, t):
        for k in range(PEER_KK):
            stage[pl.ds(ROW_SUB * k, ROW_SUB), :] = tab_vmem[idx_smem[k, t]]

    gather(stage_a, 0)
    gather(stage_b, 1)

    def pair(i, carry):
        t = 2 * i
        consume(t, pltpu.bitcast(stage_a[...], BF16))
        consume(t + 1, pltpu.bitcast(stage_b[...], BF16))
        gather(stage_a, jnp.minimum(t + 2, tb - 1))
        gather(stage_b, jnp.minimum(t + 3, tb - 1))
        return carry

    lax.fori_loop(0, tb // 2, pair, 0)


def _peer_act_kernel(idx_hbm, tab_hbm, x_ref, gate_ref, fold_ref, coef_ref, tab_vmem, idx_smem, stage_a, stage_b,
                     zbuf, sems, *, tb, chunk):
    _load_table_and_ids(tab_hbm, tab_vmem, idx_hbm, idx_smem, sems, tb)
    mask = _slot_mask()

    def consume(t, rows):
        g = lax.dot_general(x_ref[t], rows, (((1,), (1,)), ((), ())), preferred_element_type=F32)
        zbuf[t] = jnp.where(mask, g[0:SLOT] + g[SLOT:], 0.0)

    _for_each_token(tab_vmem, idx_smem, stage_a, stage_b, tb, consume)

    def finish(c, carry):
        r0 = pl.multiple_of(c * chunk, chunk)
        z = jnp.sum(zbuf[pl.ds(r0, chunk)], axis=1)
        part = jnp.dot(_two_bf16(z, 0), fold_ref[...], preferred_element_type=F32)
        act = part[0:chunk] + part[chunk:]
        coef_ref[pl.ds(r0, chunk), :] = gate_ref[pl.ds(r0, chunk), :] * _gelu(act)
        return carry

    lax.fori_loop(0, tb // chunk, finish, 0)


def peer_coefficients(idx_t, tab, x_slots, gate, *, tb=256, chunk=32):
    m = gate.shape[0]
    n_j = 2 * STAGE_ROWS
    fold = (np.arange(n_j)[:, None] // SLOT == np.arange(PEER_KK)[None, :]).astype(np.float32)
    return pl.pallas_call(
        functools.partial(_peer_act_kernel, tb=tb, chunk=chunk),
        grid=(m // tb,),
        in_specs=[pl.BlockSpec(memory_space=pl.ANY), pl.BlockSpec(memory_space=pl.ANY),
                  pl.BlockSpec((tb, 2 * SLOT, 128), lambda i: (i, 0, 0)),
                  pl.BlockSpec((tb, PEER_KK), lambda i: (i, 0)),
                  pl.BlockSpec((n_j, PEER_KK), lambda i: (0, 0))],
        out_specs=pl.BlockSpec((tb, PEER_KK), lambda i: (i, 0)),
        out_shape=jax.ShapeDtypeStruct((m, PEER_KK), F32),
        scratch_shapes=[pltpu.VMEM(tab.shape, jnp.uint32), pltpu.SMEM((PEER_KK, tb), jnp.int32),
                        pltpu.VMEM((STAGE_ROWS, 128), jnp.uint32), pltpu.VMEM((STAGE_ROWS, 128), jnp.uint32),
                        pltpu.VMEM((tb, SLOT, n_j), F32), pltpu.SemaphoreType.DMA((2,))],
        compiler_params=_cparams(("arbitrary",), VMEM_LIMIT),
        name="peer_coefficients",
    )(idx_t, tab, _two_bf16(x_slots, 1), gate, jnp.asarray(fold, BF16))


def _peer_mix_kernel(idx_hbm, tab_hbm, coef_ref, spread_ref, y_ref, tab_vmem, idx_smem, stage_a, stage_b, lhs, sems,
                     *, tb):
    _load_table_and_ids(tab_hbm, tab_vmem, idx_hbm, idx_smem, sems, tb)
    shape = (SLOT, 2 * STAGE_ROWS)
    sub = lax.broadcasted_iota(jnp.int32, shape, 0)
    col_slot = lax.broadcasted_iota(jnp.int32, shape, 1) & (SLOT - 1)
    low_slot = (sub & (ROW_SUB - 1)) * 2
    keep = (jnp.where(col_slot == low_slot, jnp.uint32(0x0000FFFF), jnp.uint32(0))
            | jnp.where(col_slot == low_slot + 1, jnp.uint32(0xFFFF0000), jnp.uint32(0)))
    first_term = sub < ROW_SUB

    def both_halves(a):
        bits = lax.bitcast_convert_type(a, jnp.uint32)
        return bits | (bits >> 16)

    def prepare(c, carry):
        r0 = pl.multiple_of(c * 8, 8)
        terms = _two_bf16(coef_ref[pl.ds(r0, 8), :], 0)
        wide = jnp.dot(terms, spread_ref[...], preferred_element_type=F32)
        w_hi = both_halves(wide[0:8])
        w_lo = both_halves(wide[8:16])
        for j in range(8):
            word = jnp.where(first_term, jnp.broadcast_to(w_hi[j:j + 1, :], shape),
                             jnp.broadcast_to(w_lo[j:j + 1, :], shape))
            lhs[r0 + j] = word & keep
        return carry

    lax.fori_loop(0, tb // 8, prepare, 0)

    def consume(t, rows):
        y = jnp.dot(pltpu.bitcast(lhs[t], BF16), rows, preferred_element_type=F32)
        y_ref[t] = y[0:SLOT] + y[SLOT:]

    _for_each_token(tab_vmem, idx_smem, stage_a, stage_b, tb, consume)


def peer_mix(idx_t, coef, tab, *, tb=256):
    m = coef.shape[0]
    n_j = 2 * STAGE_ROWS
    spread = (np.arange(PEER_KK)[:, None] == np.arange(n_j)[None, :] // SLOT).astype(np.float32)
    return pl.pallas_call(
        functools.partial(_peer_mix_kernel, tb=tb),
        grid=(m // tb,),
        in_specs=[pl.BlockSpec(memory_space=pl.ANY), pl.BlockSpec(memory_space=pl.ANY),
                  pl.BlockSpec((tb, PEER_KK), lambda i: (i, 0)),
                  pl.BlockSpec((PEER_KK, n_j), lambda i: (0, 0))],
        out_specs=pl.BlockSpec((tb, SLOT, 128), lambda i: (i, 0, 0)),
        out_shape=jax.ShapeDtypeStruct((m, SLOT, 128), F32),
        scratch_shapes=[pltpu.VMEM(tab.shape, jnp.uint32), pltpu.SMEM((PEER_KK, tb), jnp.int32),
                        pltpu.VMEM((STAGE_ROWS, 128), jnp.uint32), pltpu.VMEM((STAGE_ROWS, 128), jnp.uint32),
                        pltpu.VMEM((tb, SLOT, n_j), jnp.uint32), pltpu.SemaphoreType.DMA((2,))],
        compiler_params=_cparams(("arbitrary",), VMEM_LIMIT),
        name="peer_mix",
    )(idx_t, tab, coef, jnp.asarray(spread, BF16))


def _ple_kernel(h_ref, ya_ref, yb_ref, p_ref, g_ref, wg_ref, wp_ref, fg_ref, o_ref, *, final_norm):
    h = h_ref[...] + jnp.concatenate([ya_ref[...], yb_ref[...]], axis=-1)
    gate = _sigmoid(jnp.dot(_rms(h, g_ref[...]).astype(BF16), wg_ref[...], preferred_element_type=F32))
    h = h + jnp.dot(p_ref[...].astype(BF16), wp_ref[...], preferred_element_type=F32) * gate
    if final_norm:
        h = _rms(h, fg_ref[...])
    o_ref[...] = h


def ple_residual(h, ya, yb, p, g, wg, wp, fg, *, final_norm, tm=512):
    m, d = h.shape
    row = lambda w: pl.BlockSpec((tm, w), lambda i: (i, 0))
    whole = lambda a: pl.BlockSpec(a.shape, lambda i: (0, 0))
    g2, fg2 = g.reshape(1, d), fg.reshape(1, d)
    return pl.pallas_call(
        functools.partial(_ple_kernel, final_norm=final_norm),
        grid=(m // tm,),
        in_specs=[row(d), row(HALF), row(HALF), row(p.shape[1]), whole(g2), whole(wg), whole(wp), whole(fg2)],
        out_specs=row(d),
        out_shape=jax.ShapeDtypeStruct((m, d), F32),
        compiler_params=_cparams(("parallel",), VMEM_LIMIT),
        name="ple_residual",
    )(h, ya, yb, p, g2, wg, wp, fg2)


def _mixer_conv_nsa(h, b, t, norm_g, w_in, conv_w, conv_b, ln_g, ln_b, cmp_pos, cmp_w1, cmp_w2, w_out):
    m = b * t
    n_q = NSA_KV_HEADS * NSA_HPG * HEAD_DIM
    kv_w = NSA_KV_HEADS * HEAD_DIM
    c0, c1, c2 = 2 * CONV_CH, 2 * CONV_CH + n_q, 2 * CONV_CH + n_q + 6 * kv_w
    wb = w_in.astype(BF16)
    vg, q, kv, gt = norm_matmul(h, norm_g, [wb[:, :c0], wb[:, c0:c1], wb[:, c1:c2], wb[:, c2:]])
    a_out = conformer_conv(vg.reshape(b, t, c0), conv_w, conv_b, ln_g, ln_b)

    kv = kv.reshape(b, t, 6, NSA_KV_HEADS, HEAD_DIM).transpose(2, 0, 3, 1, 4)
    n_chunk = t // CMP_STRIDE
    cmp_src = kv[0:2].reshape(2, b * NSA_KV_HEADS, n_chunk, CMP_STRIDE * HEAD_DIM)
    cmp = compress_blocks(cmp_src, cmp_pos.reshape(2, CMP_LEN * HEAD_DIM), cmp_w1, cmp_w2)
    cmp = cmp.reshape(2, b, NSA_KV_HEADS, n_chunk, HEAD_DIM)
    kvb = kv[2:].astype(BF16)
    qh = q.reshape(b, t, NSA_KV_HEADS, NSA_HPG, HEAD_DIM).transpose(0, 2, 3, 1, 4)
    gates = gt.reshape(b, t, NSA_KV_HEADS, 3 * NSA_HPG).transpose(0, 2, 1, 3)
    o = nsa_attention(qh, cmp[0], cmp[1], kvb[0], kvb[1], kvb[2], kvb[3], gates)
    b_out = o.transpose(0, 3, 1, 2, 4).reshape(m, n_q)
    wo = w_out.astype(BF16)
    return mm2_residual(h, a_out.reshape(m, CONV_CH), b_out, wo[:CONV_CH], wo[CONV_CH:])


def _peer_ffn(h, norm_g, wq, subkeys, u_tab, v_tab):
    q, xa, xb = norm_matmul(h, norm_g, [wq.astype(BF16)], emit_hn=True)
    zeros = jnp.zeros_like(subkeys[:, 0])
    keys_bd = jnp.concatenate([jnp.concatenate([subkeys[:, 0], zeros], axis=-1),
                               jnp.concatenate([zeros, subkeys[:, 1]], axis=-1)], axis=1)
    idx_t, gate_t = peer_route(q, keys_bd)
    m = h.shape[0]
    idx_t = idx_t.reshape(PEER_KK, m)
    gate = gate_t.reshape(PEER_KK, m).T
    coef = peer_coefficients(idx_t, pack_table(u_tab), split_slots(jnp.concatenate([xa, xb], axis=1)), gate)
    return merge_slots(peer_mix(idx_t, coef, pack_table(v_tab)))


def kernel(x, p, mix_norm, ab_w_in, ab_conv_w, ab_conv_b, ab_conv_ln_g, ab_conv_ln_b, ab_cmp_pos, ab_cmp_w1,
           ab_cmp_w2, ab_w_out, pool_w, pool_scale, ffn_norm, peer_wq, peer_subkeys, peer_u, peer_v, ple_norm,
           ple_gate_w, ple_proj, final_norm):
    b, t, d = x.shape
    m = b * t
    depth = p.shape[0]
    h = x.reshape(m, d)
    for i in range(depth):
        j = i // 2
        if i % 2 == 0:
            h = _mixer_conv_nsa(h, b, t, mix_norm[i], ab_w_in[j], ab_conv_w[j], ab_conv_b[j], ab_conv_ln_g[j],
                                ab_conv_ln_b[j], ab_cmp_pos[j], ab_cmp_w1[j], ab_cmp_w2[j], ab_w_out[j])
        else:
            h = pool_mixer_residual(h.reshape(b, t, d), mix_norm[i], pool_w[j], pool_scale[j]).reshape(m, d)
        ya, yb = _peer_ffn(h, ffn_norm[i], peer_wq[i], peer_subkeys[i], peer_u[i], peer_v[i])
        h = ple_residual(h, ya, yb, p[i].reshape(m, -1), ple_norm[i], ple_gate_w[i].astype(BF16),
                         ple_proj[i].astype(BF16), final_norm, final_norm=(i == depth - 1))
    return h.reshape(b, t, d)
```

```python
import functools

import jax
import jax.numpy as jnp
import numpy as np
from jax import lax
from jax.experimental import pallas as pl
from jax.experimental.pallas import tpu as pltpu

F32 = jnp.float32
BF16 = jnp.bfloat16
HIGHEST = lax.Precision.HIGHEST

D_MODEL = 1024
NORM_EPS = 1e-6
NEG_INF = -1e30

CONV_CH = 512
CONV_WIDTH = 31
CONV_HALO = 32

HEAD_DIM = 64
NSA_KV_HEADS = 2
NSA_HPG = 4
CMP_LEN = 32
CMP_STRIDE = 16
CMP_HIDDEN = 256
SLC_BLOCK = 64
SLC_SHIFT = 6
SLC_TOP = 16
WINDOW = 512
FORCE_SCORE = 1e9

POOL_WINDOWS = (2, 4, 8, 16)
POOL_GROUP = 256
POOL_HALO = 16

PEER_HEADS = 8
N_KEYS = 128
PEER_TOPK = 16
PEER_KK = PEER_HEADS * PEER_TOPK
HALF = D_MODEL // 2
ROW_SUB = HALF // 128

VMEM_LIMIT = 56 * 1024 * 1024


def _cparams(sem, vmem=None):
    return pltpu.CompilerParams(dimension_semantics=sem, vmem_limit_bytes=vmem)


def _rms(x, g):
    return x * lax.rsqrt(jnp.mean(x * x, axis=-1, keepdims=True) + NORM_EPS) * g


def _gelu(x):
    return 0.5 * x * (1.0 + jnp.tanh(0.7978845608028654 * (x + 0.044715 * (x * x * x))))


def _sigmoid(x):
    return 1.0 / (1.0 + jnp.exp(-x))


def _norm_mm_kernel(x_ref, g_ref, *refs, n_w, emit_hn):
    w_refs = refs[:n_w]
    o_refs = refs[n_w:]
    y = _rms(x_ref[...], g_ref[...])
    yb = y.astype(BF16)
    for w_ref, o_ref in zip(w_refs, o_refs[:n_w]):
        o_ref[...] = jnp.dot(yb, w_ref[...], preferred_element_type=F32)
    if emit_hn:
        o_refs[n_w][...] = y[:, :HALF]
        o_refs[n_w + 1][...] = y[:, HALF:]


def norm_matmul(x, g, ws, *, emit_hn=False, tm=512):
    m, d = x.shape
    n_w = len(ws)
    in_specs = [pl.BlockSpec((tm, d), lambda i: (i, 0)), pl.BlockSpec((1, d), lambda i: (0, 0))]
    in_specs += [pl.BlockSpec(w.shape, lambda i: (0, 0)) for w in ws]
    out_shape = [jax.ShapeDtypeStruct((m, w.shape[1]), F32) for w in ws]
    out_specs = [pl.BlockSpec((tm, w.shape[1]), lambda i: (i, 0)) for w in ws]
    if emit_hn:
        out_shape += [jax.ShapeDtypeStruct((m, HALF), F32)] * 2
        out_specs += [pl.BlockSpec((tm, HALF), lambda i: (i, 0))] * 2
    return pl.pallas_call(
        functools.partial(_norm_mm_kernel, n_w=n_w, emit_hn=emit_hn),
        grid=(m // tm,),
        in_specs=in_specs, out_specs=out_specs, out_shape=out_shape,
        compiler_params=_cparams(("parallel",), VMEM_LIMIT),
        name="norm_matmul",
    )(x, g.reshape(1, d), *ws)


def _conv_kernel(cur_ref, halo_ref, w_ref, b_ref, g_ref, beta_ref, o_ref, buf, *, tt):
    t = pl.program_id(1)
    cur = cur_ref[0]
    halo = halo_ref[0]
    a_halo = halo[:, :CONV_CH] * _sigmoid(halo[:, CONV_CH:])
    buf[0:CONV_HALO, :] = jnp.where(t > 0, a_halo, 0.0)
    buf[CONV_HALO:, :] = cur[:, :CONV_CH] * _sigmoid(cur[:, CONV_CH:])
    acc = jnp.zeros((tt, CONV_CH), F32)
    first = CONV_HALO - (CONV_WIDTH - 1)
    for j in range(CONV_WIDTH):
        acc = acc + buf[pl.ds(first + j, tt), :] * w_ref[j:j + 1, :]
    y = acc + b_ref[...]
    mu = jnp.mean(y, axis=-1, keepdims=True)
    yc = y - mu
    var = jnp.mean(yc * yc, axis=-1, keepdims=True)
    y = yc * lax.rsqrt(var + NORM_EPS) * g_ref[...] + beta_ref[...]
    o_ref[0] = y * _sigmoid(y)


def conformer_conv(vg, conv_w, conv_b, ln_g, ln_b, *, tt=512):
    b, t, _ = vg.shape
    tt = min(tt, t)
    per = tt // CONV_HALO
    vec = lambda v: v.reshape(1, CONV_CH)
    vspec = pl.BlockSpec((1, CONV_CH), lambda bi, ti: (0, 0))
    return pl.pallas_call(
        functools.partial(_conv_kernel, tt=tt),
        grid=(b, t // tt),
        in_specs=[
            pl.BlockSpec((1, tt, 2 * CONV_CH), lambda bi, ti: (bi, ti, 0)),
            pl.BlockSpec((1, CONV_HALO, 2 * CONV_CH), lambda bi, ti: (bi, jnp.maximum(ti * per - 1, 0), 0)),
            pl.BlockSpec((CONV_WIDTH, CONV_CH), lambda bi, ti: (0, 0)),
            vspec, vspec, vspec,
        ],
        out_specs=pl.BlockSpec((1, tt, CONV_CH), lambda bi, ti: (bi, ti, 0)),
        out_shape=jax.ShapeDtypeStruct((b, t, CONV_CH), F32),
        scratch_shapes=[pltpu.VMEM((tt + CONV_HALO, CONV_CH), F32)],
        compiler_params=_cparams(("parallel", "parallel"), VMEM_LIMIT),
        name="conformer_conv",
    )(vg, vg, conv_w, vec(conv_b), vec(ln_g), vec(ln_b))


def _compress_kernel(c_ref, pos_ref, w1_ref, w2_ref, o_ref):
    c = c_ref[0, 0]
    pos = pos_ref[0]
    w1 = w1_ref[0]
    half = CMP_STRIDE * HEAD_DIM
    ua = jnp.dot(c + pos[:, :half], w1[:half], preferred_element_type=F32, precision=HIGHEST)
    ub = jnp.dot(c + pos[:, half:], w1[half:], preferred_element_type=F32, precision=HIGHEST)
    n = c.shape[0]
    hid = _gelu(ua + pltpu.roll(ub, n - 1, axis=0))
    o_ref[0, 0] = jnp.dot(hid, w2_ref[0], preferred_element_type=F32, precision=HIGHEST)


def compress_blocks(src, pos, w1, w2):
    _, bg, n_chunk, width = src.shape
    return pl.pallas_call(
        _compress_kernel,
        grid=(2, bg),
        in_specs=[
            pl.BlockSpec((1, 1, n_chunk, width), lambda k, i: (k, i, 0, 0)),
            pl.BlockSpec((1, 1, 2 * width), lambda k, i: (k, 0, 0)),
            pl.BlockSpec((1, 2 * width, CMP_HIDDEN), lambda k, i: (k, 0, 0)),
            pl.BlockSpec((1, CMP_HIDDEN, HEAD_DIM), lambda k, i: (k, 0, 0)),
        ],
        out_specs=pl.BlockSpec((1, 1, n_chunk, HEAD_DIM), lambda k, i: (k, i, 0, 0)),
        out_shape=jax.ShapeDtypeStruct((2, bg, n_chunk, HEAD_DIM), F32),
        compiler_params=_cparams(("parallel", "parallel"), VMEM_LIMIT),
        name="nsa_compress",
    )(src, pos.reshape(2, 1, 2 * width), w1, w2)


def _topk_rows(work, k, row=None):
    if row is None:
        row = lax.broadcasted_iota(jnp.int32, work.shape, 0).astype(F32)
    else:
        row = jnp.broadcast_to(row, work.shape)
    vals, idxs = [], []
    for _ in range(k):
        m = jnp.max(work, axis=0, keepdims=True)
        first = jnp.min(jnp.where(work == m, row, 3.0e38), axis=0, keepdims=True)
        vals.append(m)
        idxs.append(first)
        work = jnp.where(row == first, -jnp.inf, work)
    return vals, idxs


def _masked_flash(qb, k_ref, v_ref, lo, hi, tk, mask_fn):
    rows = qb.shape[0]
    tq = rows // NSA_HPG
    width = v_ref.shape[-1]

    def body(j, carry):
        m, acc = carry
        off = pl.multiple_of(j * tk, tk)
        kt = k_ref[0, 0, pl.ds(off, tk), :]
        vt = v_ref[0, 0, pl.ds(off, tk), :]
        s = lax.dot_general(qb, kt, (((1,), (1,)), ((), ())), preferred_element_type=F32)
        s = (s.reshape(NSA_HPG, tq, tk) + mask_fn(j)[None]).reshape(rows, tk)
        m_new = jnp.maximum(m, jnp.max(s, axis=-1, keepdims=True))
        p = jnp.exp(s - m_new)
        acc = jnp.exp(m - m_new) * acc + jnp.dot(p.astype(BF16), vt, preferred_element_type=F32)
        return m_new, acc

    init = (jnp.full((rows, 1), NEG_INF, F32), jnp.zeros((rows, width), F32))
    _, acc = lax.fori_loop(lo, hi, body, init)
    return acc[:, :HEAD_DIM] / acc[:, HEAD_DIM:HEAD_DIM + 1]


def _nsa_kernel(q_ref, kc_ref, vc_ref, ks_ref, vs_ref, kw_ref, vw_ref, g_ref, ov_ref, o_ref, *, tq, tks, tkw):
    i = pl.program_id(2)
    t0 = i * tq
    rows = NSA_HPG * tq
    q = q_ref[0, 0].reshape(rows, HEAD_DIM) * (HEAD_DIM ** -0.5)
    qb = q.astype(BF16)
    t_q = t0 + lax.broadcasted_iota(jnp.int32, (tq, 1), 0)
    t_all = jnp.concatenate([t_q] * NSA_HPG, axis=0)

    kc = kc_ref[0, 0]
    n_cmp = kc.shape[0]
    s = lax.dot_general(q, kc, (((1,), (1,)), ((), ())), preferred_element_type=F32, precision=HIGHEST)
    cmp_end = lax.broadcasted_iota(jnp.int32, (1, n_cmp), 1) * CMP_STRIDE + (CMP_LEN - 1)
    ok = cmp_end <= t_all
    s = jnp.where(ok, s, NEG_INF)
    e = jnp.where(ok, jnp.exp(s - jnp.max(s, axis=-1, keepdims=True)), 0.0)
    den = jnp.sum(e, axis=-1, keepdims=True)
    p_cmp = e / jnp.where(den > 0.0, den, 1.0)
    o_cmp = jnp.dot(p_cmp.astype(BF16), vc_ref[0, 0].astype(BF16), preferred_element_type=F32)

    p_sum = p_cmp[0:tq]
    for h in range(1, NSA_HPG):
        p_sum = p_sum + p_cmp[h * tq:(h + 1) * tq]
    imp = jnp.dot(p_sum, ov_ref[...], preferred_element_type=F32, precision=HIGHEST)
    n_slc = ks_ref.shape[2] // SLC_BLOCK
    imp_t = imp.T[0:n_slc]
    blk = lax.broadcasted_iota(jnp.int32, (n_slc, tq), 0)
    t_lane = t0 + lax.broadcasted_iota(jnp.int32, (n_slc, tq), 1)
    cur = t_lane >> SLC_SHIFT
    forced = (blk == 0) | (blk == cur) | (blk == cur - 1)
    imp_t = jnp.where(forced, FORCE_SCORE, imp_t)
    imp_t = jnp.where(blk * SLC_BLOCK <= t_lane, imp_t, NEG_INF)
    _, picks = _topk_rows(imp_t, min(SLC_TOP, n_slc))
    blk_f = blk.astype(F32)
    member = jnp.zeros((n_slc, tq), F32)
    for pk in picks:
        member = jnp.where(blk_f == pk, 1.0, member)
    if n_slc < 128:
        member = jnp.concatenate([member, jnp.zeros((128 - n_slc, tq), F32)], axis=0)
    member_q = member.T.astype(BF16)

    blocks_per_tile = tks // SLC_BLOCK

    def slc_mask(j):
        sel_row = lax.broadcasted_iota(jnp.int32, (128, tks), 0)
        key_blk = j * blocks_per_tile + (lax.broadcasted_iota(jnp.int32, (128, tks), 1) >> SLC_SHIFT)
        expand = jnp.where(sel_row == key_blk, 1.0, 0.0).astype(BF16)
        sel = jnp.dot(member_q, expand, preferred_element_type=F32) > 0.5
        kpos = j * tks + lax.broadcasted_iota(jnp.int32, (1, tks), 1)
        return jnp.where(sel & (kpos <= t_q), 0.0, NEG_INF)

    o_slc = _masked_flash(qb, ks_ref, vs_ref, 0, (t0 + tq + tks - 1) // tks, tks, slc_mask)

    def win_mask(j):
        kpos = j * tkw + lax.broadcasted_iota(jnp.int32, (1, tkw), 1)
        dist = t_q - kpos
        return jnp.where((dist >= 0) & (dist < WINDOW), 0.0, NEG_INF)

    lo = jnp.maximum(t0 - (WINDOW - 1), 0) // tkw
    o_win = _masked_flash(qb, kw_ref, vw_ref, lo, (t0 + tq + tkw - 1) // tkw, tkw, win_mask)

    gate = _sigmoid(g_ref[0, 0])
    for h in range(NSA_HPG):
        r = slice(h * tq, (h + 1) * tq)
        o_ref[0, 0, h] = (gate[:, 3 * h:3 * h + 1] * o_cmp[r] + gate[:, 3 * h + 1:3 * h + 2] * o_slc[r]
                          + gate[:, 3 * h + 2:3 * h + 3] * o_win[r])


def nsa_attention(qh, kc, vc, ks, vs, kw, vw, gates, *, tq=128, tks=256, tkw=128):
    b, g, _, t, _ = qh.shape
    n_chunk = kc.shape[2]
    n_slc = t // SLC_BLOCK
    n_cmp = (t - CMP_LEN) // CMP_STRIDE + 1
    tks = min(tks, t)
    cmp_start = np.arange(n_chunk) * CMP_STRIDE
    slc_start = np.arange(128) * SLC_BLOCK
    overlap = ((cmp_start[:, None] < slc_start[None, :] + SLC_BLOCK)
               & (cmp_start[:, None] + CMP_LEN > slc_start[None, :])
               & (np.arange(n_chunk)[:, None] < n_cmp) & (np.arange(128)[None, :] < n_slc))
    overlap = jnp.asarray(overlap.astype(np.float32))
    full = lambda a: pl.BlockSpec((1, 1) + a.shape[2:], lambda bi, gi, i: (bi, gi, 0, 0))
    return pl.pallas_call(
        functools.partial(_nsa_kernel, tq=tq, tks=tks, tkw=tkw),
        grid=(b, g, t // tq),
        in_specs=[
            pl.BlockSpec((1, 1, NSA_HPG, tq, HEAD_DIM), lambda bi, gi, i: (bi, gi, 0, i, 0)),
            full(kc), full(vc), full(ks), full(vs), full(kw), full(vw),
            pl.BlockSpec((1, 1, tq, 3 * NSA_HPG), lambda bi, gi, i: (bi, gi, i, 0)),
            pl.BlockSpec((n_chunk, 128), lambda bi, gi, i: (0, 0)),
        ],
        out_specs=pl.BlockSpec((1, 1, NSA_HPG, tq, HEAD_DIM), lambda bi, gi, i: (bi, gi, 0, i, 0)),
        out_shape=jax.ShapeDtypeStruct(qh.shape, F32),
        compiler_params=_cparams(("parallel", "parallel", "arbitrary"), VMEM_LIMIT),
        name="nsa_attention",
    )(qh, kc, vc, ks, vs, kw, vw, gates, overlap)


def _mm2_res_kernel(h_ref, a_ref, b_ref, wa_ref, wb_ref, o_ref):
    acc = jnp.dot(a_ref[...].astype(BF16), wa_ref[...], preferred_element_type=F32)
    acc = acc + jnp.dot(b_ref[...].astype(BF16), wb_ref[...], preferred_element_type=F32)
    o_ref[...] = h_ref[...] + acc


def mm2_residual(h, a, b, wa, wb, *, tm=512):
    m, d = h.shape
    row = lambda w: pl.BlockSpec((tm, w), lambda i: (i, 0))
    whole = lambda w: pl.BlockSpec(w.shape, lambda i: (0, 0))
    return pl.pallas_call(
        _mm2_res_kernel,
        grid=(m // tm,),
        in_specs=[row(d), row(a.shape[1]), row(b.shape[1]), whole(wa), whole(wb)],
        out_specs=row(d),
        out_shape=jax.ShapeDtypeStruct((m, d), F32),
        compiler_params=_cparams(("parallel",), VMEM_LIMIT),
        name="out_proj_residual",
    )(h, a, b, wa, wb)


def _pool_kernel(cur_ref, halo_ref, g_ref, w_ref, sc_ref, o_ref, buf, *, tt):
    t = pl.program_id(1)
    g = g_ref[...]
    cur = cur_ref[0]
    hn = _rms(cur, g)
    buf[0:POOL_HALO, :] = jnp.where(t > 0, _rms(halo_ref[0], g), 0.0)
    buf[POOL_HALO:, :] = hn
    pos = t * tt + lax.broadcasted_iota(jnp.int32, (tt, 1), 0)
    outs = []
    for gi, w in enumerate(POOL_WINDOWS):
        cols = slice(gi * POOL_GROUP, (gi + 1) * POOL_GROUP)
        tot = hn[:, cols]
        for j in range(1, w):
            tot = tot + buf[pl.ds(POOL_HALO - j, tt), cols]
        cnt = jnp.minimum(pos + 1, w).astype(F32)
        d = tot / cnt - hn[:, cols]
        outs.append(jnp.dot(d.astype(BF16), w_ref[gi], preferred_element_type=F32))
    y = jnp.concatenate(outs, axis=-1) * sc_ref[...]
    o_ref[0] = cur + y


def pool_mixer_residual(h, g, pool_w, pool_scale, *, tt=512):
    b, t, d = h.shape
    tt = min(tt, t)
    per = tt // POOL_HALO
    return pl.pallas_call(
        functools.partial(_pool_kernel, tt=tt),
        grid=(b, t // tt),
        in_specs=[
            pl.BlockSpec((1, tt, d), lambda bi, ti: (bi, ti, 0)),
            pl.BlockSpec((1, POOL_HALO, d), lambda bi, ti: (bi, jnp.maximum(ti * per - 1, 0), 0)),
            pl.BlockSpec((1, d), lambda bi, ti: (0, 0)),
            pl.BlockSpec(pool_w.shape, lambda bi, ti: (0, 0, 0)),
            pl.BlockSpec((1, d), lambda bi, ti: (0, 0)),
        ],
        out_specs=pl.BlockSpec((1, tt, d), lambda bi, ti: (bi, ti, 0)),
        out_shape=jax.ShapeDtypeStruct(h.shape, F32),
        scratch_shapes=[pltpu.VMEM((tt + POOL_HALO, d), F32)],
        compiler_params=_cparams(("parallel", "parallel"), VMEM_LIMIT),
        name="pool_mixer",
    )(h, h, g.reshape(1, d), pool_w.astype(BF16), pool_scale.reshape(1, d))


def _peer_route_kernel(q_ref, key_ref, idx_ref, gate_ref):
    q = q_ref[...]
    s = lax.dot_general(key_ref[0], q, (((1,), (1,)), ((), ())), preferred_element_type=F32,
                        precision=HIGHEST)
    v0, i0 = _topk_rows(s[0:N_KEYS], PEER_TOPK)
    v1, i1 = _topk_rows(s[N_KEYS:], PEER_TOPK)
    s0 = jnp.concatenate(v0, axis=0)
    s1 = jnp.concatenate(v1, axis=0)
    step = lax.broadcasted_iota(jnp.int32, (8, 1), 0)
    stepf = step.astype(F32)
    blocks, codes = [], []
    for a, b0 in ((0, 0), (0, 8), (1, 0), (2, 0), (3, 0)):
        blocks.append(v0[a] + s1[b0:b0 + 8])
        codes.append(stepf + float(a * PEER_TOPK + b0))
    for a0, b in ((8, 0), (0, 0), (0, 1), (0, 2)):
        dup = jnp.where((step < 4) & (a0 == 0), -jnp.inf, 0.0)
        blocks.append(s0[a0:a0 + 8] + v1[b] + dup)
        codes.append((stepf + float(a0)) * PEER_TOPK + float(b))
    cv, ci = _topk_rows(jnp.concatenate(blocks, axis=0), PEER_TOPK, jnp.concatenate(codes, axis=0))
    c_top = jnp.concatenate(cv, axis=0)
    c_idx = jnp.concatenate(ci, axis=0).astype(jnp.int32)
    a_sel = c_idx >> 4
    b_sel = c_idx & (PEER_TOPK - 1)
    k1 = jnp.zeros_like(c_top)
    k2 = jnp.zeros_like(c_top)
    for r in range(PEER_TOPK):
        k1 = jnp.where(a_sel == r, i0[r], k1)
        k2 = jnp.where(b_sel == r, i1[r], k2)
    idx_ref[0] = (k1 * N_KEYS + k2).astype(jnp.int32)
    e = jnp.exp(c_top - jnp.max(c_top, axis=0, keepdims=True))
    gate_ref[0] = e / jnp.sum(e, axis=0, keepdims=True)


def peer_route(q, keys_bd, *, tb=256):
    m = q.shape[0]
    shp = (PEER_HEADS, PEER_TOPK, m)
    ospec = pl.BlockSpec((1, PEER_TOPK, tb), lambda i, h: (h, 0, i))
    return pl.pallas_call(
        _peer_route_kernel,
        grid=(m // tb, PEER_HEADS),
        in_specs=[pl.BlockSpec((tb, 2 * HEAD_DIM), lambda i, h: (i, h)),
                  pl.BlockSpec((1, 2 * N_KEYS, 2 * HEAD_DIM), lambda i, h: (h, 0, 0))],
        out_specs=[ospec, ospec],
        out_shape=[jax.ShapeDtypeStruct(shp, jnp.int32), jax.ShapeDtypeStruct(shp, F32)],
        compiler_params=_cparams(("parallel", "parallel"), VMEM_LIMIT),
        name="peer_route",
    )(q, keys_bd)


def pack_table(tab):
    bits = lax.bitcast_convert_type(tab.astype(BF16), jnp.uint16).astype(jnp.uint32)
    packed = (bits[:, :HALF] << 16) | bits[:, HALF:]
    return packed.reshape(tab.shape[0], ROW_SUB, 128)


STAGE_ROWS = PEER_KK * ROW_SUB
N_STAGE = 2
SLOT = 2 * ROW_SUB


def split_slots(a):
    m = a.shape[0]
    hi = a[:, :HALF].reshape(m, ROW_SUB, 1, 128)
    lo = a[:, HALF:].reshape(m, ROW_SUB, 1, 128)
    return jnp.concatenate([lo, hi], axis=2).reshape(m, SLOT, 128)


def merge_slots(y):
    m = y.shape[0]
    y = y.reshape(m, ROW_SUB, 2, 128)
    return y[:, :, 1].reshape(m, HALF), y[:, :, 0].reshape(m, HALF)


def _two_bf16(a, axis):
    hi = a.astype(BF16)
    lo = (a - hi.astype(F32)).astype(BF16)
    return jnp.concatenate([hi, lo], axis=axis)


def _slot_mask():
    j = lax.broadcasted_iota(jnp.int32, (SLOT, 2 * STAGE_ROWS), 1)
    r = lax.broadcasted_iota(jnp.int32, (SLOT, 2 * STAGE_ROWS), 0)
    return (j & (SLOT - 1)) == r


def _load_table_and_ids(tab_hbm, tab_vmem, idx_hbm, idx_smem, sems):
    i = pl.program_id(0)

    @pl.when(i == 0)
    def _():
        cp = pltpu.make_async_copy(tab_hbm, tab_vmem, sems.at[0])
        cp.start()
        cp.wait()

    copies = [pltpu.make_async_copy(idx_hbm.at[k, i], idx_smem[k], sems.at[1 + k]) for k in range(PEER_KK)]
    for cp in copies:
        cp.start()
    for cp in copies:
        cp.wait()


def _for_each_token(tab_vmem, idx_smem, stages, tb, contract, finish):
    def gather(stage, t):
        for k in range(PEER_KK):
            stage[pl.ds(ROW_SUB * k, ROW_SUB), :] = tab_vmem[idx_smem[k][t]]

    gather(stages[0], 0)
    gather(stages[1], 1)

    def pair(i, carry):
        t = 2 * i
        part_a = contract(t, pltpu.bitcast(stages[0][...], BF16))
        part_b = contract(t + 1, pltpu.bitcast(stages[1][...], BF16))
        finish(t, 0, part_a)
        finish(t + 1, 1, part_b)
        gather(stages[0], jnp.minimum(t + 2, tb - 1))
        gather(stages[1], jnp.minimum(t + 3, tb - 1))
        return carry

    lax.fori_loop(0, tb // 2, pair, 0)


def _peer_act_kernel(idx_hbm, tab_hbm, x_ref, gate_ref, fold_ref, coef_ref, tab_vmem, stages, zbuf, sems,
                     *idx_smem, tb, chunk):
    stages = [stages.at[n] for n in range(N_STAGE)]
    _load_table_and_ids(tab_hbm, tab_vmem, idx_hbm, idx_smem, sems)
    mask = _slot_mask()

    def contract(t, rows):
        return lax.dot_general(x_ref[t], rows, (((1,), (1,)), ((), ())), preferred_element_type=F32)

    def keep(t, pos, g):
        zbuf[t] = jnp.where(mask, g[0:SLOT] + g[SLOT:], 0.0)

    _for_each_token(tab_vmem, idx_smem, stages, tb, contract, keep)

    def finish(c, carry):
        r0 = pl.multiple_of(c * chunk, chunk)
        z = zbuf[pl.ds(r0, chunk)].reshape(chunk * SLOT, 2 * STAGE_ROWS)
        part = jnp.dot(_two_bf16(z, 0), fold_ref[...], preferred_element_type=F32)
        rows = part[0:chunk * SLOT] + part[chunk * SLOT:]
        act = jnp.sum(rows.reshape(chunk, SLOT, PEER_KK), axis=1)
        coef_ref[pl.ds(r0, chunk), :] = gate_ref[pl.ds(r0, chunk), :] * _gelu(act)
        return carry

    lax.fori_loop(0, tb // chunk, finish, 0)


def peer_coefficients(idx_t, tab, x_slots, gate, *, tb=256, chunk=32):
    m = gate.shape[0]
    n_j = 2 * STAGE_ROWS
    fold = (np.arange(n_j)[:, None] // SLOT == np.arange(PEER_KK)[None, :]).astype(np.float32)
    return pl.pallas_call(
        functools.partial(_peer_act_kernel, tb=tb, chunk=chunk),
        grid=(m // tb,),
        in_specs=[pl.BlockSpec(memory_space=pl.ANY), pl.BlockSpec(memory_space=pl.ANY),
                  pl.BlockSpec((tb, 2 * SLOT, 128), lambda i: (i, 0, 0)),
                  pl.BlockSpec((tb, PEER_KK), lambda i: (i, 0)),
                  pl.BlockSpec((n_j, PEER_KK), lambda i: (0, 0))],
        out_specs=pl.BlockSpec((tb, PEER_KK), lambda i: (i, 0)),
        out_shape=jax.ShapeDtypeStruct((m, PEER_KK), F32),
        scratch_shapes=[pltpu.VMEM(tab.shape, jnp.uint32),
                        pltpu.VMEM((N_STAGE, STAGE_ROWS, 128), jnp.uint32),
                        pltpu.VMEM((tb, SLOT, n_j), F32),
                        pltpu.SemaphoreType.DMA((1 + PEER_KK,))] + [pltpu.SMEM((tb,), jnp.int32)] * PEER_KK,
        compiler_params=_cparams(("arbitrary",), VMEM_LIMIT),
        name="peer_coefficients",
    )(idx_t.reshape(PEER_KK, m // tb, tb), tab, _two_bf16(x_slots, 1), gate, jnp.asarray(fold, BF16))


def _peer_mix_kernel(idx_hbm, tab_hbm, coef_ref, spread_ref, y_ref, tab_vmem, stages, lhs, sems,
                     *idx_smem, tb, chunk):
    stages = [stages.at[n] for n in range(N_STAGE)]
    _load_table_and_ids(tab_hbm, tab_vmem, idx_hbm, idx_smem, sems)
    shape = (SLOT, 2 * STAGE_ROWS)
    sub = lax.broadcasted_iota(jnp.int32, shape, 0)
    col_slot = lax.broadcasted_iota(jnp.int32, shape, 1) & (SLOT - 1)
    low_slot = (sub & (ROW_SUB - 1)) * 2
    keep = (jnp.where(col_slot == low_slot, jnp.uint32(0x0000FFFF), jnp.uint32(0))
            | jnp.where(col_slot == low_slot + 1, jnp.uint32(0xFFFF0000), jnp.uint32(0)))
    first_term = sub < ROW_SUB

    def both_halves(a):
        bits = lax.bitcast_convert_type(a, jnp.uint32)
        return bits | (bits >> 16)

    def prepare(c, carry):
        r0 = pl.multiple_of(c * chunk, chunk)
        terms = _two_bf16(coef_ref[pl.ds(r0, chunk), :], 0)
        wide = jnp.dot(terms, spread_ref[...], preferred_element_type=F32)
        w_hi = both_halves(wide[0:chunk])
        w_lo = both_halves(wide[chunk:])
        for j in range(chunk):
            word = jnp.where(first_term, jnp.broadcast_to(w_hi[j:j + 1, :], shape),
                             jnp.broadcast_to(w_lo[j:j + 1, :], shape))
            lhs[r0 + j] = word & keep
        return carry

    lax.fori_loop(0, tb // chunk, prepare, 0)

    def contract(t, rows):
        return jnp.dot(pltpu.bitcast(lhs[t], BF16), rows, preferred_element_type=F32)

    def store(t, pos, y):
        y_ref[t] = y[0:SLOT] + y[SLOT:]

    _for_each_token(tab_vmem, idx_smem, stages, tb, contract, store)


def peer_mix(idx_t, coef, tab, *, tb=256, chunk=32):
    m = coef.shape[0]
    n_j = 2 * STAGE_ROWS
    spread = (np.arange(PEER_KK)[:, None] == np.arange(n_j)[None, :] // SLOT).astype(np.float32)
    return pl.pallas_call(
        functools.partial(_peer_mix_kernel, tb=tb, chunk=chunk),
        grid=(m // tb,),
        in_specs=[pl.BlockSpec(memory_space=pl.ANY), pl.BlockSpec(memory_space=pl.ANY),
                  pl.BlockSpec((tb, PEER_KK), lambda i: (i, 0)),
                  pl.BlockSpec((PEER_KK, n_j), lambda i: (0, 0))],
        out_specs=pl.BlockSpec((tb, SLOT, 128), lambda i: (i, 0, 0)),
        out_shape=jax.ShapeDtypeStruct((m, SLOT, 128), F32),
        scratch_shapes=[pltpu.VMEM(tab.shape, jnp.uint32),
                        pltpu.VMEM((N_STAGE, STAGE_ROWS, 128), jnp.uint32),
                        pltpu.VMEM((tb, SLOT, n_j), jnp.uint32),
                        pltpu.SemaphoreType.DMA((1 + PEER_KK,))] + [pltpu.SMEM((tb,), jnp.int32)] * PEER_KK,
        compiler_params=_cparams(("arbitrary",), VMEM_LIMIT),
        name="peer_mix",
    )(idx_t.reshape(PEER_KK, m // tb, tb), tab, coef, jnp.asarray(spread, BF16))


def _ple_kernel(h_ref, ya_ref, yb_ref, p_ref, g_ref, wg_ref, wp_ref, fg_ref, o_ref, *, final_norm):
    h = h_ref[...] + jnp.concatenate([ya_ref[...], yb_ref[...]], axis=-1)
    gate = _sigmoid(jnp.dot(_rms(h, g_ref[...]).astype(BF16), wg_ref[...], preferred_element_type=F32))
    h = h + jnp.dot(p_ref[...].astype(BF16), wp_ref[...], preferred_element_type=F32) * gate
    if final_norm:
        h = _rms(h, fg_ref[...])
    o_ref[...] = h


def ple_residual(h, ya, yb, p, g, wg, wp, fg, *, final_norm, tm=512):
    m, d = h.shape
    row = lambda w: pl.BlockSpec((tm, w), lambda i: (i, 0))
    whole = lambda a: pl.BlockSpec(a.shape, lambda i: (0, 0))
    g2, fg2 = g.reshape(1, d), fg.reshape(1, d)
    return pl.pallas_call(
        functools.partial(_ple_kernel, final_norm=final_norm),
        grid=(m // tm,),
        in_specs=[row(d), row(HALF), row(HALF), row(p.shape[1]), whole(g2), whole(wg), whole(wp), whole(fg2)],
        out_specs=row(d),
        out_shape=jax.ShapeDtypeStruct((m, d), F32),
        compiler_params=_cparams(("parallel",), VMEM_LIMIT),
        name="ple_residual",
    )(h, ya, yb, p, g2, wg, wp, fg2)


def _mixer_conv_nsa(h, b, t, norm_g, w_in, conv_w, conv_b, ln_g, ln_b, cmp_pos, cmp_w1, cmp_w2, w_out):
    m = b * t
    n_q = NSA_KV_HEADS * NSA_HPG * HEAD_DIM
    kv_w = NSA_KV_HEADS * HEAD_DIM
    c0, c1, c2 = 2 * CONV_CH, 2 * CONV_CH + n_q, 2 * CONV_CH + n_q + 6 * kv_w
    wb = w_in.astype(BF16)
    vg, q, kv, gt = norm_matmul(h, norm_g, [wb[:, :c0], wb[:, c0:c1], wb[:, c1:c2], wb[:, c2:]])
    a_out = conformer_conv(vg.reshape(b, t, c0), conv_w, conv_b, ln_g, ln_b)

    kv = kv.reshape(b, t, 6, NSA_KV_HEADS, HEAD_DIM).transpose(2, 0, 3, 1, 4)
    n_chunk = t // CMP_STRIDE
    cmp_src = kv[0:2].reshape(2, b * NSA_KV_HEADS, n_chunk, CMP_STRIDE * HEAD_DIM)
    cmp = compress_blocks(cmp_src, cmp_pos.reshape(2, CMP_LEN * HEAD_DIM), cmp_w1, cmp_w2)
    cmp = cmp.reshape(2, b, NSA_KV_HEADS, n_chunk, HEAD_DIM)
    kvb = kv[2:].astype(BF16)
    qh = q.reshape(b, t, NSA_KV_HEADS, NSA_HPG, HEAD_DIM).transpose(0, 2, 3, 1, 4)
    gates = gt.reshape(b, t, NSA_KV_HEADS, 3 * NSA_HPG).transpose(0, 2, 1, 3)
    ones_col = (jnp.arange(HEAD_DIM) == 0).astype(BF16)
    with_ones = lambda v: jnp.concatenate([v, jnp.broadcast_to(ones_col, v.shape)], axis=-1)
    o = nsa_attention(qh, cmp[0], cmp[1], kvb[0], with_ones(kvb[1]), kvb[2], with_ones(kvb[3]), gates)
    b_out = o.transpose(0, 3, 1, 2, 4).reshape(m, n_q)
    wo = w_out.astype(BF16)
    return mm2_residual(h, a_out.reshape(m, CONV_CH), b_out, wo[:CONV_CH], wo[CONV_CH:])


def _peer_ffn(h, norm_g, wq, subkeys, u_tab, v_tab):
    q, xa, xb = norm_matmul(h, norm_g, [wq.astype(BF16)], emit_hn=True)
    zeros = jnp.zeros_like(subkeys[:, 0])
    keys_bd = jnp.concatenate([jnp.concatenate([subkeys[:, 0], zeros], axis=-1),
                               jnp.concatenate([zeros, subkeys[:, 1]], axis=-1)], axis=1)
    idx_t, gate_t = peer_route(q, keys_bd)
    m = h.shape[0]
    idx_t = idx_t.reshape(PEER_KK, m)
    gate = gate_t.reshape(PEER_KK, m).T
    coef = peer_coefficients(idx_t, pack_table(u_tab), split_slots(jnp.concatenate([xa, xb], axis=1)), gate)
    return merge_slots(peer_mix(idx_t, coef, pack_table(v_tab)))


def kernel(x, p, mix_norm, ab_w_in, ab_conv_w, ab_conv_b, ab_conv_ln_g, ab_conv_ln_b, ab_cmp_pos, ab_cmp_w1,
           ab_cmp_w2, ab_w_out, pool_w, pool_scale, ffn_norm, peer_wq, peer_subkeys, peer_u, peer_v, ple_norm,
           ple_gate_w, ple_proj, final_norm):
    b, t, d = x.shape
    m = b * t
    depth = p.shape[0]
    h = x.reshape(m, d)
    for i in range(depth):
        j = i // 2
        if i % 2 == 0:
            h = _mixer_conv_nsa(h, b, t, mix_norm[i], ab_w_in[j], ab_conv_w[j], ab_conv_b[j], ab_conv_ln_g[j],
                                ab_conv_ln_b[j], ab_cmp_pos[j], ab_cmp_w1[j], ab_cmp_w2[j], ab_w_out[j])
        else:
            h = pool_mixer_residual(h.reshape(b, t, d), mix_norm[i], pool_w[j], pool_scale[j]).reshape(m, d)
        ya, yb = _peer_ffn(h, ffn_norm[i], peer_wq[i], peer_subkeys[i], peer_u[i], peer_v[i])
        h = ple_residual(h, ya, yb, p[i].reshape(m, -1), ple_norm[i], ple_gate_w[i].astype(BF16),
                         ple_proj[i].astype(BF16), final_norm, final_norm=(i == depth - 1))
    return h.reshape(b, t, d)
```

```python
import functools

import jax
import jax.numpy as jnp
import numpy as np
from jax import lax
from jax.experimental import pallas as pl
from jax.experimental.pallas import tpu as pltpu

F32 = jnp.float32
BF16 = jnp.bfloat16
HIGHEST = lax.Precision.HIGHEST

D_MODEL = 1024
NORM_EPS = 1e-6
NEG_INF = -1e30

CONV_CH = 512
CONV_WIDTH = 31
CONV_HALO = 32

HEAD_DIM = 64
NSA_KV_HEADS = 2
NSA_HPG = 4
CMP_LEN = 32
CMP_STRIDE = 16
CMP_HIDDEN = 256
SLC_BLOCK = 64
SLC_SHIFT = 6
SLC_TOP = 16
WINDOW = 512
FORCE_SCORE = 1e9

POOL_WINDOWS = (2, 4, 8, 16)
POOL_GROUP = 256
POOL_HALO = 16

PEER_HEADS = 8
N_KEYS = 128
PEER_TOPK = 16
PEER_KK = PEER_HEADS * PEER_TOPK
HALF = D_MODEL // 2
ROW_SUB = HALF // 128

VMEM_LIMIT = 56 * 1024 * 1024


def _cparams(sem, vmem=None):
    return pltpu.CompilerParams(dimension_semantics=sem, vmem_limit_bytes=vmem)


def _rms(x, g):
    return x * lax.rsqrt(jnp.mean(x * x, axis=-1, keepdims=True) + NORM_EPS) * g


def _gelu(x):
    return 0.5 * x * (1.0 + jnp.tanh(0.7978845608028654 * (x + 0.044715 * (x * x * x))))


def _sigmoid(x):
    return 1.0 / (1.0 + jnp.exp(-x))


def _norm_mm_kernel(x_ref, g_ref, *refs, n_w, emit_hn):
    w_refs = refs[:n_w]
    o_refs = refs[n_w:]
    y = _rms(x_ref[...], g_ref[...])
    yb = y.astype(BF16)
    for w_ref, o_ref in zip(w_refs, o_refs[:n_w]):
        o_ref[...] = jnp.dot(yb, w_ref[...], preferred_element_type=F32)
    if emit_hn:
        lo = (y - yb.astype(F32)).astype(BF16)
        blocks = [term[:, _slot_dims(c)] for term in (yb, lo) for c in range(SLOT)]
        o_refs[n_w][...] = jnp.concatenate(blocks, axis=-1)


def norm_matmul(x, g, ws, *, emit_hn=False, tm=512):
    m, d = x.shape
    n_w = len(ws)
    in_specs = [pl.BlockSpec((tm, d), lambda i: (i, 0)), pl.BlockSpec((1, d), lambda i: (0, 0))]
    in_specs += [pl.BlockSpec(w.shape, lambda i: (0, 0)) for w in ws]
    out_shape = [jax.ShapeDtypeStruct((m, w.shape[1]), F32) for w in ws]
    out_specs = [pl.BlockSpec((tm, w.shape[1]), lambda i: (i, 0)) for w in ws]
    if emit_hn:
        out_shape += [jax.ShapeDtypeStruct((m, 2 * d), BF16)]
        out_specs += [pl.BlockSpec((tm, 2 * d), lambda i: (i, 0))]
    return pl.pallas_call(
        functools.partial(_norm_mm_kernel, n_w=n_w, emit_hn=emit_hn),
        grid=(m // tm,),
        in_specs=in_specs, out_specs=out_specs, out_shape=out_shape,
        compiler_params=_cparams(("parallel",), VMEM_LIMIT),
        name="norm_matmul",
    )(x, g.reshape(1, d), *ws)


def _conv_kernel(cur_ref, halo_ref, w_ref, b_ref, g_ref, beta_ref, o_ref, buf, *, tt):
    t = pl.program_id(1)
    cur = cur_ref[0]
    halo = halo_ref[0]
    a_halo = halo[:, :CONV_CH] * _sigmoid(halo[:, CONV_CH:])
    buf[0:CONV_HALO, :] = jnp.where(t > 0, a_halo, 0.0)
    buf[CONV_HALO:, :] = cur[:, :CONV_CH] * _sigmoid(cur[:, CONV_CH:])
    acc = jnp.zeros((tt, CONV_CH), F32)
    first = CONV_HALO - (CONV_WIDTH - 1)
    for j in range(CONV_WIDTH):
        acc = acc + buf[pl.ds(first + j, tt), :] * w_ref[j:j + 1, :]
    y = acc + b_ref[...]
    mu = jnp.mean(y, axis=-1, keepdims=True)
    yc = y - mu
    var = jnp.mean(yc * yc, axis=-1, keepdims=True)
    y = yc * lax.rsqrt(var + NORM_EPS) * g_ref[...] + beta_ref[...]
    o_ref[0] = y * _sigmoid(y)


def conformer_conv(vg, conv_w, conv_b, ln_g, ln_b, *, tt=512):
    b, t, _ = vg.shape
    tt = min(tt, t)
    per = tt // CONV_HALO
    vec = lambda v: v.reshape(1, CONV_CH)
    vspec = pl.BlockSpec((1, CONV_CH), lambda bi, ti: (0, 0))
    return pl.pallas_call(
        functools.partial(_conv_kernel, tt=tt),
        grid=(b, t // tt),
        in_specs=[
            pl.BlockSpec((1, tt, 2 * CONV_CH), lambda bi, ti: (bi, ti, 0)),
            pl.BlockSpec((1, CONV_HALO, 2 * CONV_CH), lambda bi, ti: (bi, jnp.maximum(ti * per - 1, 0), 0)),
            pl.BlockSpec((CONV_WIDTH, CONV_CH), lambda bi, ti: (0, 0)),
            vspec, vspec, vspec,
        ],
        out_specs=pl.BlockSpec((1, tt, CONV_CH), lambda bi, ti: (bi, ti, 0)),
        out_shape=jax.ShapeDtypeStruct((b, t, CONV_CH), F32),
        scratch_shapes=[pltpu.VMEM((tt + CONV_HALO, CONV_CH), F32)],
        compiler_params=_cparams(("parallel", "parallel"), VMEM_LIMIT),
        name="conformer_conv",
    )(vg, vg, conv_w, vec(conv_b), vec(ln_g), vec(ln_b))


def _compress_kernel(c_ref, pos_ref, w1_ref, w2_ref, o_ref):
    c = c_ref[0, 0]
    pos = pos_ref[0]
    w1 = w1_ref[0]
    half = CMP_STRIDE * HEAD_DIM
    ua = jnp.dot(c + pos[:, :half], w1[:half], preferred_element_type=F32, precision=HIGHEST)
    ub = jnp.dot(c + pos[:, half:], w1[half:], preferred_element_type=F32, precision=HIGHEST)
    n = c.shape[0]
    hid = _gelu(ua + pltpu.roll(ub, n - 1, axis=0))
    o_ref[0, 0] = jnp.dot(hid, w2_ref[0], preferred_element_type=F32, precision=HIGHEST)


def compress_blocks(src, pos, w1, w2):
    _, bg, n_chunk, width = src.shape
    return pl.pallas_call(
        _compress_kernel,
        grid=(2, bg),
        in_specs=[
            pl.BlockSpec((1, 1, n_chunk, width), lambda k, i: (k, i, 0, 0)),
            pl.BlockSpec((1, 1, 2 * width), lambda k, i: (k, 0, 0)),
            pl.BlockSpec((1, 2 * width, CMP_HIDDEN), lambda k, i: (k, 0, 0)),
            pl.BlockSpec((1, CMP_HIDDEN, HEAD_DIM), lambda k, i: (k, 0, 0)),
        ],
        out_specs=pl.BlockSpec((1, 1, n_chunk, HEAD_DIM), lambda k, i: (k, i, 0, 0)),
        out_shape=jax.ShapeDtypeStruct((2, bg, n_chunk, HEAD_DIM), F32),
        compiler_params=_cparams(("parallel", "parallel"), VMEM_LIMIT),
        name="nsa_compress",
    )(src, pos.reshape(2, 1, 2 * width), w1, w2)


def _topk_rows(work, k, row=None):
    if row is None:
        row = lax.broadcasted_iota(jnp.int32, work.shape, 0).astype(F32)
    else:
        row = jnp.broadcast_to(row, work.shape)
    vals, idxs = [], []
    for _ in range(k):
        m = jnp.max(work, axis=0, keepdims=True)
        first = jnp.min(jnp.where(work == m, row, 3.0e38), axis=0, keepdims=True)
        vals.append(m)
        idxs.append(first)
        work = jnp.where(row == first, -jnp.inf, work)
    return vals, idxs


def _masked_flash(qb, k_ref, v_ref, lo, hi, tk, mask_fn):
    rows = qb.shape[0]
    tq = rows // NSA_HPG
    width = v_ref.shape[-1]

    def scores(j):
        kt = k_ref[0, 0, pl.ds(pl.multiple_of(j * tk, tk), tk), :]
        s = lax.dot_general(qb, kt, (((1,), (1,)), ((), ())), preferred_element_type=F32)
        return (s.reshape(NSA_HPG, tq, tk) + mask_fn(j)[None]).reshape(rows, tk)

    def body(j, carry):
        m, acc, s = carry
        s_next = scores(jnp.minimum(j + 1, hi - 1))
        vt = v_ref[0, 0, pl.ds(pl.multiple_of(j * tk, tk), tk), :]
        m_new = jnp.maximum(m, jnp.max(s, axis=-1, keepdims=True))
        p = jnp.exp(s - m_new)
        acc = jnp.exp(m - m_new) * acc + jnp.dot(p.astype(BF16), vt, preferred_element_type=F32)
        return m_new, acc, s_next

    init = (jnp.full((rows, 1), NEG_INF, F32), jnp.zeros((rows, width), F32), scores(lo))
    _, acc, _ = lax.fori_loop(lo, hi, body, init)
    return acc[:, :HEAD_DIM] / acc[:, HEAD_DIM:HEAD_DIM + 1]


def _nsa_kernel(q_ref, kc_ref, vc_ref, ks_ref, vs_ref, kw_ref, vw_ref, g_ref, ov_ref, o_ref, *, tq, tks, tkw):
    i = pl.program_id(2)
    t0 = i * tq
    rows = NSA_HPG * tq
    q = q_ref[0, 0].reshape(rows, HEAD_DIM) * (HEAD_DIM ** -0.5)
    qb = q.astype(BF16)
    t_q = t0 + lax.broadcasted_iota(jnp.int32, (tq, 1), 0)
    t_all = jnp.concatenate([t_q] * NSA_HPG, axis=0)

    kc = kc_ref[0, 0]
    n_cmp = kc.shape[0]
    s = lax.dot_general(q, kc, (((1,), (1,)), ((), ())), preferred_element_type=F32, precision=HIGHEST)
    cmp_end = lax.broadcasted_iota(jnp.int32, (1, n_cmp), 1) * CMP_STRIDE + (CMP_LEN - 1)
    ok = cmp_end <= t_all
    s = jnp.where(ok, s, NEG_INF)
    e = jnp.where(ok, jnp.exp(s - jnp.max(s, axis=-1, keepdims=True)), 0.0)
    den = jnp.sum(e, axis=-1, keepdims=True)
    p_cmp = e / jnp.where(den > 0.0, den, 1.0)
    o_cmp = jnp.dot(p_cmp.astype(BF16), vc_ref[0, 0].astype(BF16), preferred_element_type=F32)

    p_sum = p_cmp[0:tq]
    for h in range(1, NSA_HPG):
        p_sum = p_sum + p_cmp[h * tq:(h + 1) * tq]
    imp = jnp.dot(p_sum, ov_ref[...], preferred_element_type=F32, precision=HIGHEST)
    n_slc = ks_ref.shape[2] // SLC_BLOCK
    imp_t = imp.T[0:n_slc]
    blk = lax.broadcasted_iota(jnp.int32, (n_slc, tq), 0)
    t_lane = t0 + lax.broadcasted_iota(jnp.int32, (n_slc, tq), 1)
    cur = t_lane >> SLC_SHIFT
    forced = (blk == 0) | (blk == cur) | (blk == cur - 1)
    imp_t = jnp.where(forced, FORCE_SCORE, imp_t)
    imp_t = jnp.where(blk * SLC_BLOCK <= t_lane, imp_t, NEG_INF)
    _, picks = _topk_rows(imp_t, min(SLC_TOP, n_slc))
    blk_f = blk.astype(F32)
    member = jnp.zeros((n_slc, tq), F32)
    for pk in picks:
        member = jnp.where(blk_f == pk, 1.0, member)
    if n_slc < 128:
        member = jnp.concatenate([member, jnp.zeros((128 - n_slc, tq), F32)], axis=0)
    member_q = member.T.astype(BF16)

    blocks_per_tile = tks // SLC_BLOCK

    def slc_mask(j):
        sel_row = lax.broadcasted_iota(jnp.int32, (128, tks), 0)
        key_blk = j * blocks_per_tile + (lax.broadcasted_iota(jnp.int32, (128, tks), 1) >> SLC_SHIFT)
        expand = jnp.where(sel_row == key_blk, 1.0, 0.0).astype(BF16)
        sel = jnp.dot(member_q, expand, preferred_element_type=F32) > 0.5
        kpos = j * tks + lax.broadcasted_iota(jnp.int32, (1, tks), 1)
        return jnp.where(sel & (kpos <= t_q), 0.0, NEG_INF)

    o_slc = _masked_flash(qb, ks_ref, vs_ref, 0, (t0 + tq + tks - 1) // tks, tks, slc_mask)

    def win_mask(j):
        kpos = j * tkw + lax.broadcasted_iota(jnp.int32, (1, tkw), 1)
        dist = t_q - kpos
        return jnp.where((dist >= 0) & (dist < WINDOW), 0.0, NEG_INF)

    lo = jnp.maximum(t0 - (WINDOW - 1), 0) // tkw
    o_win = _masked_flash(qb, kw_ref, vw_ref, lo, (t0 + tq + tkw - 1) // tkw, tkw, win_mask)

    gate = _sigmoid(g_ref[0, 0])
    for h in range(NSA_HPG):
        r = slice(h * tq, (h + 1) * tq)
        o_ref[0, 0, h] = (gate[:, 3 * h:3 * h + 1] * o_cmp[r] + gate[:, 3 * h + 1:3 * h + 2] * o_slc[r]
                          + gate[:, 3 * h + 2:3 * h + 3] * o_win[r])


def nsa_attention(qh, kc, vc, ks, vs, kw, vw, gates, *, tq=128, tks=256, tkw=128):
    b, g, _, t, _ = qh.shape
    n_chunk = kc.shape[2]
    n_slc = t // SLC_BLOCK
    n_cmp = (t - CMP_LEN) // CMP_STRIDE + 1
    tks = min(tks, t)
    cmp_start = np.arange(n_chunk) * CMP_STRIDE
    slc_start = np.arange(128) * SLC_BLOCK
    overlap = ((cmp_start[:, None] < slc_start[None, :] + SLC_BLOCK)
               & (cmp_start[:, None] + CMP_LEN > slc_start[None, :])
               & (np.arange(n_chunk)[:, None] < n_cmp) & (np.arange(128)[None, :] < n_slc))
    overlap = jnp.asarray(overlap.astype(np.float32))
    full = lambda a: pl.BlockSpec((1, 1) + a.shape[2:], lambda bi, gi, i: (bi, gi, 0, 0))
    return pl.pallas_call(
        functools.partial(_nsa_kernel, tq=tq, tks=tks, tkw=tkw),
        grid=(b, g, t // tq),
        in_specs=[
            pl.BlockSpec((1, 1, NSA_HPG, tq, HEAD_DIM), lambda bi, gi, i: (bi, gi, 0, i, 0)),
            full(kc), full(vc), full(ks), full(vs), full(kw), full(vw),
            pl.BlockSpec((1, 1, tq, 3 * NSA_HPG), lambda bi, gi, i: (bi, gi, i, 0)),
            pl.BlockSpec((n_chunk, 128), lambda bi, gi, i: (0, 0)),
        ],
        out_specs=pl.BlockSpec((1, 1, NSA_HPG, tq, HEAD_DIM), lambda bi, gi, i: (bi, gi, 0, i, 0)),
        out_shape=jax.ShapeDtypeStruct(qh.shape, F32),
        compiler_params=_cparams(("parallel", "parallel", "arbitrary"), VMEM_LIMIT),
        name="nsa_attention",
    )(qh, kc, vc, ks, vs, kw, vw, gates, overlap)


def _mm2_res_kernel(h_ref, a_ref, b_ref, wa_ref, wb_ref, o_ref):
    acc = jnp.dot(a_ref[...].astype(BF16), wa_ref[...], preferred_element_type=F32)
    acc = acc + jnp.dot(b_ref[...].astype(BF16), wb_ref[...], preferred_element_type=F32)
    o_ref[...] = h_ref[...] + acc


def mm2_residual(h, a, b, wa, wb, *, tm=512):
    m, d = h.shape
    row = lambda w: pl.BlockSpec((tm, w), lambda i: (i, 0))
    whole = lambda w: pl.BlockSpec(w.shape, lambda i: (0, 0))
    return pl.pallas_call(
        _mm2_res_kernel,
        grid=(m // tm,),
        in_specs=[row(d), row(a.shape[1]), row(b.shape[1]), whole(wa), whole(wb)],
        out_specs=row(d),
        out_shape=jax.ShapeDtypeStruct((m, d), F32),
        compiler_params=_cparams(("parallel",), VMEM_LIMIT),
        name="out_proj_residual",
    )(h, a, b, wa, wb)


def _pool_kernel(cur_ref, halo_ref, g_ref, w_ref, sc_ref, o_ref, buf, *, tt):
    t = pl.program_id(1)
    g = g_ref[...]
    cur = cur_ref[0]
    hn = _rms(cur, g)
    buf[0:POOL_HALO, :] = jnp.where(t > 0, _rms(halo_ref[0], g), 0.0)
    buf[POOL_HALO:, :] = hn
    pos = t * tt + lax.broadcasted_iota(jnp.int32, (tt, 1), 0)
    outs = []
    for gi, w in enumerate(POOL_WINDOWS):
        cols = slice(gi * POOL_GROUP, (gi + 1) * POOL_GROUP)
        tot = hn[:, cols]
        for j in range(1, w):
            tot = tot + buf[pl.ds(POOL_HALO - j, tt), cols]
        cnt = jnp.minimum(pos + 1, w).astype(F32)
        d = tot / cnt - hn[:, cols]
        outs.append(jnp.dot(d.astype(BF16), w_ref[gi], preferred_element_type=F32))
    y = jnp.concatenate(outs, axis=-1) * sc_ref[...]
    o_ref[0] = cur + y


def pool_mixer_residual(h, g, pool_w, pool_scale, *, tt=512):
    b, t, d = h.shape
    tt = min(tt, t)
    per = tt // POOL_HALO
    return pl.pallas_call(
        functools.partial(_pool_kernel, tt=tt),
        grid=(b, t // tt),
        in_specs=[
            pl.BlockSpec((1, tt, d), lambda bi, ti: (bi, ti, 0)),
            pl.BlockSpec((1, POOL_HALO, d), lambda bi, ti: (bi, jnp.maximum(ti * per - 1, 0), 0)),
            pl.BlockSpec((1, d), lambda bi, ti: (0, 0)),
            pl.BlockSpec(pool_w.shape, lambda bi, ti: (0, 0, 0)),
            pl.BlockSpec((1, d), lambda bi, ti: (0, 0)),
        ],
        out_specs=pl.BlockSpec((1, tt, d), lambda bi, ti: (bi, ti, 0)),
        out_shape=jax.ShapeDtypeStruct(h.shape, F32),
        scratch_shapes=[pltpu.VMEM((tt + POOL_HALO, d), F32)],
        compiler_params=_cparams(("parallel", "parallel"), VMEM_LIMIT),
        name="pool_mixer",
    )(h, h, g.reshape(1, d), pool_w.astype(BF16), pool_scale.reshape(1, d))


def _peer_route_kernel(q_ref, key_ref, idx_ref, gate_ref):
    q = q_ref[...]
    s = lax.dot_general(key_ref[0], q, (((1,), (1,)), ((), ())), preferred_element_type=F32,
                        precision=HIGHEST)
    v0, i0 = _topk_rows(s[0:N_KEYS], PEER_TOPK)
    v1, i1 = _topk_rows(s[N_KEYS:], PEER_TOPK)
    s0 = jnp.concatenate(v0, axis=0)
    s1 = jnp.concatenate(v1, axis=0)
    step = lax.broadcasted_iota(jnp.int32, (8, 1), 0)
    stepf = step.astype(F32)
    blocks, codes = [], []
    for a, b0 in ((0, 0), (0, 8), (1, 0), (2, 0), (3, 0)):
        blocks.append(v0[a] + s1[b0:b0 + 8])
        codes.append(stepf + float(a * PEER_TOPK + b0))
    for a0, b in ((8, 0), (0, 0), (0, 1), (0, 2)):
        dup = jnp.where((step < 4) & (a0 == 0), -jnp.inf, 0.0)
        blocks.append(s0[a0:a0 + 8] + v1[b] + dup)
        codes.append((stepf + float(a0)) * PEER_TOPK + float(b))
    cv, ci = _topk_rows(jnp.concatenate(blocks, axis=0), PEER_TOPK, jnp.concatenate(codes, axis=0))
    c_top = jnp.concatenate(cv, axis=0)
    c_idx = jnp.concatenate(ci, axis=0).astype(jnp.int32)
    a_sel = c_idx >> 4
    b_sel = c_idx & (PEER_TOPK - 1)
    k1 = jnp.zeros_like(c_top)
    k2 = jnp.zeros_like(c_top)
    for r in range(PEER_TOPK):
        k1 = jnp.where(a_sel == r, i0[r], k1)
        k2 = jnp.where(b_sel == r, i1[r], k2)
    idx_ref[0] = ((k1 * N_KEYS + k2) * ROW_SUB).astype(jnp.int32)
    e = jnp.exp(c_top - jnp.max(c_top, axis=0, keepdims=True))
    gate_ref[0] = e / jnp.sum(e, axis=0, keepdims=True)


def peer_route(q, keys_bd, *, tb=512):
    m = q.shape[0]
    shp = (PEER_HEADS, PEER_TOPK, m)
    ospec = pl.BlockSpec((1, PEER_TOPK, tb), lambda i, h: (h, 0, i))
    return pl.pallas_call(
        _peer_route_kernel,
        grid=(m // tb, PEER_HEADS),
        in_specs=[pl.BlockSpec((tb, 2 * HEAD_DIM), lambda i, h: (i, h)),
                  pl.BlockSpec((1, 2 * N_KEYS, 2 * HEAD_DIM), lambda i, h: (h, 0, 0))],
        out_specs=[ospec, ospec],
        out_shape=[jax.ShapeDtypeStruct(shp, jnp.int32), jax.ShapeDtypeStruct(shp, F32)],
        compiler_params=_cparams(("parallel", "parallel"), VMEM_LIMIT),
        name="peer_route",
    )(q, keys_bd)


def pack_table(tab):
    bits = lax.bitcast_convert_type(tab.astype(BF16), jnp.uint16).astype(jnp.uint32)
    packed = (bits[:, :HALF] << 16) | bits[:, HALF:]
    return packed.reshape(tab.shape[0] * ROW_SUB, 128)


STAGE_ROWS = PEER_KK * ROW_SUB
N_STAGE = 2
SLOT = 2 * ROW_SUB


def _slot_dims(c):
    start = (c // 2) * 128 + (0 if c % 2 else HALF)
    return slice(start, start + 128)


def _from_slots(y):
    slot_of = {(_slot_dims(c).start // 128): c for c in range(SLOT)}
    return jnp.concatenate([y[:, slot_of[blk] * 128:(slot_of[blk] + 1) * 128] for blk in range(SLOT)], axis=-1)


def _two_bf16(a, axis):
    hi = a.astype(BF16)
    lo = (a - hi.astype(F32)).astype(BF16)
    return jnp.concatenate([hi, lo], axis=axis)


def _slot_mask():
    j = lax.broadcasted_iota(jnp.int32, (SLOT, 2 * STAGE_ROWS), 1)
    r = lax.broadcasted_iota(jnp.int32, (SLOT, 2 * STAGE_ROWS), 0)
    return (j & (SLOT - 1)) == r


def _load_table_and_ids(tab_hbm, tab_vmem, idx_hbm, idx_smem, sems):
    i = pl.program_id(0)

    @pl.when(i == 0)
    def _():
        cp = pltpu.make_async_copy(tab_hbm, tab_vmem, sems.at[0])
        cp.start()
        cp.wait()

    copies = [pltpu.make_async_copy(idx_hbm.at[k, i], idx_smem[k], sems.at[1 + k]) for k in range(PEER_KK)]
    for cp in copies:
        cp.start()
    for cp in copies:
        cp.wait()


def _for_each_token(tab_vmem, idx_smem, stages, tb, contract, finish):
    def gather(stage, t):
        for k in range(PEER_KK):
            row = pl.multiple_of(idx_smem[k][t], ROW_SUB)
            stage[pl.ds(ROW_SUB * k, ROW_SUB), :] = tab_vmem[pl.ds(row, ROW_SUB), :]

    gather(stages[0], 0)
    gather(stages[1], 1)

    def pair(i, carry):
        t = 2 * i
        part_a = contract(t, pltpu.bitcast(stages[0][...], BF16))
        part_b = contract(t + 1, pltpu.bitcast(stages[1][...], BF16))
        finish(t, 0, part_a)
        finish(t + 1, 1, part_b)
        gather(stages[0], jnp.minimum(t + 2, tb - 1))
        gather(stages[1], jnp.minimum(t + 3, tb - 1))
        return carry

    lax.fori_loop(0, tb // 2, pair, 0)


def _peer_act_kernel(idx_hbm, tab_hbm, x_ref, gate_ref, fold_ref, coef_ref, tab_vmem, stages, zbuf, sems,
                     *idx_smem, tb, chunk):
    stages = [stages.at[n] for n in range(N_STAGE)]
    _load_table_and_ids(tab_hbm, tab_vmem, idx_hbm, idx_smem, sems)
    mask = _slot_mask()

    def contract(t, rows):
        return lax.dot_general(x_ref[t], rows, (((1,), (1,)), ((), ())), preferred_element_type=F32)

    def keep(t, pos, g):
        zbuf[t] = jnp.where(mask, g[0:SLOT] + g[SLOT:], 0.0)

    _for_each_token(tab_vmem, idx_smem, stages, tb, contract, keep)

    def finish(c, carry):
        r0 = pl.multiple_of(c * chunk, chunk)
        z = zbuf[pl.ds(r0, chunk)].reshape(chunk * SLOT, 2 * STAGE_ROWS)
        part = jnp.dot(_two_bf16(z, 0), fold_ref[...], preferred_element_type=F32)
        rows = part[0:chunk * SLOT] + part[chunk * SLOT:]
        act = jnp.sum(rows.reshape(chunk, SLOT, PEER_KK), axis=1)
        coef_ref[pl.ds(r0, chunk), :] = gate_ref[pl.ds(r0, chunk), :] * _gelu(act)
        return carry

    lax.fori_loop(0, tb // chunk, finish, 0)


def peer_coefficients(idx_t, tab, x_terms, gate, *, tb=256, chunk=32):
    m = gate.shape[0]
    n_j = 2 * STAGE_ROWS
    fold = (np.arange(n_j)[:, None] // SLOT == np.arange(PEER_KK)[None, :]).astype(np.float32)
    return pl.pallas_call(
        functools.partial(_peer_act_kernel, tb=tb, chunk=chunk),
        grid=(m // tb,),
        in_specs=[pl.BlockSpec(memory_space=pl.ANY), pl.BlockSpec(memory_space=pl.ANY),
                  pl.BlockSpec((tb, 2 * SLOT, 128), lambda i: (i, 0, 0)),
                  pl.BlockSpec((tb, PEER_KK), lambda i: (i, 0)),
                  pl.BlockSpec((n_j, PEER_KK), lambda i: (0, 0))],
        out_specs=pl.BlockSpec((tb, PEER_KK), lambda i: (i, 0)),
        out_shape=jax.ShapeDtypeStruct((m, PEER_KK), F32),
        scratch_shapes=[pltpu.VMEM(tab.shape, jnp.uint32),
                        pltpu.VMEM((N_STAGE, STAGE_ROWS, 128), jnp.uint32),
                        pltpu.VMEM((tb, SLOT, n_j), F32),
                        pltpu.SemaphoreType.DMA((1 + PEER_KK,))] + [pltpu.SMEM((tb,), jnp.int32)] * PEER_KK,
        compiler_params=_cparams(("arbitrary",), VMEM_LIMIT),
        name="peer_coefficients",
    )(idx_t.reshape(PEER_KK, m // tb, tb), tab, x_terms, gate, jnp.asarray(fold, BF16))


def _peer_mix_kernel(idx_hbm, tab_hbm, coef_ref, spread_ref, y_ref, tab_vmem, stages, lhs, sems,
                     *idx_smem, tb, chunk):
    stages = [stages.at[n] for n in range(N_STAGE)]
    _load_table_and_ids(tab_hbm, tab_vmem, idx_hbm, idx_smem, sems)
    shape = (SLOT, 2 * STAGE_ROWS)
    sub = lax.broadcasted_iota(jnp.int32, shape, 0)
    col_slot = lax.broadcasted_iota(jnp.int32, shape, 1) & (SLOT - 1)
    low_slot = (sub & (ROW_SUB - 1)) * 2
    keep = (jnp.where(col_slot == low_slot, jnp.uint32(0x0000FFFF), jnp.uint32(0))
            | jnp.where(col_slot == low_slot + 1, jnp.uint32(0xFFFF0000), jnp.uint32(0)))
    first_term = sub < ROW_SUB

    def both_halves(a):
        bits = lax.bitcast_convert_type(a, jnp.uint32)
        return bits | (bits >> 16)

    def prepare(c, carry):
        r0 = pl.multiple_of(c * chunk, chunk)
        terms = _two_bf16(coef_ref[pl.ds(r0, chunk), :], 0)
        wide = jnp.dot(terms, spread_ref[...], preferred_element_type=F32)
        w_hi = both_halves(wide[0:chunk])
        w_lo = both_halves(wide[chunk:])
        for j in range(chunk):
            word = jnp.where(first_term, jnp.broadcast_to(w_hi[j:j + 1, :], shape),
                             jnp.broadcast_to(w_lo[j:j + 1, :], shape))
            lhs[r0 + j] = word & keep
        return carry

    lax.fori_loop(0, tb // chunk, prepare, 0)

    def contract(t, rows):
        return jnp.dot(pltpu.bitcast(lhs[t], BF16), rows, preferred_element_type=F32)

    def store(t, pos, y):
        y_ref[t] = y[0:SLOT] + y[SLOT:]

    _for_each_token(tab_vmem, idx_smem, stages, tb, contract, store)


def peer_mix(idx_t, coef, tab, *, tb=256, chunk=32):
    m = coef.shape[0]
    n_j = 2 * STAGE_ROWS
    spread = (np.arange(PEER_KK)[:, None] == np.arange(n_j)[None, :] // SLOT).astype(np.float32)
    return pl.pallas_call(
        functools.partial(_peer_mix_kernel, tb=tb, chunk=chunk),
        grid=(m // tb,),
        in_specs=[pl.BlockSpec(memory_space=pl.ANY), pl.BlockSpec(memory_space=pl.ANY),
                  pl.BlockSpec((tb, PEER_KK), lambda i: (i, 0)),
                  pl.BlockSpec((PEER_KK, n_j), lambda i: (0, 0))],
        out_specs=pl.BlockSpec((tb, SLOT, 128), lambda i: (i, 0, 0)),
        out_shape=jax.ShapeDtypeStruct((m, SLOT, 128), F32),
        scratch_shapes=[pltpu.VMEM(tab.shape, jnp.uint32),
                        pltpu.VMEM((N_STAGE, STAGE_ROWS, 128), jnp.uint32),
                        pltpu.VMEM((tb, SLOT, n_j), jnp.uint32),
                        pltpu.SemaphoreType.DMA((1 + PEER_KK,))] + [pltpu.SMEM((tb,), jnp.int32)] * PEER_KK,
        compiler_params=_cparams(("arbitrary",), VMEM_LIMIT),
        name="peer_mix",
    )(idx_t.reshape(PEER_KK, m // tb, tb), tab, coef, jnp.asarray(spread, BF16))


def _ple_kernel(h_ref, y_ref, p_ref, g_ref, wg_ref, wp_ref, fg_ref, o_ref, *, final_norm):
    h = h_ref[...] + _from_slots(y_ref[...])
    gate = _sigmoid(jnp.dot(_rms(h, g_ref[...]).astype(BF16), wg_ref[...], preferred_element_type=F32))
    h = h + jnp.dot(p_ref[...].astype(BF16), wp_ref[...], preferred_element_type=F32) * gate
    if final_norm:
        h = _rms(h, fg_ref[...])
    o_ref[...] = h


def ple_residual(h, y, p, g, wg, wp, fg, *, final_norm, tm=512):
    m, d = h.shape
    row = lambda w: pl.BlockSpec((tm, w), lambda i: (i, 0))
    whole = lambda a: pl.BlockSpec(a.shape, lambda i: (0, 0))
    g2, fg2 = g.reshape(1, d), fg.reshape(1, d)
    return pl.pallas_call(
        functools.partial(_ple_kernel, final_norm=final_norm),
        grid=(m // tm,),
        in_specs=[row(d), row(d), row(p.shape[1]), whole(g2), whole(wg), whole(wp), whole(fg2)],
        out_specs=row(d),
        out_shape=jax.ShapeDtypeStruct((m, d), F32),
        compiler_params=_cparams(("parallel",), VMEM_LIMIT),
        name="ple_residual",
    )(h, y, p, g2, wg, wp, fg2)


def _mixer_conv_nsa(h, b, t, norm_g, w_in, conv_w, conv_b, ln_g, ln_b, cmp_pos, cmp_w1, cmp_w2, w_out):
    m = b * t
    n_q = NSA_KV_HEADS * NSA_HPG * HEAD_DIM
    kv_w = NSA_KV_HEADS * HEAD_DIM
    c0, c1, c2 = 2 * CONV_CH, 2 * CONV_CH + n_q, 2 * CONV_CH + n_q + 6 * kv_w
    wb = w_in.astype(BF16)
    vg, q, kv, gt = norm_matmul(h, norm_g, [wb[:, :c0], wb[:, c0:c1], wb[:, c1:c2], wb[:, c2:]])
    a_out = conformer_conv(vg.reshape(b, t, c0), conv_w, conv_b, ln_g, ln_b)

    kv = kv.reshape(b, t, 6, NSA_KV_HEADS, HEAD_DIM).transpose(2, 0, 3, 1, 4)
    n_chunk = t // CMP_STRIDE
    cmp_src = kv[0:2].reshape(2, b * NSA_KV_HEADS, n_chunk, CMP_STRIDE * HEAD_DIM)
    cmp = compress_blocks(cmp_src, cmp_pos.reshape(2, CMP_LEN * HEAD_DIM), cmp_w1, cmp_w2)
    cmp = cmp.reshape(2, b, NSA_KV_HEADS, n_chunk, HEAD_DIM)
    kvb = kv[2:].astype(BF16)
    qh = q.reshape(b, t, NSA_KV_HEADS, NSA_HPG, HEAD_DIM).transpose(0, 2, 3, 1, 4)
    gates = gt.reshape(b, t, NSA_KV_HEADS, 3 * NSA_HPG).transpose(0, 2, 1, 3)
    ones_col = (jnp.arange(HEAD_DIM) == 0).astype(BF16)
    with_ones = lambda v: jnp.concatenate([v, jnp.broadcast_to(ones_col, v.shape)], axis=-1)
    o = nsa_attention(qh, cmp[0], cmp[1], kvb[0], with_ones(kvb[1]), kvb[2], with_ones(kvb[3]), gates)
    b_out = o.transpose(0, 3, 1, 2, 4).reshape(m, n_q)
    wo = w_out.astype(BF16)
    return mm2_residual(h, a_out.reshape(m, CONV_CH), b_out, wo[:CONV_CH], wo[CONV_CH:])


def _peer_ffn(h, norm_g, wq, subkeys, u_tab, v_tab):
    q, x_terms = norm_matmul(h, norm_g, [wq.astype(BF16)], emit_hn=True)
    zeros = jnp.zeros_like(subkeys[:, 0])
    keys_bd = jnp.concatenate([jnp.concatenate([subkeys[:, 0], zeros], axis=-1),
                               jnp.concatenate([zeros, subkeys[:, 1]], axis=-1)], axis=1)
    idx_t, gate_t = peer_route(q, keys_bd)
    m = h.shape[0]
    idx_t = idx_t.reshape(PEER_KK, m)
    gate = gate_t.reshape(PEER_KK, m).T
    coef = peer_coefficients(idx_t, pack_table(u_tab), x_terms.reshape(m, 2 * SLOT, 128), gate)
    return peer_mix(idx_t, coef, pack_table(v_tab)).reshape(m, SLOT * 128)


def kernel(x, p, mix_norm, ab_w_in, ab_conv_w, ab_conv_b, ab_conv_ln_g, ab_conv_ln_b, ab_cmp_pos, ab_cmp_w1,
           ab_cmp_w2, ab_w_out, pool_w, pool_scale, ffn_norm, peer_wq, peer_subkeys, peer_u, peer_v, ple_norm,
           ple_gate_w, ple_proj, final_norm):
    b, t, d = x.shape
    m = b * t
    depth = p.shape[0]
    h = x.reshape(m, d)
    for i in range(depth):
        j = i // 2
        if i % 2 == 0:
            h = _mixer_conv_nsa(h, b, t, mix_norm[i], ab_w_in[j], ab_conv_w[j], ab_conv_b[j], ab_conv_ln_g[j],
                                ab_conv_ln_b[j], ab_cmp_pos[j], ab_cmp_w1[j], ab_cmp_w2[j], ab_w_out[j])
        else:
            h = pool_mixer_residual(h.reshape(b, t, d), mix_norm[i], pool_w[j], pool_scale[j]).reshape(m, d)
        y = _peer_ffn(h, ffn_norm[i], peer_wq[i], peer_subkeys[i], peer_u[i], peer_v[i])
        h = ple_residual(h, y, p[i].reshape(m, -1), ple_norm[i], ple_gate_w[i].astype(BF16),
                         ple_proj[i].astype(BF16), final_norm, final_norm=(i == depth - 1))
    return h.reshape(b, t, d)
```

```python
import functools

import jax
import jax.numpy as jnp
import numpy as np
from jax import lax
from jax.experimental import pallas as pl
from jax.experimental.pallas import tpu as pltpu

F32 = jnp.float32
BF16 = jnp.bfloat16
HIGHEST = lax.Precision.HIGHEST

D_MODEL = 1024
NORM_EPS = 1e-6
NEG_INF = -1e30

CONV_CH = 512
CONV_WIDTH = 31
CONV_HALO = 32

HEAD_DIM = 64
NSA_KV_HEADS = 2
NSA_HPG = 4
CMP_LEN = 32
CMP_STRIDE = 16
CMP_HIDDEN = 256
SLC_BLOCK = 64
SLC_SHIFT = 6
SLC_TOP = 16
WINDOW = 512
FORCE_SCORE = 1e9

POOL_WINDOWS = (2, 4, 8, 16)
POOL_GROUP = 256
POOL_HALO = 16

PEER_HEADS = 8
N_KEYS = 128
PEER_TOPK = 16
PEER_KK = PEER_HEADS * PEER_TOPK
HALF = D_MODEL // 2
ROW_SUB = HALF // 128

VMEM_LIMIT = 56 * 1024 * 1024


def _cparams(sem, vmem=None):
    return pltpu.CompilerParams(dimension_semantics=sem, vmem_limit_bytes=vmem)


def _rms(x, g):
    return x * lax.rsqrt(jnp.mean(x * x, axis=-1, keepdims=True) + NORM_EPS) * g


def _gelu(x):
    return 0.5 * x * (1.0 + jnp.tanh(0.7978845608028654 * (x + 0.044715 * (x * x * x))))


def _sigmoid(x):
    return 1.0 / (1.0 + jnp.exp(-x))


def _norm_mm_kernel(x_ref, g_ref, *refs, n_w, emit_hn):
    w_refs = refs[:n_w]
    o_refs = refs[n_w:]
    y = _rms(x_ref[...], g_ref[...])
    yb = y.astype(BF16)
    for w_ref, o_ref in zip(w_refs, o_refs[:n_w]):
        o_ref[...] = jnp.dot(yb, w_ref[...], preferred_element_type=F32)
    if emit_hn:
        lo = (y - yb.astype(F32)).astype(BF16)
        blocks = [term[:, _slot_dims(c)] for term in (yb, lo) for c in range(SLOT)]
        o_refs[n_w][...] = jnp.concatenate(blocks, axis=-1)


def norm_matmul(x, g, ws, *, emit_hn=False, tm=512):
    m, d = x.shape
    n_w = len(ws)
    in_specs = [pl.BlockSpec((tm, d), lambda i: (i, 0)), pl.BlockSpec((1, d), lambda i: (0, 0))]
    in_specs += [pl.BlockSpec(w.shape, lambda i: (0, 0)) for w in ws]
    out_shape = [jax.ShapeDtypeStruct((m, w.shape[1]), F32) for w in ws]
    out_specs = [pl.BlockSpec((tm, w.shape[1]), lambda i: (i, 0)) for w in ws]
    if emit_hn:
        out_shape += [jax.ShapeDtypeStruct((m, 2 * d), BF16)]
        out_specs += [pl.BlockSpec((tm, 2 * d), lambda i: (i, 0))]
    return pl.pallas_call(
        functools.partial(_norm_mm_kernel, n_w=n_w, emit_hn=emit_hn),
        grid=(m // tm,),
        in_specs=in_specs, out_specs=out_specs, out_shape=out_shape,
        compiler_params=_cparams(("parallel",), VMEM_LIMIT),
        name="norm_matmul",
    )(x, g.reshape(1, d), *ws)


def _conv_kernel(cur_ref, halo_ref, w_ref, b_ref, g_ref, beta_ref, o_ref, buf, *, tt):
    t = pl.program_id(1)
    cur = cur_ref[0]
    halo = halo_ref[0]
    a_halo = halo[:, :CONV_CH] * _sigmoid(halo[:, CONV_CH:])
    buf[0:CONV_HALO, :] = jnp.where(t > 0, a_halo, 0.0)
    buf[CONV_HALO:, :] = cur[:, :CONV_CH] * _sigmoid(cur[:, CONV_CH:])
    acc = jnp.zeros((tt, CONV_CH), F32)
    first = CONV_HALO - (CONV_WIDTH - 1)
    for j in range(CONV_WIDTH):
        acc = acc + buf[pl.ds(first + j, tt), :] * w_ref[j:j + 1, :]
    y = acc + b_ref[...]
    mu = jnp.mean(y, axis=-1, keepdims=True)
    yc = y - mu
    var = jnp.mean(yc * yc, axis=-1, keepdims=True)
    y = yc * lax.rsqrt(var + NORM_EPS) * g_ref[...] + beta_ref[...]
    o_ref[0] = y * _sigmoid(y)


def conformer_conv(vg, conv_w, conv_b, ln_g, ln_b, *, tt=512):
    b, t, _ = vg.shape
    tt = min(tt, t)
    per = tt // CONV_HALO
    vec = lambda v: v.reshape(1, CONV_CH)
    vspec = pl.BlockSpec((1, CONV_CH), lambda bi, ti: (0, 0))
    return pl.pallas_call(
        functools.partial(_conv_kernel, tt=tt),
        grid=(b, t // tt),
        in_specs=[
            pl.BlockSpec((1, tt, 2 * CONV_CH), lambda bi, ti: (bi, ti, 0)),
            pl.BlockSpec((1, CONV_HALO, 2 * CONV_CH), lambda bi, ti: (bi, jnp.maximum(ti * per - 1, 0), 0)),
            pl.BlockSpec((CONV_WIDTH, CONV_CH), lambda bi, ti: (0, 0)),
            vspec, vspec, vspec,
        ],
        out_specs=pl.BlockSpec((1, tt, CONV_CH), lambda bi, ti: (bi, ti, 0)),
        out_shape=jax.ShapeDtypeStruct((b, t, CONV_CH), F32),
        scratch_shapes=[pltpu.VMEM((tt + CONV_HALO, CONV_CH), F32)],
        compiler_params=_cparams(("parallel", "parallel"), VMEM_LIMIT),
        name="conformer_conv",
    )(vg, vg, conv_w, vec(conv_b), vec(ln_g), vec(ln_b))


def _compress_kernel(c_ref, pos_ref, w1_ref, w2_ref, o_ref):
    c = c_ref[0, 0]
    n = c.shape[0]
    ca = c + pos_ref[0, 0]
    cb = c + pos_ref[0, 1]
    for g in range(NSA_KV_HEADS):
        ua = jnp.dot(ca, w1_ref[0, g, 0], preferred_element_type=F32, precision=HIGHEST)
        ub = jnp.dot(cb, w1_ref[0, g, 1], preferred_element_type=F32, precision=HIGHEST)
        hid = _gelu(ua + pltpu.roll(ub, n - 1, axis=0))
        o_ref[0, 0, g] = jnp.dot(hid, w2_ref[0], preferred_element_type=F32, precision=HIGHEST)


def compress_blocks(src, pos, w1, w2):
    _, b, n_chunk, width = src.shape
    groups = NSA_KV_HEADS
    posx = jnp.broadcast_to(pos.reshape(2, 2, CMP_STRIDE, 1, HEAD_DIM), (2, 2, CMP_STRIDE, groups, HEAD_DIM))
    posx = posx.reshape(2, 2, 1, width)
    w1r = w1.reshape(2, 1, 2, CMP_STRIDE, 1, HEAD_DIM, CMP_HIDDEN)
    own = (jnp.arange(groups)[:, None] == jnp.arange(groups)[None, :]).reshape(1, groups, 1, 1, groups, 1, 1)
    w1x = jnp.where(own, w1r, 0.0).reshape(2, groups, 2, width, CMP_HIDDEN)
    return pl.pallas_call(
        _compress_kernel,
        grid=(2, b),
        in_specs=[
            pl.BlockSpec((1, 1, n_chunk, width), lambda k, i: (k, i, 0, 0)),
            pl.BlockSpec((1, 2, 1, width), lambda k, i: (k, 0, 0, 0)),
            pl.BlockSpec((1, groups, 2, width, CMP_HIDDEN), lambda k, i: (k, 0, 0, 0, 0)),
            pl.BlockSpec((1, CMP_HIDDEN, HEAD_DIM), lambda k, i: (k, 0, 0)),
        ],
        out_specs=pl.BlockSpec((1, 1, groups, n_chunk, HEAD_DIM), lambda k, i: (k, i, 0, 0, 0)),
        out_shape=jax.ShapeDtypeStruct((2, b, groups, n_chunk, HEAD_DIM), F32),
        compiler_params=_cparams(("parallel", "parallel"), VMEM_LIMIT),
        name="nsa_compress",
    )(src, posx, w1x, w2)


def _topk_rows(work, k, row=None):
    if row is None:
        row = lax.broadcasted_iota(jnp.int32, work.shape, 0).astype(F32)
    else:
        row = jnp.broadcast_to(row, work.shape)
    vals, idxs = [], []
    for _ in range(k):
        m = jnp.max(work, axis=0, keepdims=True)
        first = jnp.min(jnp.where(work == m, row, 3.0e38), axis=0, keepdims=True)
        vals.append(m)
        idxs.append(first)
        work = jnp.where(row == first, -jnp.inf, work)
    return vals, idxs


def _masked_flash(qb, k_ref, v_ref, lo, hi, tk, mask_fn, first_group):
    rows = qb.shape[0]
    tq = rows // NSA_HPG

    def scores(j):
        kt = k_ref[0, pl.ds(pl.multiple_of(j * tk, tk), tk), :]
        s = lax.dot_general(qb, kt, (((1,), (1,)), ((), ())), preferred_element_type=F32)
        return (s.reshape(NSA_HPG, tq, tk) + mask_fn(j)[None]).reshape(rows, tk)

    def body(j, carry):
        m, acc, s = carry
        s_next = scores(jnp.minimum(j + 1, hi - 1))
        vt = v_ref[0, 0, pl.ds(pl.multiple_of(j * tk, tk), tk), :]
        m_new = jnp.maximum(m, jnp.max(s, axis=-1, keepdims=True))
        p = jnp.exp(s - m_new)
        acc = jnp.exp(m - m_new) * acc + jnp.dot(p.astype(BF16), vt, preferred_element_type=F32)
        return m_new, acc, s_next

    init = (jnp.full((rows, 1), NEG_INF, F32), jnp.zeros((rows, 2 * HEAD_DIM), F32), scores(lo))
    _, acc, _ = lax.fori_loop(lo, hi, body, init)
    left, right = acc[:, :HEAD_DIM], acc[:, HEAD_DIM:]
    return jnp.where(first_group, left / right[:, 0:1], right / left[:, 0:1])


def _nsa_kernel(q_ref, kc_ref, vc_ref, ks_ref, vs_ref, kw_ref, vw_ref, g_ref, ov_ref, o_ref, *, tq, tks, tkw):
    i = pl.program_id(2)
    first_group = pl.program_id(1) == 0
    t0 = i * tq
    rows = NSA_HPG * tq
    q_tile = q_ref[0]
    q = jnp.concatenate([q_tile[:, h * HEAD_DIM:(h + 1) * HEAD_DIM] for h in range(NSA_HPG)], axis=0)
    q = q * (HEAD_DIM ** -0.5)
    zero = jnp.zeros_like(q)
    qb = jnp.where(first_group, jnp.concatenate([q, zero], axis=-1),
                   jnp.concatenate([zero, q], axis=-1)).astype(BF16)
    t_q = t0 + lax.broadcasted_iota(jnp.int32, (tq, 1), 0)
    t_all = jnp.concatenate([t_q] * NSA_HPG, axis=0)

    kc = kc_ref[0, 0]
    n_cmp = kc.shape[0]
    s = lax.dot_general(q, kc, (((1,), (1,)), ((), ())), preferred_element_type=F32, precision=HIGHEST)
    cmp_end = lax.broadcasted_iota(jnp.int32, (1, n_cmp), 1) * CMP_STRIDE + (CMP_LEN - 1)
    ok = cmp_end <= t_all
    s = jnp.where(ok, s, NEG_INF)
    e = jnp.where(ok, jnp.exp(s - jnp.max(s, axis=-1, keepdims=True)), 0.0)
    den = jnp.sum(e, axis=-1, keepdims=True)
    p_cmp = e / jnp.where(den > 0.0, den, 1.0)
    o_cmp = jnp.dot(p_cmp.astype(BF16), vc_ref[0, 0].astype(BF16), preferred_element_type=F32)

    p_sum = p_cmp[0:tq]
    for h in range(1, NSA_HPG):
        p_sum = p_sum + p_cmp[h * tq:(h + 1) * tq]
    imp = jnp.dot(p_sum, ov_ref[...], preferred_element_type=F32, precision=HIGHEST)
    n_slc = ks_ref.shape[1] // SLC_BLOCK
    imp_t = imp.T[0:n_slc]
    blk = lax.broadcasted_iota(jnp.int32, (n_slc, tq), 0)
    t_lane = t0 + lax.broadcasted_iota(jnp.int32, (n_slc, tq), 1)
    cur = t_lane >> SLC_SHIFT
    forced = (blk == 0) | (blk == cur) | (blk == cur - 1)
    imp_t = jnp.where(forced, FORCE_SCORE, imp_t)
    imp_t = jnp.where(blk * SLC_BLOCK <= t_lane, imp_t, NEG_INF)
    _, picks = _topk_rows(imp_t, min(SLC_TOP, n_slc))
    blk_f = blk.astype(F32)
    member = jnp.zeros((n_slc, tq), F32)
    for pk in picks:
        member = jnp.where(blk_f == pk, 1.0, member)
    if n_slc < 128:
        member = jnp.concatenate([member, jnp.zeros((128 - n_slc, tq), F32)], axis=0)
    member_q = member.T.astype(BF16)

    blocks_per_tile = tks // SLC_BLOCK

    def slc_mask(j):
        sel_row = lax.broadcasted_iota(jnp.int32, (128, tks), 0)
        key_blk = j * blocks_per_tile + (lax.broadcasted_iota(jnp.int32, (128, tks), 1) >> SLC_SHIFT)
        expand = jnp.where(sel_row == key_blk, 1.0, 0.0).astype(BF16)
        sel = jnp.dot(member_q, expand, preferred_element_type=F32) > 0.5
        kpos = j * tks + lax.broadcasted_iota(jnp.int32, (1, tks), 1)
        return jnp.where(sel & (kpos <= t_q), 0.0, NEG_INF)

    o_slc = _masked_flash(qb, ks_ref, vs_ref, 0, (t0 + tq + tks - 1) // tks, tks, slc_mask, first_group)

    def win_mask(j):
        kpos = j * tkw + lax.broadcasted_iota(jnp.int32, (1, tkw), 1)
        dist = t_q - kpos
        return jnp.where((dist >= 0) & (dist < WINDOW), 0.0, NEG_INF)

    lo = jnp.maximum(t0 - (WINDOW - 1), 0) // tkw
    o_win = _masked_flash(qb, kw_ref, vw_ref, lo, (t0 + tq + tkw - 1) // tkw, tkw, win_mask, first_group)

    gate = _sigmoid(g_ref[0])
    n_gate = 3 * NSA_HPG
    gate = jnp.where(first_group, gate[:, :n_gate], gate[:, n_gate:])
    heads = []
    for h in range(NSA_HPG):
        r = slice(h * tq, (h + 1) * tq)
        heads.append(gate[:, 3 * h:3 * h + 1] * o_cmp[r] + gate[:, 3 * h + 1:3 * h + 2] * o_slc[r]
                     + gate[:, 3 * h + 2:3 * h + 3] * o_win[r])
    o_ref[0] = jnp.concatenate(heads, axis=-1)


def nsa_attention(q, kc, vc, ks, vs, kw, vw, gates, *, tq=256, tks=256, tkw=128):
    b, t, _ = q.shape
    g = NSA_KV_HEADS
    n_chunk = kc.shape[2]
    n_slc = t // SLC_BLOCK
    n_cmp = (t - CMP_LEN) // CMP_STRIDE + 1
    tks = min(tks, t)
    cmp_start = np.arange(n_chunk) * CMP_STRIDE
    slc_start = np.arange(128) * SLC_BLOCK
    overlap = ((cmp_start[:, None] < slc_start[None, :] + SLC_BLOCK)
               & (cmp_start[:, None] + CMP_LEN > slc_start[None, :])
               & (np.arange(n_chunk)[:, None] < n_cmp) & (np.arange(128)[None, :] < n_slc))
    overlap = jnp.asarray(overlap.astype(np.float32))
    per_group = lambda a: pl.BlockSpec((1, 1) + a.shape[2:], lambda bi, gi, i: (bi, gi, 0, 0))
    shared = lambda a: pl.BlockSpec((1,) + a.shape[1:], lambda bi, gi, i: (bi, 0, 0))
    heads_w = NSA_HPG * HEAD_DIM
    return pl.pallas_call(
        functools.partial(_nsa_kernel, tq=tq, tks=tks, tkw=tkw),
        grid=(b, g, t // tq),
        in_specs=[
            pl.BlockSpec((1, tq, heads_w), lambda bi, gi, i: (bi, i, gi)),
            per_group(kc), per_group(vc), shared(ks), per_group(vs), shared(kw), per_group(vw),
            pl.BlockSpec((1, tq, g * 3 * NSA_HPG), lambda bi, gi, i: (bi, i, 0)),
            pl.BlockSpec((n_chunk, 128), lambda bi, gi, i: (0, 0)),
        ],
        out_specs=pl.BlockSpec((1, tq, heads_w), lambda bi, gi, i: (bi, i, gi)),
        out_shape=jax.ShapeDtypeStruct(q.shape, F32),
        compiler_params=_cparams(("parallel", "parallel", "arbitrary"), VMEM_LIMIT),
        name="nsa_attention",
    )(q, kc, vc, ks, vs, kw, vw, gates, overlap)


def _mm2_res_kernel(h_ref, a_ref, b_ref, wa_ref, wb_ref, o_ref):
    acc = jnp.dot(a_ref[...].astype(BF16), wa_ref[...], preferred_element_type=F32)
    acc = acc + jnp.dot(b_ref[...].astype(BF16), wb_ref[...], preferred_element_type=F32)
    o_ref[...] = h_ref[...] + acc


def mm2_residual(h, a, b, wa, wb, *, tm=512):
    m, d = h.shape
    row = lambda w: pl.BlockSpec((tm, w), lambda i: (i, 0))
    whole = lambda w: pl.BlockSpec(w.shape, lambda i: (0, 0))
    return pl.pallas_call(
        _mm2_res_kernel,
        grid=(m // tm,),
        in_specs=[row(d), row(a.shape[1]), row(b.shape[1]), whole(wa), whole(wb)],
        out_specs=row(d),
        out_shape=jax.ShapeDtypeStruct((m, d), F32),
        compiler_params=_cparams(("parallel",), VMEM_LIMIT),
        name="out_proj_residual",
    )(h, a, b, wa, wb)


def _pool_kernel(cur_ref, halo_ref, g_ref, w_ref, sc_ref, o_ref, buf, *, tt):
    t = pl.program_id(1)
    g = g_ref[...]
    cur = cur_ref[0]
    hn = _rms(cur, g)
    buf[0:POOL_HALO, :] = jnp.where(t > 0, _rms(halo_ref[0], g), 0.0)
    buf[POOL_HALO:, :] = hn
    pos = t * tt + lax.broadcasted_iota(jnp.int32, (tt, 1), 0)
    outs = []
    for gi, w in enumerate(POOL_WINDOWS):
        cols = slice(gi * POOL_GROUP, (gi + 1) * POOL_GROUP)
        tot = hn[:, cols]
        for j in range(1, w):
            tot = tot + buf[pl.ds(POOL_HALO - j, tt), cols]
        cnt = jnp.minimum(pos + 1, w).astype(F32)
        d = tot / cnt - hn[:, cols]
        outs.append(jnp.dot(d.astype(BF16), w_ref[gi], preferred_element_type=F32))
    y = jnp.concatenate(outs, axis=-1) * sc_ref[...]
    o_ref[0] = cur + y


def pool_mixer_residual(h, g, pool_w, pool_scale, *, tt=512):
    b, t, d = h.shape
    tt = min(tt, t)
    per = tt // POOL_HALO
    return pl.pallas_call(
        functools.partial(_pool_kernel, tt=tt),
        grid=(b, t // tt),
        in_specs=[
            pl.BlockSpec((1, tt, d), lambda bi, ti: (bi, ti, 0)),
            pl.BlockSpec((1, POOL_HALO, d), lambda bi, ti: (bi, jnp.maximum(ti * per - 1, 0), 0)),
            pl.BlockSpec((1, d), lambda bi, ti: (0, 0)),
            pl.BlockSpec(pool_w.shape, lambda bi, ti: (0, 0, 0)),
            pl.BlockSpec((1, d), lambda bi, ti: (0, 0)),
        ],
        out_specs=pl.BlockSpec((1, tt, d), lambda bi, ti: (bi, ti, 0)),
        out_shape=jax.ShapeDtypeStruct(h.shape, F32),
        scratch_shapes=[pltpu.VMEM((tt + POOL_HALO, d), F32)],
        compiler_params=_cparams(("parallel", "parallel"), VMEM_LIMIT),
        name="pool_mixer",
    )(h, h, g.reshape(1, d), pool_w.astype(BF16), pool_scale.reshape(1, d))


def _peer_route_kernel(q_ref, key_ref, idx_ref, gate_ref):
    q = q_ref[...]
    s = lax.dot_general(key_ref[0], q, (((1,), (1,)), ((), ())), preferred_element_type=F32,
                        precision=HIGHEST)
    v0, i0 = _topk_rows(s[0:N_KEYS], PEER_TOPK)
    v1, i1 = _topk_rows(s[N_KEYS:], PEER_TOPK)
    s0 = jnp.concatenate(v0, axis=0)
    s1 = jnp.concatenate(v1, axis=0)
    step = lax.broadcasted_iota(jnp.int32, (8, 1), 0)
    stepf = step.astype(F32)
    blocks, codes = [], []
    for a, b0 in ((0, 0), (0, 8), (1, 0), (2, 0), (3, 0)):
        blocks.append(v0[a] + s1[b0:b0 + 8])
        codes.append(stepf + float(a * PEER_TOPK + b0))
    for a0, b in ((8, 0), (0, 0), (0, 1), (0, 2)):
        dup = jnp.where((step < 4) & (a0 == 0), -jnp.inf, 0.0)
        blocks.append(s0[a0:a0 + 8] + v1[b] + dup)
        codes.append((stepf + float(a0)) * PEER_TOPK + float(b))
    cv, ci = _topk_rows(jnp.concatenate(blocks, axis=0), PEER_TOPK, jnp.concatenate(codes, axis=0))
    c_top = jnp.concatenate(cv, axis=0)
    c_idx = jnp.concatenate(ci, axis=0).astype(jnp.int32)
    a_sel = c_idx >> 4
    b_sel = c_idx & (PEER_TOPK - 1)
    k1 = jnp.zeros_like(c_top)
    k2 = jnp.zeros_like(c_top)
    for r in range(PEER_TOPK):
        k1 = jnp.where(a_sel == r, i0[r], k1)
        k2 = jnp.where(b_sel == r, i1[r], k2)
    idx_ref[0] = ((k1 * N_KEYS + k2) * ROW_SUB).astype(jnp.int32)
    e = jnp.exp(c_top - jnp.max(c_top, axis=0, keepdims=True))
    gate_ref[0] = e / jnp.sum(e, axis=0, keepdims=True)


def peer_route(q, keys_bd, *, tb=512):
    m = q.shape[0]
    shp = (PEER_HEADS, PEER_TOPK, m)
    ospec = pl.BlockSpec((1, PEER_TOPK, tb), lambda i, h: (h, 0, i))
    return pl.pallas_call(
        _peer_route_kernel,
        grid=(m // tb, PEER_HEADS),
        in_specs=[pl.BlockSpec((tb, 2 * HEAD_DIM), lambda i, h: (i, h)),
                  pl.BlockSpec((1, 2 * N_KEYS, 2 * HEAD_DIM), lambda i, h: (h, 0, 0))],
        out_specs=[ospec, ospec],
        out_shape=[jax.ShapeDtypeStruct(shp, jnp.int32), jax.ShapeDtypeStruct(shp, F32)],
        compiler_params=_cparams(("parallel", "parallel"), VMEM_LIMIT),
        name="peer_route",
    )(q, keys_bd)


def pack_table(tab):
    bits = lax.bitcast_convert_type(tab.astype(BF16), jnp.uint16).astype(jnp.uint32)
    packed = (bits[:, :HALF] << 16) | bits[:, HALF:]
    return packed.reshape(tab.shape[0] * ROW_SUB, 128)


STAGE_ROWS = PEER_KK * ROW_SUB
N_STAGE = 2
SLOT = 2 * ROW_SUB


def _slot_dims(c):
    start = (c // 2) * 128 + (0 if c % 2 else HALF)
    return slice(start, start + 128)


def _from_slots(y):
    slot_of = {(_slot_dims(c).start // 128): c for c in range(SLOT)}
    return jnp.concatenate([y[:, slot_of[blk] * 128:(slot_of[blk] + 1) * 128] for blk in range(SLOT)], axis=-1)


def _two_bf16(a, axis):
    hi = a.astype(BF16)
    lo = (a - hi.astype(F32)).astype(BF16)
    return jnp.concatenate([hi, lo], axis=axis)


def _load_table_and_ids(tab_hbm, tab_vmem, idx_hbm, idx_smem, sems):
    i = pl.program_id(0)

    @pl.when(i == 0)
    def _():
        cp = pltpu.make_async_copy(tab_hbm, tab_vmem, sems.at[0])
        cp.start()
        cp.wait()

    copies = [pltpu.make_async_copy(idx_hbm.at[k, i], idx_smem[k], sems.at[1 + k]) for k in range(PEER_KK)]
    for cp in copies:
        cp.start()
    for cp in copies:
        cp.wait()


def _for_each_token(tab_vmem, idx_smem, stages, tb, contract, finish):
    def gather(stage, t):
        for k in range(PEER_KK):
            row = pl.multiple_of(idx_smem[k][t], ROW_SUB)
            stage[pl.ds(ROW_SUB * k, ROW_SUB), :] = tab_vmem[pl.ds(row, ROW_SUB), :]

    gather(stages[0], 0)
    gather(stages[1], 1)

    def pair(i, carry):
        t = 2 * i
        part_a = contract(t, pltpu.bitcast(stages[0][...], BF16))
        part_b = contract(t + 1, pltpu.bitcast(stages[1][...], BF16))
        finish(t, 0, part_a)
        finish(t + 1, 1, part_b)
        gather(stages[0], jnp.minimum(t + 2, tb - 1))
        gather(stages[1], jnp.minimum(t + 3, tb - 1))
        return carry

    lax.fori_loop(0, tb // 2, pair, 0)


def _slot_mask():
    j = lax.broadcasted_iota(jnp.int32, (SLOT, 2 * STAGE_ROWS), 1)
    r = lax.broadcasted_iota(jnp.int32, (SLOT, 2 * STAGE_ROWS), 0)
    return (j & (SLOT - 1)) == r


def _peer_act_kernel(idx_hbm, tab_hbm, x_ref, gate_ref, fold_ref, coef_ref, tab_vmem, stages, zbuf, sems,
                     *idx_smem, tb, chunk):
    stages = [stages.at[n] for n in range(N_STAGE)]
    _load_table_and_ids(tab_hbm, tab_vmem, idx_hbm, idx_smem, sems)
    mask = _slot_mask()

    def contract(t, rows):
        return lax.dot_general(x_ref[t], rows, (((1,), (1,)), ((), ())), preferred_element_type=F32)

    def keep(t, pos, g):
        zbuf[t] = jnp.where(mask, g[0:SLOT] + g[SLOT:], 0.0)

    _for_each_token(tab_vmem, idx_smem, stages, tb, contract, keep)

    def finish(c, carry):
        r0 = pl.multiple_of(c * chunk, chunk)
        z = zbuf[pl.ds(r0, chunk)].reshape(chunk * SLOT, 2 * STAGE_ROWS)
        part = jnp.dot(_two_bf16(z, 0), fold_ref[...], preferred_element_type=F32)
        rows = part[0:chunk * SLOT] + part[chunk * SLOT:]
        act = jnp.sum(rows.reshape(chunk, SLOT, PEER_KK), axis=1)
        coef_ref[pl.ds(r0, chunk), :] = gate_ref[pl.ds(r0, chunk), :] * _gelu(act)
        return carry

    lax.fori_loop(0, tb // chunk, finish, 0)


def peer_coefficients(idx_t, tab, x_terms, gate, *, tb=256, chunk=32):
    m = gate.shape[0]
    n_j = 2 * STAGE_ROWS
    fold = (np.arange(n_j)[:, None] // SLOT == np.arange(PEER_KK)[None, :]).astype(np.float32)
    return pl.pallas_call(
        functools.partial(_peer_act_kernel, tb=tb, chunk=chunk),
        grid=(m // tb,),
        in_specs=[pl.BlockSpec(memory_space=pl.ANY), pl.BlockSpec(memory_space=pl.ANY),
                  pl.BlockSpec((tb, 2 * SLOT, 128), lambda i: (i, 0, 0)),
                  pl.BlockSpec((tb, PEER_KK), lambda i: (i, 0)),
                  pl.BlockSpec((n_j, PEER_KK), lambda i: (0, 0))],
        out_specs=pl.BlockSpec((tb, PEER_KK), lambda i: (i, 0)),
        out_shape=jax.ShapeDtypeStruct((m, PEER_KK), F32),
        scratch_shapes=[pltpu.VMEM(tab.shape, jnp.uint32),
                        pltpu.VMEM((N_STAGE, STAGE_ROWS, 128), jnp.uint32),
                        pltpu.VMEM((tb, SLOT, n_j), F32),
                        pltpu.SemaphoreType.DMA((1 + PEER_KK,))] + [pltpu.SMEM((tb,), jnp.int32)] * PEER_KK,
        compiler_params=_cparams(("arbitrary",), VMEM_LIMIT),
        name="peer_coefficients",
    )(idx_t.reshape(PEER_KK, m // tb, tb), tab, x_terms, gate, jnp.asarray(fold, BF16))


def _peer_mix_kernel(idx_hbm, tab_hbm, coef_ref, spread_ref, y_ref, tab_vmem, stages, lhs, sems,
                     *idx_smem, tb, chunk):
    stages = [stages.at[n] for n in range(N_STAGE)]
    _load_table_and_ids(tab_hbm, tab_vmem, idx_hbm, idx_smem, sems)
    shape = (SLOT, 2 * STAGE_ROWS)
    sub = lax.broadcasted_iota(jnp.int32, shape, 0)
    col_slot = lax.broadcasted_iota(jnp.int32, shape, 1) & (SLOT - 1)
    low_slot = (sub & (ROW_SUB - 1)) * 2
    keep = (jnp.where(col_slot == low_slot, jnp.uint32(0x0000FFFF), jnp.uint32(0))
            | jnp.where(col_slot == low_slot + 1, jnp.uint32(0xFFFF0000), jnp.uint32(0)))
    first_term = sub < ROW_SUB

    def both_halves(a):
        bits = lax.bitcast_convert_type(a, jnp.uint32)
        return bits | (bits >> 16)

    def prepare(c, carry):
        r0 = pl.multiple_of(c * chunk, chunk)
        terms = _two_bf16(coef_ref[pl.ds(r0, chunk), :], 0)
        wide = jnp.dot(terms, spread_ref[...], preferred_element_type=F32)
        w_hi = both_halves(wide[0:chunk])
        w_lo = both_halves(wide[chunk:])
        for j in range(chunk):
            word = jnp.where(first_term, jnp.broadcast_to(w_hi[j:j + 1, :], shape),
                             jnp.broadcast_to(w_lo[j:j + 1, :], shape))
            lhs[r0 + j] = word & keep
        return carry

    lax.fori_loop(0, tb // chunk, prepare, 0)

    def contract(t, rows):
        return jnp.dot(pltpu.bitcast(lhs[t], BF16), rows, preferred_element_type=F32)

    def store(t, pos, y):
        y_ref[t] = y[0:SLOT] + y[SLOT:]

    _for_each_token(tab_vmem, idx_smem, stages, tb, contract, store)


def peer_mix(idx_t, coef, tab, *, tb=256, chunk=32):
    m = coef.shape[0]
    n_j = 2 * STAGE_ROWS
    spread = (np.arange(PEER_KK)[:, None] == np.arange(n_j)[None, :] // SLOT).astype(np.float32)
    return pl.pallas_call(
        functools.partial(_peer_mix_kernel, tb=tb, chunk=chunk),
        grid=(m // tb,),
        in_specs=[pl.BlockSpec(memory_space=pl.ANY), pl.BlockSpec(memory_space=pl.ANY),
                  pl.BlockSpec((tb, PEER_KK), lambda i: (i, 0)),
                  pl.BlockSpec((PEER_KK, n_j), lambda i: (0, 0))],
        out_specs=pl.BlockSpec((tb, SLOT, 128), lambda i: (i, 0, 0)),
        out_shape=jax.ShapeDtypeStruct((m, SLOT, 128), F32),
        scratch_shapes=[pltpu.VMEM(tab.shape, jnp.uint32),
                        pltpu.VMEM((N_STAGE, STAGE_ROWS, 128), jnp.uint32),
                        pltpu.VMEM((tb, SLOT, n_j), jnp.uint32),
                        pltpu.SemaphoreType.DMA((1 + PEER_KK,))] + [pltpu.SMEM((tb,), jnp.int32)] * PEER_KK,
        compiler_params=_cparams(("arbitrary",), VMEM_LIMIT),
        name="peer_mix",
    )(idx_t.reshape(PEER_KK, m // tb, tb), tab, coef, jnp.asarray(spread, BF16))


def _ple_kernel(h_ref, y_ref, p_ref, g_ref, wg_ref, wp_ref, fg_ref, o_ref, *, final_norm):
    h = h_ref[...] + _from_slots(y_ref[...])
    gate = _sigmoid(jnp.dot(_rms(h, g_ref[...]).astype(BF16), wg_ref[...], preferred_element_type=F32))
    h = h + jnp.dot(p_ref[...].astype(BF16), wp_ref[...], preferred_element_type=F32) * gate
    if final_norm:
        h = _rms(h, fg_ref[...])
    o_ref[...] = h


def ple_residual(h, y, p, g, wg, wp, fg, *, final_norm, tm=512):
    m, d = h.shape
    row = lambda w: pl.BlockSpec((tm, w), lambda i: (i, 0))
    whole = lambda a: pl.BlockSpec(a.shape, lambda i: (0, 0))
    g2, fg2 = g.reshape(1, d), fg.reshape(1, d)
    return pl.pallas_call(
        functools.partial(_ple_kernel, final_norm=final_norm),
        grid=(m // tm,),
        in_specs=[row(d), row(d), row(p.shape[1]), whole(g2), whole(wg), whole(wp), whole(fg2)],
        out_specs=row(d),
        out_shape=jax.ShapeDtypeStruct((m, d), F32),
        compiler_params=_cparams(("parallel",), VMEM_LIMIT),
        name="ple_residual",
    )(h, y, p, g2, wg, wp, fg2)


def _mixer_conv_nsa(h, b, t, norm_g, w_in, conv_w, conv_b, ln_g, ln_b, cmp_pos, cmp_w1, cmp_w2, w_out):
    m = b * t
    n_q = NSA_KV_HEADS * NSA_HPG * HEAD_DIM
    kv_w = NSA_KV_HEADS * HEAD_DIM
    c0, c1, c2 = 2 * CONV_CH, 2 * CONV_CH + n_q, 2 * CONV_CH + n_q + 6 * kv_w
    wb = w_in.astype(BF16)
    vg, q, kv, gt = norm_matmul(h, norm_g, [wb[:, :c0], wb[:, c0:c1], wb[:, c1:c2], wb[:, c2:]])
    a_out = conformer_conv(vg.reshape(b, t, c0), conv_w, conv_b, ln_g, ln_b)

    kv = kv.reshape(b, t, 6, kv_w)
    kind = lambda c: kv[:, :, c]
    n_chunk = t // CMP_STRIDE
    cmp_src = jnp.stack([kind(0), kind(1)]).reshape(2, b, n_chunk, CMP_STRIDE * kv_w)
    cmp = compress_blocks(cmp_src, cmp_pos, cmp_w1, cmp_w2)

    lane_group = jnp.arange(kv_w) // HEAD_DIM

    def value_operand(v):
        per_group = []
        for g in range(NSA_KV_HEADS):
            other = jnp.where(jnp.arange(kv_w) == HEAD_DIM * (1 - g), 1.0, 0.0)
            per_group.append(jnp.where(lane_group == g, v, other).astype(BF16))
        return jnp.stack(per_group, axis=1)

    o = nsa_attention(q.reshape(b, t, n_q), cmp[0], cmp[1], kind(2).astype(BF16), value_operand(kind(3)),
                      kind(4).astype(BF16), value_operand(kind(5)), gt.reshape(b, t, -1))
    wo = w_out.astype(BF16)
    return mm2_residual(h, a_out.reshape(m, CONV_CH), o.reshape(m, n_q), wo[:CONV_CH], wo[CONV_CH:])


def _peer_ffn(h, norm_g, wq, subkeys, u_tab, v_tab):
    q, x_terms = norm_matmul(h, norm_g, [wq.astype(BF16)], emit_hn=True)
    zeros = jnp.zeros_like(subkeys[:, 0])
    keys_bd = jnp.concatenate([jnp.concatenate([subkeys[:, 0], zeros], axis=-1),
                               jnp.concatenate([zeros, subkeys[:, 1]], axis=-1)], axis=1)
    idx_t, gate_t = peer_route(q, keys_bd)
    m = h.shape[0]
    idx_t = idx_t.reshape(PEER_KK, m)
    gate = gate_t.reshape(PEER_KK, m).T
    coef = peer_coefficients(idx_t, pack_table(u_tab), x_terms.reshape(m, 2 * SLOT, 128), gate)
    return peer_mix(idx_t, coef, pack_table(v_tab)).reshape(m, SLOT * 128)


def kernel(x, p, mix_norm, ab_w_in, ab_conv_w, ab_conv_b, ab_conv_ln_g, ab_conv_ln_b, ab_cmp_pos, ab_cmp_w1,
           ab_cmp_w2, ab_w_out, pool_w, pool_scale, ffn_norm, peer_wq, peer_subkeys, peer_u, peer_v, ple_norm,
           ple_gate_w, ple_proj, final_norm):
    b, t, d = x.shape
    m = b * t
    depth = p.shape[0]
    h = x.reshape(m, d)
    for i in range(depth):
        j = i // 2
        if i % 2 == 0:
            h = _mixer_conv_nsa(h, b, t, mix_norm[i], ab_w_in[j], ab_conv_w[j], ab_conv_b[j], ab_conv_ln_g[j],
                                ab_conv_ln_b[j], ab_cmp_pos[j], ab_cmp_w1[j], ab_cmp_w2[j], ab_w_out[j])
        else:
            h = pool_mixer_residual(h.reshape(b, t, d), mix_norm[i], pool_w[j], pool_scale[j]).reshape(m, d)
        y = _peer_ffn(h, ffn_norm[i], peer_wq[i], peer_subkeys[i], peer_u[i], peer_v[i])
        h = ple_residual(h, y, p[i].reshape(m, -1), ple_norm[i], ple_gate_w[i].astype(BF16),
                         ple_proj[i].astype(BF16), final_norm, final_norm=(i == depth - 1))
    return h.reshape(b, t, d)
```

```python
import functools

import jax
import jax.numpy as jnp
import numpy as np
from jax import lax
from jax.experimental import pallas as pl
from jax.experimental.pallas import tpu as pltpu

F32 = jnp.float32
BF16 = jnp.bfloat16
HIGHEST = lax.Precision.HIGHEST

D_MODEL = 1024
NORM_EPS = 1e-6
NEG_INF = -1e30

CONV_CH = 512
CONV_WIDTH = 31
CONV_HALO = 32

HEAD_DIM = 64
NSA_KV_HEADS = 2
NSA_HPG = 4
CMP_LEN = 32
CMP_STRIDE = 16
CMP_HIDDEN = 256
SLC_BLOCK = 64
SLC_SHIFT = 6
SLC_TOP = 16
WINDOW = 512
FORCE_SCORE = 1e9

POOL_WINDOWS = (2, 4, 8, 16)
POOL_GROUP = 256
POOL_HALO = 16

PEER_HEADS = 8
N_KEYS = 128
PEER_TOPK = 16
PEER_KK = PEER_HEADS * PEER_TOPK
HALF = D_MODEL // 2
ROW_SUB = HALF // 128

VMEM_LIMIT = 56 * 1024 * 1024


def _cparams(sem, vmem=None):
    return pltpu.CompilerParams(dimension_semantics=sem, vmem_limit_bytes=vmem)


def _rms(x, g):
    return x * lax.rsqrt(jnp.mean(x * x, axis=-1, keepdims=True) + NORM_EPS) * g


def _gelu(x):
    return 0.5 * x * (1.0 + jnp.tanh(0.7978845608028654 * (x + 0.044715 * (x * x * x))))


def _sigmoid(x):
    return 1.0 / (1.0 + jnp.exp(-x))


def _norm_mm_kernel(x_ref, g_ref, *refs, n_w, emit_hn):
    w_refs = refs[:n_w]
    o_refs = refs[n_w:]
    y = _rms(x_ref[...], g_ref[...])
    yb = y.astype(BF16)
    for w_ref, o_ref in zip(w_refs, o_refs[:n_w]):
        o_ref[...] = jnp.dot(yb, w_ref[...], preferred_element_type=F32)
    if emit_hn:
        lo = (y - yb.astype(F32)).astype(BF16)
        blocks = [term[:, _slot_dims(c)] for term in (yb, lo) for c in range(SLOT)]
        o_refs[n_w][...] = jnp.concatenate(blocks, axis=-1)


def norm_matmul(x, g, ws, *, emit_hn=False, tm=512):
    m, d = x.shape
    n_w = len(ws)
    in_specs = [pl.BlockSpec((tm, d), lambda i: (i, 0)), pl.BlockSpec((1, d), lambda i: (0, 0))]
    in_specs += [pl.BlockSpec(w.shape, lambda i: (0, 0)) for w in ws]
    out_shape = [jax.ShapeDtypeStruct((m, w.shape[1]), F32) for w in ws]
    out_specs = [pl.BlockSpec((tm, w.shape[1]), lambda i: (i, 0)) for w in ws]
    if emit_hn:
        out_shape += [jax.ShapeDtypeStruct((m, 2 * d), BF16)]
        out_specs += [pl.BlockSpec((tm, 2 * d), lambda i: (i, 0))]
    return pl.pallas_call(
        functools.partial(_norm_mm_kernel, n_w=n_w, emit_hn=emit_hn),
        grid=(m // tm,),
        in_specs=in_specs, out_specs=out_specs, out_shape=out_shape,
        compiler_params=_cparams(("parallel",), VMEM_LIMIT),
        name="norm_matmul",
    )(x, g.reshape(1, d), *ws)


def _conv_kernel(cur_ref, halo_ref, w_ref, b_ref, g_ref, beta_ref, o_ref, buf, *, tt):
    t = pl.program_id(1)
    cur = cur_ref[0]
    halo = halo_ref[0]
    a_halo = halo[:, :CONV_CH] * _sigmoid(halo[:, CONV_CH:])
    buf[0:CONV_HALO, :] = jnp.where(t > 0, a_halo, 0.0)
    buf[CONV_HALO:, :] = cur[:, :CONV_CH] * _sigmoid(cur[:, CONV_CH:])
    acc = jnp.zeros((tt, CONV_CH), F32)
    first = CONV_HALO - (CONV_WIDTH - 1)
    for j in range(CONV_WIDTH):
        acc = acc + buf[pl.ds(first + j, tt), :] * w_ref[j:j + 1, :]
    y = acc + b_ref[...]
    mu = jnp.mean(y, axis=-1, keepdims=True)
    yc = y - mu
    var = jnp.mean(yc * yc, axis=-1, keepdims=True)
    y = yc * lax.rsqrt(var + NORM_EPS) * g_ref[...] + beta_ref[...]
    o_ref[0] = y * _sigmoid(y)


def conformer_conv(vg, conv_w, conv_b, ln_g, ln_b, *, tt=512):
    b, t, _ = vg.shape
    tt = min(tt, t)
    per = tt // CONV_HALO
    vec = lambda v: v.reshape(1, CONV_CH)
    vspec = pl.BlockSpec((1, CONV_CH), lambda bi, ti: (0, 0))
    return pl.pallas_call(
        functools.partial(_conv_kernel, tt=tt),
        grid=(b, t // tt),
        in_specs=[
            pl.BlockSpec((1, tt, 2 * CONV_CH), lambda bi, ti: (bi, ti, 0)),
            pl.BlockSpec((1, CONV_HALO, 2 * CONV_CH), lambda bi, ti: (bi, jnp.maximum(ti * per - 1, 0), 0)),
            pl.BlockSpec((CONV_WIDTH, CONV_CH), lambda bi, ti: (0, 0)),
            vspec, vspec, vspec,
        ],
        out_specs=pl.BlockSpec((1, tt, CONV_CH), lambda bi, ti: (bi, ti, 0)),
        out_shape=jax.ShapeDtypeStruct((b, t, CONV_CH), F32),
        scratch_shapes=[pltpu.VMEM((tt + CONV_HALO, CONV_CH), F32)],
        compiler_params=_cparams(("parallel", "parallel"), VMEM_LIMIT),
        name="conformer_conv",
    )(vg, vg, conv_w, vec(conv_b), vec(ln_g), vec(ln_b))


def _compress_kernel(c_ref, pos_ref, w1_ref, w2_ref, o_ref):
    c = c_ref[0, 0]
    n = c.shape[0]
    ca = c + pos_ref[0, 0]
    cb = c + pos_ref[0, 1]
    for g in range(NSA_KV_HEADS):
        ua = jnp.dot(ca, w1_ref[0, g, 0], preferred_element_type=F32, precision=HIGHEST)
        ub = jnp.dot(cb, w1_ref[0, g, 1], preferred_element_type=F32, precision=HIGHEST)
        hid = _gelu(ua + pltpu.roll(ub, n - 1, axis=0))
        o_ref[0, 0, g] = jnp.dot(hid, w2_ref[0], preferred_element_type=F32, precision=HIGHEST)


def compress_blocks(src, pos, w1, w2):
    _, b, n_chunk, width = src.shape
    groups = NSA_KV_HEADS
    posx = jnp.broadcast_to(pos.reshape(2, 2, CMP_STRIDE, 1, HEAD_DIM), (2, 2, CMP_STRIDE, groups, HEAD_DIM))
    posx = posx.reshape(2, 2, 1, width)
    w1r = w1.reshape(2, 1, 2, CMP_STRIDE, 1, HEAD_DIM, CMP_HIDDEN)
    own = (jnp.arange(groups)[:, None] == jnp.arange(groups)[None, :]).reshape(1, groups, 1, 1, groups, 1, 1)
    w1x = jnp.where(own, w1r, 0.0).reshape(2, groups, 2, width, CMP_HIDDEN)
    return pl.pallas_call(
        _compress_kernel,
        grid=(2, b),
        in_specs=[
            pl.BlockSpec((1, 1, n_chunk, width), lambda k, i: (k, i, 0, 0)),
            pl.BlockSpec((1, 2, 1, width), lambda k, i: (k, 0, 0, 0)),
            pl.BlockSpec((1, groups, 2, width, CMP_HIDDEN), lambda k, i: (k, 0, 0, 0, 0)),
            pl.BlockSpec((1, CMP_HIDDEN, HEAD_DIM), lambda k, i: (k, 0, 0)),
        ],
        out_specs=pl.BlockSpec((1, 1, groups, n_chunk, HEAD_DIM), lambda k, i: (k, i, 0, 0, 0)),
        out_shape=jax.ShapeDtypeStruct((2, b, groups, n_chunk, HEAD_DIM), F32),
        compiler_params=_cparams(("parallel", "parallel"), VMEM_LIMIT),
        name="nsa_compress",
    )(src, posx, w1x, w2)


def _topk_rows(work, k, row=None):
    if row is None:
        row = lax.broadcasted_iota(jnp.int32, work.shape, 0).astype(F32)
    else:
        row = jnp.broadcast_to(row, work.shape)
    vals, idxs = [], []
    for _ in range(k):
        m = jnp.max(work, axis=0, keepdims=True)
        first = jnp.min(jnp.where(work == m, row, 3.0e38), axis=0, keepdims=True)
        vals.append(m)
        idxs.append(first)
        work = jnp.where(row == first, -jnp.inf, work)
    return vals, idxs


def _masked_flash(qb, k_ref, v_ref, lo, hi, tk, mask_fn, first_group):
    rows = qb.shape[0]
    tq = rows // NSA_HPG

    def scores(j):
        kt = k_ref[0, pl.ds(pl.multiple_of(j * tk, tk), tk), :]
        s = lax.dot_general(qb, kt, (((1,), (1,)), ((), ())), preferred_element_type=F32)
        return (s.reshape(NSA_HPG, tq, tk) + mask_fn(j)[None]).reshape(rows, tk)

    def body(j, carry):
        m, acc, s = carry
        s_next = scores(jnp.minimum(j + 1, hi - 1))
        vt = v_ref[0, 0, pl.ds(pl.multiple_of(j * tk, tk), tk), :]
        m_new = jnp.maximum(m, jnp.max(s, axis=-1, keepdims=True))
        p = jnp.exp(s - m_new)
        acc = jnp.exp(m - m_new) * acc + jnp.dot(p.astype(BF16), vt, preferred_element_type=F32)
        return m_new, acc, s_next

    init = (jnp.full((rows, 1), NEG_INF, F32), jnp.zeros((rows, 2 * HEAD_DIM), F32), scores(lo))
    _, acc, _ = lax.fori_loop(lo, hi, body, init)
    left, right = acc[:, :HEAD_DIM], acc[:, HEAD_DIM:]
    return jnp.where(first_group, left / right[:, 0:1], right / left[:, 0:1])


def _nsa_kernel(q_ref, kc_ref, vc_ref, ks_ref, vs_ref, kw_ref, vw_ref, g_ref, ov_ref, o_ref, *, tq, tks, tkw):
    i = pl.program_id(2)
    first_group = pl.program_id(1) == 0
    t0 = i * tq
    rows = NSA_HPG * tq
    q_tile = q_ref[0]
    q = jnp.concatenate([q_tile[:, h * HEAD_DIM:(h + 1) * HEAD_DIM] for h in range(NSA_HPG)], axis=0)
    q = q * (HEAD_DIM ** -0.5)
    zero = jnp.zeros_like(q)
    qb = jnp.where(first_group, jnp.concatenate([q, zero], axis=-1),
                   jnp.concatenate([zero, q], axis=-1)).astype(BF16)
    t_q = t0 + lax.broadcasted_iota(jnp.int32, (tq, 1), 0)
    t_all = jnp.concatenate([t_q] * NSA_HPG, axis=0)

    kc = kc_ref[0, 0]
    n_cmp = kc.shape[0]
    s = lax.dot_general(q, kc, (((1,), (1,)), ((), ())), preferred_element_type=F32, precision=HIGHEST)
    cmp_end = lax.broadcasted_iota(jnp.int32, (1, n_cmp), 1) * CMP_STRIDE + (CMP_LEN - 1)
    ok = cmp_end <= t_all
    s = jnp.where(ok, s, NEG_INF)
    e = jnp.where(ok, jnp.exp(s - jnp.max(s, axis=-1, keepdims=True)), 0.0)
    den = jnp.sum(e, axis=-1, keepdims=True)
    p_cmp = e / jnp.where(den > 0.0, den, 1.0)
    o_cmp = jnp.dot(p_cmp.astype(BF16), vc_ref[0, 0].astype(BF16), preferred_element_type=F32)

    p_sum = p_cmp[0:tq]
    for h in range(1, NSA_HPG):
        p_sum = p_sum + p_cmp[h * tq:(h + 1) * tq]
    imp = jnp.dot(p_sum, ov_ref[...], preferred_element_type=F32, precision=HIGHEST)
    n_slc = ks_ref.shape[1] // SLC_BLOCK
    imp_t = imp.T[0:n_slc]
    blk = lax.broadcasted_iota(jnp.int32, (n_slc, tq), 0)
    t_lane = t0 + lax.broadcasted_iota(jnp.int32, (n_slc, tq), 1)
    cur = t_lane >> SLC_SHIFT
    forced = (blk == 0) | (blk == cur) | (blk == cur - 1)
    imp_t = jnp.where(forced, FORCE_SCORE, imp_t)
    imp_t = jnp.where(blk * SLC_BLOCK <= t_lane, imp_t, NEG_INF)
    _, picks = _topk_rows(imp_t, min(SLC_TOP, n_slc))
    blk_f = blk.astype(F32)
    member = jnp.zeros((n_slc, tq), F32)
    for pk in picks:
        member = jnp.where(blk_f == pk, 1.0, member)
    if n_slc < 128:
        member = jnp.concatenate([member, jnp.zeros((128 - n_slc, tq), F32)], axis=0)
    member_q = member.T.astype(BF16)

    blocks_per_tile = tks // SLC_BLOCK

    def slc_mask(j):
        sel_row = lax.broadcasted_iota(jnp.int32, (128, tks), 0)
        key_blk = j * blocks_per_tile + (lax.broadcasted_iota(jnp.int32, (128, tks), 1) >> SLC_SHIFT)
        expand = jnp.where(sel_row == key_blk, 1.0, 0.0).astype(BF16)
        sel = jnp.dot(member_q, expand, preferred_element_type=F32) > 0.5
        kpos = j * tks + lax.broadcasted_iota(jnp.int32, (1, tks), 1)
        return jnp.where(sel & (kpos <= t_q), 0.0, NEG_INF)

    o_slc = _masked_flash(qb, ks_ref, vs_ref, 0, (t0 + tq + tks - 1) // tks, tks, slc_mask, first_group)

    def win_mask(j):
        kpos = j * tkw + lax.broadcasted_iota(jnp.int32, (1, tkw), 1)
        dist = t_q - kpos
        return jnp.where((dist >= 0) & (dist < WINDOW), 0.0, NEG_INF)

    lo = jnp.maximum(t0 - (WINDOW - 1), 0) // tkw
    o_win = _masked_flash(qb, kw_ref, vw_ref, lo, (t0 + tq + tkw - 1) // tkw, tkw, win_mask, first_group)

    gate = _sigmoid(g_ref[0])
    n_gate = 3 * NSA_HPG
    gate = jnp.where(first_group, gate[:, :n_gate], gate[:, n_gate:])
    heads = []
    for h in range(NSA_HPG):
        r = slice(h * tq, (h + 1) * tq)
        heads.append(gate[:, 3 * h:3 * h + 1] * o_cmp[r] + gate[:, 3 * h + 1:3 * h + 2] * o_slc[r]
                     + gate[:, 3 * h + 2:3 * h + 3] * o_win[r])
    o_ref[0] = jnp.concatenate(heads, axis=-1)


def nsa_attention(q, kc, vc, ks, vs, kw, vw, gates, *, tq=256, tks=256, tkw=128):
    b, t, _ = q.shape
    g = NSA_KV_HEADS
    n_chunk = kc.shape[2]
    n_slc = t // SLC_BLOCK
    n_cmp = (t - CMP_LEN) // CMP_STRIDE + 1
    tks = min(tks, t)
    cmp_start = np.arange(n_chunk) * CMP_STRIDE
    slc_start = np.arange(128) * SLC_BLOCK
    overlap = ((cmp_start[:, None] < slc_start[None, :] + SLC_BLOCK)
               & (cmp_start[:, None] + CMP_LEN > slc_start[None, :])
               & (np.arange(n_chunk)[:, None] < n_cmp) & (np.arange(128)[None, :] < n_slc))
    overlap = jnp.asarray(overlap.astype(np.float32))
    per_group = lambda a: pl.BlockSpec((1, 1) + a.shape[2:], lambda bi, gi, i: (bi, gi, 0, 0))
    shared = lambda a: pl.BlockSpec((1,) + a.shape[1:], lambda bi, gi, i: (bi, 0, 0))
    heads_w = NSA_HPG * HEAD_DIM
    return pl.pallas_call(
        functools.partial(_nsa_kernel, tq=tq, tks=tks, tkw=tkw),
        grid=(b, g, t // tq),
        in_specs=[
            pl.BlockSpec((1, tq, heads_w), lambda bi, gi, i: (bi, i, gi)),
            per_group(kc), per_group(vc), shared(ks), per_group(vs), shared(kw), per_group(vw),
            pl.BlockSpec((1, tq, g * 3 * NSA_HPG), lambda bi, gi, i: (bi, i, 0)),
            pl.BlockSpec((n_chunk, 128), lambda bi, gi, i: (0, 0)),
        ],
        out_specs=pl.BlockSpec((1, tq, heads_w), lambda bi, gi, i: (bi, i, gi)),
        out_shape=jax.ShapeDtypeStruct(q.shape, F32),
        compiler_params=_cparams(("parallel", "parallel", "arbitrary"), VMEM_LIMIT),
        name="nsa_attention",
    )(q, kc, vc, ks, vs, kw, vw, gates, overlap)


def _mm2_res_kernel(h_ref, a_ref, b_ref, wa_ref, wb_ref, o_ref):
    acc = jnp.dot(a_ref[...].astype(BF16), wa_ref[...], preferred_element_type=F32)
    acc = acc + jnp.dot(b_ref[...].astype(BF16), wb_ref[...], preferred_element_type=F32)
    o_ref[...] = h_ref[...] + acc


def mm2_residual(h, a, b, wa, wb, *, tm=512):
    m, d = h.shape
    row = lambda w: pl.BlockSpec((tm, w), lambda i: (i, 0))
    whole = lambda w: pl.BlockSpec(w.shape, lambda i: (0, 0))
    return pl.pallas_call(
        _mm2_res_kernel,
        grid=(m // tm,),
        in_specs=[row(d), row(a.shape[1]), row(b.shape[1]), whole(wa), whole(wb)],
        out_specs=row(d),
        out_shape=jax.ShapeDtypeStruct((m, d), F32),
        compiler_params=_cparams(("parallel",), VMEM_LIMIT),
        name="out_proj_residual",
    )(h, a, b, wa, wb)


def _pool_kernel(cur_ref, halo_ref, g_ref, w_ref, sc_ref, o_ref, buf, *, tt):
    t = pl.program_id(1)
    g = g_ref[...]
    cur = cur_ref[0]
    hn = _rms(cur, g)
    buf[0:POOL_HALO, :] = jnp.where(t > 0, _rms(halo_ref[0], g), 0.0)
    buf[POOL_HALO:, :] = hn
    pos = t * tt + lax.broadcasted_iota(jnp.int32, (tt, 1), 0)
    outs = []
    for gi, w in enumerate(POOL_WINDOWS):
        cols = slice(gi * POOL_GROUP, (gi + 1) * POOL_GROUP)
        tot = hn[:, cols]
        for j in range(1, w):
            tot = tot + buf[pl.ds(POOL_HALO - j, tt), cols]
        cnt = jnp.minimum(pos + 1, w).astype(F32)
        d = tot / cnt - hn[:, cols]
        outs.append(jnp.dot(d.astype(BF16), w_ref[gi], preferred_element_type=F32))
    y = jnp.concatenate(outs, axis=-1) * sc_ref[...]
    o_ref[0] = cur + y


def pool_mixer_residual(h, g, pool_w, pool_scale, *, tt=512):
    b, t, d = h.shape
    tt = min(tt, t)
    per = tt // POOL_HALO
    return pl.pallas_call(
        functools.partial(_pool_kernel, tt=tt),
        grid=(b, t // tt),
        in_specs=[
            pl.BlockSpec((1, tt, d), lambda bi, ti: (bi, ti, 0)),
            pl.BlockSpec((1, POOL_HALO, d), lambda bi, ti: (bi, jnp.maximum(ti * per - 1, 0), 0)),
            pl.BlockSpec((1, d), lambda bi, ti: (0, 0)),
            pl.BlockSpec(pool_w.shape, lambda bi, ti: (0, 0, 0)),
            pl.BlockSpec((1, d), lambda bi, ti: (0, 0)),
        ],
        out_specs=pl.BlockSpec((1, tt, d), lambda bi, ti: (bi, ti, 0)),
        out_shape=jax.ShapeDtypeStruct(h.shape, F32),
        scratch_shapes=[pltpu.VMEM((tt + POOL_HALO, d), F32)],
        compiler_params=_cparams(("parallel", "parallel"), VMEM_LIMIT),
        name="pool_mixer",
    )(h, h, g.reshape(1, d), pool_w.astype(BF16), pool_scale.reshape(1, d))


def _peer_route_kernel(q_ref, key_ref, idx_ref, gate_ref):
    q = q_ref[...]
    s = lax.dot_general(key_ref[0], q, (((1,), (1,)), ((), ())), preferred_element_type=F32,
                        precision=HIGHEST)
    v0, i0 = _topk_rows(s[0:N_KEYS], PEER_TOPK)
    v1, i1 = _topk_rows(s[N_KEYS:], PEER_TOPK)
    s0 = jnp.concatenate(v0, axis=0)
    s1 = jnp.concatenate(v1, axis=0)
    step = lax.broadcasted_iota(jnp.int32, (8, 1), 0)
    stepf = step.astype(F32)
    blocks, codes = [], []
    for a, b0 in ((0, 0), (0, 8), (1, 0), (2, 0), (3, 0)):
        blocks.append(v0[a] + s1[b0:b0 + 8])
        codes.append(stepf + float(a * PEER_TOPK + b0))
    for a0, b in ((8, 0), (0, 0), (0, 1), (0, 2)):
        dup = jnp.where((step < 4) & (a0 == 0), -jnp.inf, 0.0)
        blocks.append(s0[a0:a0 + 8] + v1[b] + dup)
        codes.append((stepf + float(a0)) * PEER_TOPK + float(b))
    cv, ci = _topk_rows(jnp.concatenate(blocks, axis=0), PEER_TOPK, jnp.concatenate(codes, axis=0))
    c_top = jnp.concatenate(cv, axis=0)
    c_idx = jnp.concatenate(ci, axis=0).astype(jnp.int32)
    a_sel = c_idx >> 4
    b_sel = c_idx & (PEER_TOPK - 1)
    k1 = jnp.zeros_like(c_top)
    k2 = jnp.zeros_like(c_top)
    for r in range(PEER_TOPK):
        k1 = jnp.where(a_sel == r, i0[r], k1)
        k2 = jnp.where(b_sel == r, i1[r], k2)
    idx_ref[0] = ((k1 * N_KEYS + k2) * ROW_SUB).astype(jnp.int32)
    e = jnp.exp(c_top - jnp.max(c_top, axis=0, keepdims=True))
    gate_ref[0] = e / jnp.sum(e, axis=0, keepdims=True)


def peer_route(q, keys_bd, *, tb=512):
    m = q.shape[0]
    shp = (PEER_HEADS, PEER_TOPK, m)
    ospec = pl.BlockSpec((1, PEER_TOPK, tb), lambda i, h: (h, 0, i))
    return pl.pallas_call(
        _peer_route_kernel,
        grid=(m // tb, PEER_HEADS),
        in_specs=[pl.BlockSpec((tb, 2 * HEAD_DIM), lambda i, h: (i, h)),
                  pl.BlockSpec((1, 2 * N_KEYS, 2 * HEAD_DIM), lambda i, h: (h, 0, 0))],
        out_specs=[ospec, ospec],
        out_shape=[jax.ShapeDtypeStruct(shp, jnp.int32), jax.ShapeDtypeStruct(shp, F32)],
        compiler_params=_cparams(("parallel", "parallel"), VMEM_LIMIT),
        name="peer_route",
    )(q, keys_bd)


def pack_table(tab):
    bits = lax.bitcast_convert_type(tab.astype(BF16), jnp.uint16).astype(jnp.uint32)
    packed = (bits[:, :HALF] << 16) | bits[:, HALF:]
    return packed.reshape(tab.shape[0] * ROW_SUB, 128)


STAGE_ROWS = PEER_KK * ROW_SUB
N_STAGE = 2
SLOT = 2 * ROW_SUB


def _slot_dims(c):
    start = (c // 2) * 128 + (0 if c % 2 else HALF)
    return slice(start, start + 128)


def _from_slots(y):
    slot_of = {(_slot_dims(c).start // 128): c for c in range(SLOT)}
    return jnp.concatenate([y[:, slot_of[blk] * 128:(slot_of[blk] + 1) * 128] for blk in range(SLOT)], axis=-1)


def _two_bf16(a, axis):
    hi = a.astype(BF16)
    lo = (a - hi.astype(F32)).astype(BF16)
    return jnp.concatenate([hi, lo], axis=axis)


def _for_both_blocks(tab_hbm, tab_vmem, idx_hbm, idx_smem, sems, run_block):
    i = pl.program_id(0)
    sets = (idx_smem[:PEER_KK], idx_smem[PEER_KK:])

    def copies(block, s):
        return [pltpu.make_async_copy(idx_hbm.at[k, block], sets[s][k], sems.at[1 + s * PEER_KK + k])
                for k in range(PEER_KK)]

    @pl.when(i == 0)
    def _():
        cp = pltpu.make_async_copy(tab_hbm, tab_vmem, sems.at[0])
        cp.start()
        for c in copies(0, 0):
            c.start()
        cp.wait()

    for c in copies(2 * i + 1, 1):
        c.start()
    for c in copies(2 * i, 0):
        c.wait()
    run_block(0, sets[0])

    @pl.when(i + 1 < pl.num_programs(0))
    def _():
        for c in copies(2 * i + 2, 0):
            c.start()

    for c in copies(2 * i + 1, 1):
        c.wait()
    run_block(1, sets[1])


def _for_each_token(tab_vmem, idx_smem, stages, tb, contract, finish):
    def gather(stage, t):
        for k in range(PEER_KK):
            row = pl.multiple_of(idx_smem[k][t], ROW_SUB)
            stage[pl.ds(ROW_SUB * k, ROW_SUB), :] = tab_vmem[pl.ds(row, ROW_SUB), :]

    gather(stages[0], 0)
    gather(stages[1], 1)

    def pair(i, carry):
        t = 2 * i
        part_a = contract(t, pltpu.bitcast(stages[0][...], BF16))
        part_b = contract(t + 1, pltpu.bitcast(stages[1][...], BF16))
        finish(t, 0, part_a)
        finish(t + 1, 1, part_b)
        gather(stages[0], jnp.minimum(t + 2, tb - 1))
        gather(stages[1], jnp.minimum(t + 3, tb - 1))
        return carry

    lax.fori_loop(0, tb // 2, pair, 0)


def _slot_mask():
    j = lax.broadcasted_iota(jnp.int32, (SLOT, 2 * STAGE_ROWS), 1)
    r = lax.broadcasted_iota(jnp.int32, (SLOT, 2 * STAGE_ROWS), 0)
    return (j & (SLOT - 1)) == r


def _peer_act_kernel(idx_hbm, tab_hbm, x_ref, gate_ref, fold_ref, coef_ref, tab_vmem, stages, zbuf, sems,
                     *idx_smem, tb, chunk):
    stages = [stages.at[n] for n in range(N_STAGE)]
    mask = _slot_mask()

    def run_block(half, ids):
        base = half * tb

        def contract(t, rows):
            return lax.dot_general(x_ref[base + t], rows, (((1,), (1,)), ((), ())), preferred_element_type=F32)

        def keep(t, pos, g):
            zbuf[t] = jnp.where(mask, g[0:SLOT] + g[SLOT:], 0.0)

        _for_each_token(tab_vmem, ids, stages, tb, contract, keep)

        def finish(c, carry):
            r0 = pl.multiple_of(c * chunk, chunk)
            z = zbuf[pl.ds(r0, chunk)].reshape(chunk * SLOT, 2 * STAGE_ROWS)
            part = jnp.dot(_two_bf16(z, 0), fold_ref[...], preferred_element_type=F32)
            rows = part[0:chunk * SLOT] + part[chunk * SLOT:]
            act = jnp.sum(rows.reshape(chunk, SLOT, PEER_KK), axis=1)
            coef_ref[pl.ds(base + r0, chunk), :] = gate_ref[pl.ds(base + r0, chunk), :] * _gelu(act)
            return carry

        lax.fori_loop(0, tb // chunk, finish, 0)

    _for_both_blocks(tab_hbm, tab_vmem, idx_hbm, idx_smem, sems, run_block)


def peer_coefficients(idx_t, tab, x_terms, gate, *, tb=256, chunk=32):
    m = gate.shape[0]
    n_j = 2 * STAGE_ROWS
    fold = (np.arange(n_j)[:, None] // SLOT == np.arange(PEER_KK)[None, :]).astype(np.float32)
    return pl.pallas_call(
        functools.partial(_peer_act_kernel, tb=tb, chunk=chunk),
        grid=(m // (2 * tb),),
        in_specs=[pl.BlockSpec(memory_space=pl.ANY), pl.BlockSpec(memory_space=pl.ANY),
                  pl.BlockSpec((2 * tb, 2 * SLOT, 128), lambda i: (i, 0, 0)),
                  pl.BlockSpec((2 * tb, PEER_KK), lambda i: (i, 0)),
                  pl.BlockSpec((n_j, PEER_KK), lambda i: (0, 0))],
        out_specs=pl.BlockSpec((2 * tb, PEER_KK), lambda i: (i, 0)),
        out_shape=jax.ShapeDtypeStruct((m, PEER_KK), F32),
        scratch_shapes=[pltpu.VMEM(tab.shape, jnp.uint32),
                        pltpu.VMEM((N_STAGE, STAGE_ROWS, 128), jnp.uint32),
                        pltpu.VMEM((tb, SLOT, n_j), F32),
                        pltpu.SemaphoreType.DMA((1 + 2 * PEER_KK,))] + [pltpu.SMEM((tb,), jnp.int32)] * (2 * PEER_KK),
        compiler_params=_cparams(("arbitrary",), VMEM_LIMIT),
        name="peer_coefficients",
    )(idx_t.reshape(PEER_KK, m // tb, tb), tab, x_terms, gate, jnp.asarray(fold, BF16))


def _peer_mix_kernel(idx_hbm, tab_hbm, coef_ref, spread_ref, y_ref, tab_vmem, stages, lhs, sems,
                     *idx_smem, tb, chunk):
    stages = [stages.at[n] for n in range(N_STAGE)]
    shape = (SLOT, 2 * STAGE_ROWS)
    sub = lax.broadcasted_iota(jnp.int32, shape, 0)
    col_slot = lax.broadcasted_iota(jnp.int32, shape, 1) & (SLOT - 1)
    low_slot = (sub & (ROW_SUB - 1)) * 2
    keep = (jnp.where(col_slot == low_slot, jnp.uint32(0x0000FFFF), jnp.uint32(0))
            | jnp.where(col_slot == low_slot + 1, jnp.uint32(0xFFFF0000), jnp.uint32(0)))
    first_term = sub < ROW_SUB

    def both_halves(a):
        bits = lax.bitcast_convert_type(a, jnp.uint32)
        return bits | (bits >> 16)

    def run_block(half, ids):
        base = half * tb

        def prepare(c, carry):
            r0 = pl.multiple_of(c * chunk, chunk)
            terms = _two_bf16(coef_ref[pl.ds(base + r0, chunk), :], 0)
            wide = jnp.dot(terms, spread_ref[...], preferred_element_type=F32)
            w_hi = both_halves(wide[0:chunk])
            w_lo = both_halves(wide[chunk:])
            for j in range(chunk):
                word = jnp.where(first_term, jnp.broadcast_to(w_hi[j:j + 1, :], shape),
                                 jnp.broadcast_to(w_lo[j:j + 1, :], shape))
                lhs[r0 + j] = word & keep
            return carry

        lax.fori_loop(0, tb // chunk, prepare, 0)

        def contract(t, rows):
            return jnp.dot(pltpu.bitcast(lhs[t], BF16), rows, preferred_element_type=F32)

        def store(t, pos, y):
            y_ref[base + t] = y[0:SLOT] + y[SLOT:]

        _for_each_token(tab_vmem, ids, stages, tb, contract, store)

    _for_both_blocks(tab_hbm, tab_vmem, idx_hbm, idx_smem, sems, run_block)


def peer_mix(idx_t, coef, tab, *, tb=256, chunk=32):
    m = coef.shape[0]
    n_j = 2 * STAGE_ROWS
    spread = (np.arange(PEER_KK)[:, None] == np.arange(n_j)[None, :] // SLOT).astype(np.float32)
    return pl.pallas_call(
        functools.partial(_peer_mix_kernel, tb=tb, chunk=chunk),
        grid=(m // (2 * tb),),
        in_specs=[pl.BlockSpec(memory_space=pl.ANY), pl.BlockSpec(memory_space=pl.ANY),
                  pl.BlockSpec((2 * tb, PEER_KK), lambda i: (i, 0)),
                  pl.BlockSpec((PEER_KK, n_j), lambda i: (0, 0))],
        out_specs=pl.BlockSpec((2 * tb, SLOT, 128), lambda i: (i, 0, 0)),
        out_shape=jax.ShapeDtypeStruct((m, SLOT, 128), F32),
        scratch_shapes=[pltpu.VMEM(tab.shape, jnp.uint32),
                        pltpu.VMEM((N_STAGE, STAGE_ROWS, 128), jnp.uint32),
                        pltpu.VMEM((tb, SLOT, n_j), jnp.uint32),
                        pltpu.SemaphoreType.DMA((1 + 2 * PEER_KK,))] + [pltpu.SMEM((tb,), jnp.int32)] * (2 * PEER_KK),
        compiler_params=_cparams(("arbitrary",), VMEM_LIMIT),
        name="peer_mix",
    )(idx_t.reshape(PEER_KK, m // tb, tb), tab, coef, jnp.asarray(spread, BF16))


def _ple_kernel(h_ref, y_ref, p_ref, g_ref, wg_ref, wp_ref, fg_ref, o_ref, *, final_norm):
    h = h_ref[...] + _from_slots(y_ref[...])
    gate = _sigmoid(jnp.dot(_rms(h, g_ref[...]).astype(BF16), wg_ref[...], preferred_element_type=F32))
    h = h + jnp.dot(p_ref[...].astype(BF16), wp_ref[...], preferred_element_type=F32) * gate
    if final_norm:
        h = _rms(h, fg_ref[...])
    o_ref[...] = h


def ple_residual(h, y, p, g, wg, wp, fg, *, final_norm, tm=512):
    m, d = h.shape
    row = lambda w: pl.BlockSpec((tm, w), lambda i: (i, 0))
    whole = lambda a: pl.BlockSpec(a.shape, lambda i: (0, 0))
    g2, fg2 = g.reshape(1, d), fg.reshape(1, d)
    return pl.pallas_call(
        functools.partial(_ple_kernel, final_norm=final_norm),
        grid=(m // tm,),
        in_specs=[row(d), row(d), row(p.shape[1]), whole(g2), whole(wg), whole(wp), whole(fg2)],
        out_specs=row(d),
        out_shape=jax.ShapeDtypeStruct((m, d), F32),
        compiler_params=_cparams(("parallel",), VMEM_LIMIT),
        name="ple_residual",
    )(h, y, p, g2, wg, wp, fg2)


def _mixer_conv_nsa(h, b, t, norm_g, w_in, conv_w, conv_b, ln_g, ln_b, cmp_pos, cmp_w1, cmp_w2, w_out):
    m = b * t
    n_q = NSA_KV_HEADS * NSA_HPG * HEAD_DIM
    kv_w = NSA_KV_HEADS * HEAD_DIM
    c0, c1, c2 = 2 * CONV_CH, 2 * CONV_CH + n_q, 2 * CONV_CH + n_q + 6 * kv_w
    wb = w_in.astype(BF16)
    vg, q, kv, gt = norm_matmul(h, norm_g, [wb[:, :c0], wb[:, c0:c1], wb[:, c1:c2], wb[:, c2:]])
    a_out = conformer_conv(vg.reshape(b, t, c0), conv_w, conv_b, ln_g, ln_b)

    kv = kv.reshape(b, t, 6, kv_w)
    kind = lambda c: kv[:, :, c]
    n_chunk = t // CMP_STRIDE
    cmp_src = jnp.stack([kind(0), kind(1)]).reshape(2, b, n_chunk, CMP_STRIDE * kv_w)
    cmp = compress_blocks(cmp_src, cmp_pos, cmp_w1, cmp_w2)

    lane_group = jnp.arange(kv_w) // HEAD_DIM

    def value_operand(v):
        per_group = []
        for g in range(NSA_KV_HEADS):
            other = jnp.where(jnp.arange(kv_w) == HEAD_DIM * (1 - g), 1.0, 0.0)
            per_group.append(jnp.where(lane_group == g, v, other).astype(BF16))
        return jnp.stack(per_group, axis=1)

    o = nsa_attention(q.reshape(b, t, n_q), cmp[0], cmp[1], kind(2).astype(BF16), value_operand(kind(3)),
                      kind(4).astype(BF16), value_operand(kind(5)), gt.reshape(b, t, -1))
    wo = w_out.astype(BF16)
    return mm2_residual(h, a_out.reshape(m, CONV_CH), o.reshape(m, n_q), wo[:CONV_CH], wo[CONV_CH:])


def _peer_ffn(h, norm_g, wq, subkeys, u_tab, v_tab):
    q, x_terms = norm_matmul(h, norm_g, [wq.astype(BF16)], emit_hn=True)
    zeros = jnp.zeros_like(subkeys[:, 0])
    keys_bd = jnp.concatenate([jnp.concatenate([subkeys[:, 0], zeros], axis=-1),
                               jnp.concatenate([zeros, subkeys[:, 1]], axis=-1)], axis=1)
    idx_t, gate_t = peer_route(q, keys_bd)
    m = h.shape[0]
    idx_t = idx_t.reshape(PEER_KK, m)
    gate = gate_t.reshape(PEER_KK, m).T
    coef = peer_coefficients(idx_t, pack_table(u_tab), x_terms.reshape(m, 2 * SLOT, 128), gate)
    return peer_mix(idx_t, coef, pack_table(v_tab)).reshape(m, SLOT * 128)


def kernel(x, p, mix_norm, ab_w_in, ab_conv_w, ab_conv_b, ab_conv_ln_g, ab_conv_ln_b, ab_cmp_pos, ab_cmp_w1,
           ab_cmp_w2, ab_w_out, pool_w, pool_scale, ffn_norm, peer_wq, peer_subkeys, peer_u, peer_v, ple_norm,
           ple_gate_w, ple_proj, final_norm):
    b, t, d = x.shape
    m = b * t
    depth = p.shape[0]
    h = x.reshape(m, d)
    for i in range(depth):
        j = i // 2
        if i % 2 == 0:
            h = _mixer_conv_nsa(h, b, t, mix_norm[i], ab_w_in[j], ab_conv_w[j], ab_conv_b[j], ab_conv_ln_g[j],
                                ab_conv_ln_b[j], ab_cmp_pos[j], ab_cmp_w1[j], ab_cmp_w2[j], ab_w_out[j])
        else:
            h = pool_mixer_residual(h.reshape(b, t, d), mix_norm[i], pool_w[j], pool_scale[j]).reshape(m, d)
        y = _peer_ffn(h, ffn_norm[i], peer_wq[i], peer_subkeys[i], peer_u[i], peer_v[i])
        h = ple_residual(h, y, p[i].reshape(m, -1), ple_norm[i], ple_gate_w[i].astype(BF16),
                         ple_proj[i].astype(BF16), final_norm, final_norm=(i == depth - 1))
    return h.reshape(b, t, d)
```

```python
import functools

import jax
import jax.numpy as jnp
import numpy as np
from jax import lax
from jax.experimental import pallas as pl
from jax.experimental.pallas import tpu as pltpu

F32 = jnp.float32
BF16 = jnp.bfloat16
HIGHEST = lax.Precision.HIGHEST

D_MODEL = 1024
NORM_EPS = 1e-6
NEG_INF = -1e30

CONV_CH = 512
CONV_WIDTH = 31
CONV_HALO = 32

HEAD_DIM = 64
NSA_KV_HEADS = 2
NSA_HPG = 4
CMP_LEN = 32
CMP_STRIDE = 16
CMP_HIDDEN = 256
SLC_BLOCK = 64
SLC_SHIFT = 6
SLC_TOP = 16
WINDOW = 512
FORCE_SCORE = 1e9

POOL_WINDOWS = (2, 4, 8, 16)
POOL_GROUP = 256
POOL_HALO = 16

PEER_HEADS = 8
N_KEYS = 128
PEER_TOPK = 16
PEER_KK = PEER_HEADS * PEER_TOPK
HALF = D_MODEL // 2
ROW_SUB = HALF // 128

VMEM_LIMIT = 56 * 1024 * 1024


def _cparams(sem, vmem=None):
    return pltpu.CompilerParams(dimension_semantics=sem, vmem_limit_bytes=vmem)


def _rms(x, g):
    return x * lax.rsqrt(jnp.mean(x * x, axis=-1, keepdims=True) + NORM_EPS) * g


def _gelu(x):
    return 0.5 * x * (1.0 + jnp.tanh(0.7978845608028654 * (x + 0.044715 * (x * x * x))))


def _sigmoid(x):
    return 1.0 / (1.0 + jnp.exp(-x))


def _norm_mm_kernel(x_ref, g_ref, *refs, n_w, emit_hn):
    w_refs = refs[:n_w]
    o_refs = refs[n_w:]
    y = _rms(x_ref[...], g_ref[...])
    yb = y.astype(BF16)
    for w_ref, o_ref in zip(w_refs, o_refs[:n_w]):
        o_ref[...] = jnp.dot(yb, w_ref[...], preferred_element_type=F32)
    if emit_hn:
        lo = (y - yb.astype(F32)).astype(BF16)
        blocks = [term[:, _slot_dims(c)] for term in (yb, lo) for c in range(SLOT)]
        o_refs[n_w][...] = jnp.concatenate(blocks, axis=-1)


def norm_matmul(x, g, ws, *, emit_hn=False, tm=512):
    m, d = x.shape
    n_w = len(ws)
    in_specs = [pl.BlockSpec((tm, d), lambda i: (i, 0)), pl.BlockSpec((1, d), lambda i: (0, 0))]
    in_specs += [pl.BlockSpec(w.shape, lambda i: (0, 0)) for w in ws]
    out_shape = [jax.ShapeDtypeStruct((m, w.shape[1]), F32) for w in ws]
    out_specs = [pl.BlockSpec((tm, w.shape[1]), lambda i: (i, 0)) for w in ws]
    if emit_hn:
        out_shape += [jax.ShapeDtypeStruct((m, 2 * d), BF16)]
        out_specs += [pl.BlockSpec((tm, 2 * d), lambda i: (i, 0))]
    return pl.pallas_call(
        functools.partial(_norm_mm_kernel, n_w=n_w, emit_hn=emit_hn),
        grid=(m // tm,),
        in_specs=in_specs, out_specs=out_specs, out_shape=out_shape,
        compiler_params=_cparams(("parallel",), VMEM_LIMIT),
        name="norm_matmul",
    )(x, g.reshape(1, d), *ws)


def _conv_kernel(cur_ref, halo_ref, w_ref, b_ref, g_ref, beta_ref, o_ref, buf, *, tt):
    t = pl.program_id(1)
    cur = cur_ref[0]
    halo = halo_ref[0]
    a_halo = halo[:, :CONV_CH] * _sigmoid(halo[:, CONV_CH:])
    buf[0:CONV_HALO, :] = jnp.where(t > 0, a_halo, 0.0)
    buf[CONV_HALO:, :] = cur[:, :CONV_CH] * _sigmoid(cur[:, CONV_CH:])
    acc = jnp.zeros((tt, CONV_CH), F32)
    first = CONV_HALO - (CONV_WIDTH - 1)
    for j in range(CONV_WIDTH):
        acc = acc + buf[pl.ds(first + j, tt), :] * w_ref[j:j + 1, :]
    y = acc + b_ref[...]
    mu = jnp.mean(y, axis=-1, keepdims=True)
    yc = y - mu
    var = jnp.mean(yc * yc, axis=-1, keepdims=True)
    y = yc * lax.rsqrt(var + NORM_EPS) * g_ref[...] + beta_ref[...]
    o_ref[0] = y * _sigmoid(y)


def conformer_conv(vg, conv_w, conv_b, ln_g, ln_b, *, tt=512):
    b, t, _ = vg.shape
    tt = min(tt, t)
    per = tt // CONV_HALO
    vec = lambda v: v.reshape(1, CONV_CH)
    vspec = pl.BlockSpec((1, CONV_CH), lambda bi, ti: (0, 0))
    return pl.pallas_call(
        functools.partial(_conv_kernel, tt=tt),
        grid=(b, t // tt),
        in_specs=[
            pl.BlockSpec((1, tt, 2 * CONV_CH), lambda bi, ti: (bi, ti, 0)),
            pl.BlockSpec((1, CONV_HALO, 2 * CONV_CH), lambda bi, ti: (bi, jnp.maximum(ti * per - 1, 0), 0)),
            pl.BlockSpec((CONV_WIDTH, CONV_CH), lambda bi, ti: (0, 0)),
            vspec, vspec, vspec,
        ],
        out_specs=pl.BlockSpec((1, tt, CONV_CH), lambda bi, ti: (bi, ti, 0)),
        out_shape=jax.ShapeDtypeStruct((b, t, CONV_CH), F32),
        scratch_shapes=[pltpu.VMEM((tt + CONV_HALO, CONV_CH), F32)],
        compiler_params=_cparams(("parallel", "parallel"), VMEM_LIMIT),
        name="conformer_conv",
    )(vg, vg, conv_w, vec(conv_b), vec(ln_g), vec(ln_b))


def _compress_kernel(c_ref, pos_ref, w1_ref, w2_ref, o_ref):
    c = c_ref[0, 0]
    n = c.shape[0]
    ca = c + pos_ref[0, 0]
    cb = c + pos_ref[0, 1]
    for g in range(NSA_KV_HEADS):
        ua = jnp.dot(ca, w1_ref[0, g, 0], preferred_element_type=F32, precision=HIGHEST)
        ub = jnp.dot(cb, w1_ref[0, g, 1], preferred_element_type=F32, precision=HIGHEST)
        hid = _gelu(ua + pltpu.roll(ub, n - 1, axis=0))
        o_ref[0, 0, g] = jnp.dot(hid, w2_ref[0], preferred_element_type=F32, precision=HIGHEST)


def compress_blocks(src, pos, w1, w2):
    _, b, n_chunk, width = src.shape
    groups = NSA_KV_HEADS
    posx = jnp.broadcast_to(pos.reshape(2, 2, CMP_STRIDE, 1, HEAD_DIM), (2, 2, CMP_STRIDE, groups, HEAD_DIM))
    posx = posx.reshape(2, 2, 1, width)
    w1r = w1.reshape(2, 1, 2, CMP_STRIDE, 1, HEAD_DIM, CMP_HIDDEN)
    own = (jnp.arange(groups)[:, None] == jnp.arange(groups)[None, :]).reshape(1, groups, 1, 1, groups, 1, 1)
    w1x = jnp.where(own, w1r, 0.0).reshape(2, groups, 2, width, CMP_HIDDEN)
    return pl.pallas_call(
        _compress_kernel,
        grid=(2, b),
        in_specs=[
            pl.BlockSpec((1, 1, n_chunk, width), lambda k, i: (k, i, 0, 0)),
            pl.BlockSpec((1, 2, 1, width), lambda k, i: (k, 0, 0, 0)),
            pl.BlockSpec((1, groups, 2, width, CMP_HIDDEN), lambda k, i: (k, 0, 0, 0, 0)),
            pl.BlockSpec((1, CMP_HIDDEN, HEAD_DIM), lambda k, i: (k, 0, 0)),
        ],
        out_specs=pl.BlockSpec((1, 1, groups, n_chunk, HEAD_DIM), lambda k, i: (k, i, 0, 0, 0)),
        out_shape=jax.ShapeDtypeStruct((2, b, groups, n_chunk, HEAD_DIM), F32),
        compiler_params=_cparams(("parallel", "parallel"), VMEM_LIMIT),
        name="nsa_compress",
    )(src, posx, w1x, w2)


def _topk_rows(work, k, row=None):
    if row is None:
        row = lax.broadcasted_iota(jnp.int32, work.shape, 0).astype(F32)
    else:
        row = jnp.broadcast_to(row, work.shape)
    vals, idxs = [], []
    for _ in range(k):
        m = jnp.max(work, axis=0, keepdims=True)
        first = jnp.min(jnp.where(work == m, row, 3.0e38), axis=0, keepdims=True)
        vals.append(m)
        idxs.append(first)
        work = jnp.where(row == first, -jnp.inf, work)
    return vals, idxs


def _sorting_network(n):
    pairs = []
    p = 1
    while p < n:
        k = p
        while k >= 1:
            for j in range(k % p, n - k, 2 * k):
                for i in range(min(k, n - j - k)):
                    if (i + j) // (2 * p) == (i + j + k) // (2 * p):
                        pairs.append((i + j, i + j + k))
            k //= 2
        p *= 2
    return pairs


def _topk_rows_by_columns(work, k):
    n = work.shape[0] // 8
    lanes = work.shape[1]
    sub = lax.broadcasted_iota(jnp.int32, (8, lanes), 0).astype(F32)
    vals = [work[8 * i:8 * (i + 1)] for i in range(n)]
    ids = [sub + float(8 * i) for i in range(n)]
    for a, b in _sorting_network(n):
        first = (vals[a] > vals[b]) | ((vals[a] == vals[b]) & (ids[a] < ids[b]))
        vals[a], vals[b] = jnp.where(first, vals[a], vals[b]), jnp.where(first, vals[b], vals[a])
        ids[a], ids[b] = jnp.where(first, ids[a], ids[b]), jnp.where(first, ids[b], ids[a])
    out_v, out_i = [], []
    for r in range(k):
        m = jnp.max(vals[0], axis=0, keepdims=True)
        pick = jnp.min(jnp.where(vals[0] == m, ids[0], 3.0e38), axis=0, keepdims=True)
        out_v.append(m)
        out_i.append(pick)
        won = ids[0] == pick
        for i in range(k - 1 - r):
            vals[i] = jnp.where(won, vals[i + 1], vals[i])
            ids[i] = jnp.where(won, ids[i + 1], ids[i])
    return out_v, out_i


def _masked_flash(qb, k_ref, v_ref, lo, hi, tk, mask_fn, first_group):
    rows = qb.shape[0]
    tq = rows // NSA_HPG

    def scores(j):
        kt = k_ref[0, pl.ds(pl.multiple_of(j * tk, tk), tk), :]
        s = lax.dot_general(qb, kt, (((1,), (1,)), ((), ())), preferred_element_type=F32)
        return (s.reshape(NSA_HPG, tq, tk) + mask_fn(j)[None]).reshape(rows, tk)

    def body(j, carry):
        m, acc, s = carry
        s_next = scores(jnp.minimum(j + 1, hi - 1))
        vt = v_ref[0, 0, pl.ds(pl.multiple_of(j * tk, tk), tk), :]
        m_new = jnp.maximum(m, jnp.max(s, axis=-1, keepdims=True))
        p = jnp.exp(s - m_new)
        acc = jnp.exp(m - m_new) * acc + jnp.dot(p.astype(BF16), vt, preferred_element_type=F32)
        return m_new, acc, s_next

    init = (jnp.full((rows, 1), NEG_INF, F32), jnp.zeros((rows, 2 * HEAD_DIM), F32), scores(lo))
    _, acc, _ = lax.fori_loop(lo, hi, body, init)
    left, right = acc[:, :HEAD_DIM], acc[:, HEAD_DIM:]
    return jnp.where(first_group, left / right[:, 0:1], right / left[:, 0:1])


def _nsa_kernel(q_ref, kc_ref, vc_ref, ks_ref, vs_ref, kw_ref, vw_ref, g_ref, ov_ref, o_ref, *, tq, tks, tkw):
    i = pl.program_id(2)
    first_group = pl.program_id(1) == 0
    t0 = i * tq
    rows = NSA_HPG * tq
    q_tile = q_ref[0]
    q = jnp.concatenate([q_tile[:, h * HEAD_DIM:(h + 1) * HEAD_DIM] for h in range(NSA_HPG)], axis=0)
    q = q * (HEAD_DIM ** -0.5)
    zero = jnp.zeros_like(q)
    qb = jnp.where(first_group, jnp.concatenate([q, zero], axis=-1),
                   jnp.concatenate([zero, q], axis=-1)).astype(BF16)
    t_q = t0 + lax.broadcasted_iota(jnp.int32, (tq, 1), 0)
    t_all = jnp.concatenate([t_q] * NSA_HPG, axis=0)

    kc = kc_ref[0, 0]
    n_cmp = kc.shape[0]
    s = lax.dot_general(q, kc, (((1,), (1,)), ((), ())), preferred_element_type=F32, precision=HIGHEST)
    cmp_end = lax.broadcasted_iota(jnp.int32, (1, n_cmp), 1) * CMP_STRIDE + (CMP_LEN - 1)
    ok = cmp_end <= t_all
    s = jnp.where(ok, s, NEG_INF)
    e = jnp.where(ok, jnp.exp(s - jnp.max(s, axis=-1, keepdims=True)), 0.0)
    den = jnp.sum(e, axis=-1, keepdims=True)
    p_cmp = e / jnp.where(den > 0.0, den, 1.0)
    o_cmp = jnp.dot(p_cmp.astype(BF16), vc_ref[0, 0].astype(BF16), preferred_element_type=F32)

    p_sum = p_cmp[0:tq]
    for h in range(1, NSA_HPG):
        p_sum = p_sum + p_cmp[h * tq:(h + 1) * tq]
    imp = jnp.dot(p_sum, ov_ref[...], preferred_element_type=F32, precision=HIGHEST)
    n_slc = ks_ref.shape[1] // SLC_BLOCK
    imp_t = imp.T[0:n_slc]
    blk = lax.broadcasted_iota(jnp.int32, (n_slc, tq), 0)
    t_lane = t0 + lax.broadcasted_iota(jnp.int32, (n_slc, tq), 1)
    cur = t_lane >> SLC_SHIFT
    forced = (blk == 0) | (blk == cur) | (blk == cur - 1)
    imp_t = jnp.where(forced, FORCE_SCORE, imp_t)
    imp_t = jnp.where(blk * SLC_BLOCK <= t_lane, imp_t, NEG_INF)
    _, picks = _topk_rows(imp_t, min(SLC_TOP, n_slc))
    blk_f = blk.astype(F32)
    member = jnp.zeros((n_slc, tq), F32)
    for pk in picks:
        member = jnp.where(blk_f == pk, 1.0, member)
    if n_slc < 128:
        member = jnp.concatenate([member, jnp.zeros((128 - n_slc, tq), F32)], axis=0)
    member_q = member.T.astype(BF16)

    blocks_per_tile = tks // SLC_BLOCK

    def slc_mask(j):
        sel_row = lax.broadcasted_iota(jnp.int32, (128, tks), 0)
        key_blk = j * blocks_per_tile + (lax.broadcasted_iota(jnp.int32, (128, tks), 1) >> SLC_SHIFT)
        expand = jnp.where(sel_row == key_blk, 1.0, 0.0).astype(BF16)
        sel = jnp.dot(member_q, expand, preferred_element_type=F32) > 0.5
        kpos = j * tks + lax.broadcasted_iota(jnp.int32, (1, tks), 1)
        return jnp.where(sel & (kpos <= t_q), 0.0, NEG_INF)

    o_slc = _masked_flash(qb, ks_ref, vs_ref, 0, (t0 + tq + tks - 1) // tks, tks, slc_mask, first_group)

    def win_mask(j):
        kpos = j * tkw + lax.broadcasted_iota(jnp.int32, (1, tkw), 1)
        dist = t_q - kpos
        return jnp.where((dist >= 0) & (dist < WINDOW), 0.0, NEG_INF)

    lo = jnp.maximum(t0 - (WINDOW - 1), 0) // tkw
    o_win = _masked_flash(qb, kw_ref, vw_ref, lo, (t0 + tq + tkw - 1) // tkw, tkw, win_mask, first_group)

    gate = _sigmoid(g_ref[0])
    n_gate = 3 * NSA_HPG
    gate = jnp.where(first_group, gate[:, :n_gate], gate[:, n_gate:])
    heads = []
    for h in range(NSA_HPG):
        r = slice(h * tq, (h + 1) * tq)
        heads.append(gate[:, 3 * h:3 * h + 1] * o_cmp[r] + gate[:, 3 * h + 1:3 * h + 2] * o_slc[r]
                     + gate[:, 3 * h + 2:3 * h + 3] * o_win[r])
    o_ref[0] = jnp.concatenate(heads, axis=-1)


def nsa_attention(q, kc, vc, ks, vs, kw, vw, gates, *, tq=256, tks=256, tkw=128):
    b, t, _ = q.shape
    g = NSA_KV_HEADS
    n_chunk = kc.shape[2]
    n_slc = t // SLC_BLOCK
    n_cmp = (t - CMP_LEN) // CMP_STRIDE + 1
    tks = min(tks, t)
    cmp_start = np.arange(n_chunk) * CMP_STRIDE
    slc_start = np.arange(128) * SLC_BLOCK
    overlap = ((cmp_start[:, None] < slc_start[None, :] + SLC_BLOCK)
               & (cmp_start[:, None] + CMP_LEN > slc_start[None, :])
               & (np.arange(n_chunk)[:, None] < n_cmp) & (np.arange(128)[None, :] < n_slc))
    overlap = jnp.asarray(overlap.astype(np.float32))
    per_group = lambda a: pl.BlockSpec((1, 1) + a.shape[2:], lambda bi, gi, i: (bi, gi, 0, 0))
    shared = lambda a: pl.BlockSpec((1,) + a.shape[1:], lambda bi, gi, i: (bi, 0, 0))
    heads_w = NSA_HPG * HEAD_DIM
    return pl.pallas_call(
        functools.partial(_nsa_kernel, tq=tq, tks=tks, tkw=tkw),
        grid=(b, g, t // tq),
        in_specs=[
            pl.BlockSpec((1, tq, heads_w), lambda bi, gi, i: (bi, i, gi)),
            per_group(kc), per_group(vc), shared(ks), per_group(vs), shared(kw), per_group(vw),
            pl.BlockSpec((1, tq, g * 3 * NSA_HPG), lambda bi, gi, i: (bi, i, 0)),
            pl.BlockSpec((n_chunk, 128), lambda bi, gi, i: (0, 0)),
        ],
        out_specs=pl.BlockSpec((1, tq, heads_w), lambda bi, gi, i: (bi, i, gi)),
        out_shape=jax.ShapeDtypeStruct(q.shape, F32),
        compiler_params=_cparams(("parallel", "parallel", "arbitrary"), VMEM_LIMIT),
        name="nsa_attention",
    )(q, kc, vc, ks, vs, kw, vw, gates, overlap)


def _mm2_res_kernel(h_ref, a_ref, b_ref, wa_ref, wb_ref, o_ref):
    acc = jnp.dot(a_ref[...].astype(BF16), wa_ref[...], preferred_element_type=F32)
    acc = acc + jnp.dot(b_ref[...].astype(BF16), wb_ref[...], preferred_element_type=F32)
    o_ref[...] = h_ref[...] + acc


def mm2_residual(h, a, b, wa, wb, *, tm=512):
    m, d = h.shape
    row = lambda w: pl.BlockSpec((tm, w), lambda i: (i, 0))
    whole = lambda w: pl.BlockSpec(w.shape, lambda i: (0, 0))
    return pl.pallas_call(
        _mm2_res_kernel,
        grid=(m // tm,),
        in_specs=[row(d), row(a.shape[1]), row(b.shape[1]), whole(wa), whole(wb)],
        out_specs=row(d),
        out_shape=jax.ShapeDtypeStruct((m, d), F32),
        compiler_params=_cparams(("parallel",), VMEM_LIMIT),
        name="out_proj_residual",
    )(h, a, b, wa, wb)


def _pool_kernel(cur_ref, halo_ref, g_ref, w_ref, sc_ref, o_ref, buf, *, tt):
    t = pl.program_id(1)
    g = g_ref[...]
    cur = cur_ref[0]
    hn = _rms(cur, g)
    buf[0:POOL_HALO, :] = jnp.where(t > 0, _rms(halo_ref[0], g), 0.0)
    buf[POOL_HALO:, :] = hn
    pos = t * tt + lax.broadcasted_iota(jnp.int32, (tt, 1), 0)
    outs = []
    for gi, w in enumerate(POOL_WINDOWS):
        cols = slice(gi * POOL_GROUP, (gi + 1) * POOL_GROUP)
        tot = hn[:, cols]
        for j in range(1, w):
            tot = tot + buf[pl.ds(POOL_HALO - j, tt), cols]
        cnt = jnp.minimum(pos + 1, w).astype(F32)
        d = tot / cnt - hn[:, cols]
        outs.append(jnp.dot(d.astype(BF16), w_ref[gi], preferred_element_type=F32))
    y = jnp.concatenate(outs, axis=-1) * sc_ref[...]
    o_ref[0] = cur + y


def pool_mixer_residual(h, g, pool_w, pool_scale, *, tt=512):
    b, t, d = h.shape
    tt = min(tt, t)
    per = tt // POOL_HALO
    return pl.pallas_call(
        functools.partial(_pool_kernel, tt=tt),
        grid=(b, t // tt),
        in_specs=[
            pl.BlockSpec((1, tt, d), lambda bi, ti: (bi, ti, 0)),
            pl.BlockSpec((1, POOL_HALO, d), lambda bi, ti: (bi, jnp.maximum(ti * per - 1, 0), 0)),
            pl.BlockSpec((1, d), lambda bi, ti: (0, 0)),
            pl.BlockSpec(pool_w.shape, lambda bi, ti: (0, 0, 0)),
            pl.BlockSpec((1, d), lambda bi, ti: (0, 0)),
        ],
        out_specs=pl.BlockSpec((1, tt, d), lambda bi, ti: (bi, ti, 0)),
        out_shape=jax.ShapeDtypeStruct(h.shape, F32),
        scratch_shapes=[pltpu.VMEM((tt + POOL_HALO, d), F32)],
        compiler_params=_cparams(("parallel", "parallel"), VMEM_LIMIT),
        name="pool_mixer",
    )(h, h, g.reshape(1, d), pool_w.astype(BF16), pool_scale.reshape(1, d))


def _peer_route_kernel(q_ref, key_ref, idx_ref, gate_ref):
    q = q_ref[...]
    s = lax.dot_general(key_ref[0], q, (((1,), (1,)), ((), ())), preferred_element_type=F32,
                        precision=HIGHEST)
    v0, i0 = _topk_rows_by_columns(s[0:N_KEYS], PEER_TOPK)
    v1, i1 = _topk_rows_by_columns(s[N_KEYS:], PEER_TOPK)
    s0 = jnp.concatenate(v0, axis=0)
    s1 = jnp.concatenate(v1, axis=0)
    step = lax.broadcasted_iota(jnp.int32, (8, 1), 0)
    stepf = step.astype(F32)
    blocks, codes = [], []
    for a, b0 in ((0, 0), (0, 8), (1, 0), (2, 0), (3, 0)):
        blocks.append(v0[a] + s1[b0:b0 + 8])
        codes.append(stepf + float(a * PEER_TOPK + b0))
    for a0, b in ((8, 0), (0, 0), (0, 1), (0, 2)):
        dup = jnp.where((step < 4) & (a0 == 0), -jnp.inf, 0.0)
        blocks.append(s0[a0:a0 + 8] + v1[b] + dup)
        codes.append((stepf + float(a0)) * PEER_TOPK + float(b))
    cv, ci = _topk_rows(jnp.concatenate(blocks, axis=0), PEER_TOPK, jnp.concatenate(codes, axis=0))
    c_top = jnp.concatenate(cv, axis=0)
    c_idx = jnp.concatenate(ci, axis=0).astype(jnp.int32)
    a_sel = c_idx >> 4
    b_sel = c_idx & (PEER_TOPK - 1)
    k1 = jnp.zeros_like(c_top)
    k2 = jnp.zeros_like(c_top)
    for r in range(PEER_TOPK):
        k1 = jnp.where(a_sel == r, i0[r], k1)
        k2 = jnp.where(b_sel == r, i1[r], k2)
    idx_ref[0] = ((k1 * N_KEYS + k2) * ROW_SUB).astype(jnp.int32)
    e = jnp.exp(c_top - jnp.max(c_top, axis=0, keepdims=True))
    gate_ref[0] = e / jnp.sum(e, axis=0, keepdims=True)


def peer_route(q, keys_bd, *, tb=512):
    m = q.shape[0]
    shp = (PEER_HEADS, PEER_TOPK, m)
    ospec = pl.BlockSpec((1, PEER_TOPK, tb), lambda i, h: (h, 0, i))
    return pl.pallas_call(
        _peer_route_kernel,
        grid=(m // tb, PEER_HEADS),
        in_specs=[pl.BlockSpec((tb, 2 * HEAD_DIM), lambda i, h: (i, h)),
                  pl.BlockSpec((1, 2 * N_KEYS, 2 * HEAD_DIM), lambda i, h: (h, 0, 0))],
        out_specs=[ospec, ospec],
        out_shape=[jax.ShapeDtypeStruct(shp, jnp.int32), jax.ShapeDtypeStruct(shp, F32)],
        compiler_params=_cparams(("parallel", "parallel"), VMEM_LIMIT),
        name="peer_route",
    )(q, keys_bd)


def pack_table(tab):
    bits = lax.bitcast_convert_type(tab.astype(BF16), jnp.uint16).astype(jnp.uint32)
    packed = (bits[:, :HALF] << 16) | bits[:, HALF:]
    return packed.reshape(tab.shape[0] * ROW_SUB, 128)


STAGE_ROWS = PEER_KK * ROW_SUB
N_STAGE = 2
SLOT = 2 * ROW_SUB


def _slot_dims(c):
    start = (c // 2) * 128 + (0 if c % 2 else HALF)
    return slice(start, start + 128)


def _from_slots(y):
    slot_of = {(_slot_dims(c).start // 128): c for c in range(SLOT)}
    return jnp.concatenate([y[:, slot_of[blk] * 128:(slot_of[blk] + 1) * 128] for blk in range(SLOT)], axis=-1)


def _two_bf16(a, axis):
    hi = a.astype(BF16)
    lo = (a - hi.astype(F32)).astype(BF16)
    return jnp.concatenate([hi, lo], axis=axis)


def _for_both_blocks(tab_hbm, tab_vmem, idx_hbm, idx_smem, sems, run_block):
    i = pl.program_id(0)
    sets = (idx_smem[:PEER_KK], idx_smem[PEER_KK:])

    def copies(block, s):
        return [pltpu.make_async_copy(idx_hbm.at[k, block], sets[s][k], sems.at[1 + s * PEER_KK + k])
                for k in range(PEER_KK)]

    @pl.when(i == 0)
    def _():
        cp = pltpu.make_async_copy(tab_hbm, tab_vmem, sems.at[0])
        cp.start()
        for c in copies(0, 0):
            c.start()
        cp.wait()

    for c in copies(2 * i + 1, 1):
        c.start()
    for c in copies(2 * i, 0):
        c.wait()
    run_block(0, sets[0])

    @pl.when(i + 1 < pl.num_programs(0))
    def _():
        for c in copies(2 * i + 2, 0):
            c.start()

    for c in copies(2 * i + 1, 1):
        c.wait()
    run_block(1, sets[1])


def _for_each_token(tab_vmem, idx_smem, stages, tb, contract, finish):
    def gather(stage, t):
        for k in range(PEER_KK):
            row = pl.multiple_of(idx_smem[k][t], ROW_SUB)
            stage[pl.ds(ROW_SUB * k, ROW_SUB), :] = tab_vmem[pl.ds(row, ROW_SUB), :]

    gather(stages[0], 0)
    gather(stages[1], 1)

    def pair(i, carry):
        t = 2 * i
        part_a = contract(t, pltpu.bitcast(stages[0][...], BF16))
        part_b = contract(t + 1, pltpu.bitcast(stages[1][...], BF16))
        finish(t, 0, part_a)
        finish(t + 1, 1, part_b)
        gather(stages[0], jnp.minimum(t + 2, tb - 1))
        gather(stages[1], jnp.minimum(t + 3, tb - 1))
        return carry

    lax.fori_loop(0, tb // 2, pair, 0)


def _slot_mask():
    j = lax.broadcasted_iota(jnp.int32, (SLOT, 2 * STAGE_ROWS), 1)
    r = lax.broadcasted_iota(jnp.int32, (SLOT, 2 * STAGE_ROWS), 0)
    return (j & (SLOT - 1)) == r


def _peer_act_kernel(idx_hbm, tab_hbm, x_ref, gate_ref, fold_ref, coef_ref, tab_vmem, stages, zbuf, sems,
                     *idx_smem, tb, chunk):
    stages = [stages.at[n] for n in range(N_STAGE)]
    mask = _slot_mask()

    def run_block(half, ids):
        base = half * tb

        def contract(t, rows):
            return lax.dot_general(x_ref[base + t], rows, (((1,), (1,)), ((), ())), preferred_element_type=F32)

        def keep(t, pos, g):
            zbuf[t] = jnp.where(mask, g[0:SLOT] + g[SLOT:], 0.0)

        _for_each_token(tab_vmem, ids, stages, tb, contract, keep)

        def finish(c, carry):
            r0 = pl.multiple_of(c * chunk, chunk)
            z = zbuf[pl.ds(r0, chunk)].reshape(chunk * SLOT, 2 * STAGE_ROWS)
            part = jnp.dot(_two_bf16(z, 0), fold_ref[...], preferred_element_type=F32)
            rows = part[0:chunk * SLOT] + part[chunk * SLOT:]
            act = jnp.sum(rows.reshape(chunk, SLOT, PEER_KK), axis=1)
            coef_ref[pl.ds(base + r0, chunk), :] = gate_ref[pl.ds(base + r0, chunk), :] * _gelu(act)
            return carry

        lax.fori_loop(0, tb // chunk, finish, 0)

    _for_both_blocks(tab_hbm, tab_vmem, idx_hbm, idx_smem, sems, run_block)


def peer_coefficients(idx_t, tab, x_terms, gate, *, tb=256, chunk=32):
    m = gate.shape[0]
    n_j = 2 * STAGE_ROWS
    fold = (np.arange(n_j)[:, None] // SLOT == np.arange(PEER_KK)[None, :]).astype(np.float32)
    return pl.pallas_call(
        functools.partial(_peer_act_kernel, tb=tb, chunk=chunk),
        grid=(m // (2 * tb),),
        in_specs=[pl.BlockSpec(memory_space=pl.ANY), pl.BlockSpec(memory_space=pl.ANY),
                  pl.BlockSpec((2 * tb, 2 * SLOT, 128), lambda i: (i, 0, 0)),
                  pl.BlockSpec((2 * tb, PEER_KK), lambda i: (i, 0)),
                  pl.BlockSpec((n_j, PEER_KK), lambda i: (0, 0))],
        out_specs=pl.BlockSpec((2 * tb, PEER_KK), lambda i: (i, 0)),
        out_shape=jax.ShapeDtypeStruct((m, PEER_KK), F32),
        scratch_shapes=[pltpu.VMEM(tab.shape, jnp.uint32),
                        pltpu.VMEM((N_STAGE, STAGE_ROWS, 128), jnp.uint32),
                        pltpu.VMEM((tb, SLOT, n_j), F32),
                        pltpu.SemaphoreType.DMA((1 + 2 * PEER_KK,))] + [pltpu.SMEM((tb,), jnp.int32)] * (2 * PEER_KK),
        compiler_params=_cparams(("arbitrary",), VMEM_LIMIT),
        name="peer_coefficients",
    )(idx_t.reshape(PEER_KK, m // tb, tb), tab, x_terms, gate, jnp.asarray(fold, BF16))


def _peer_mix_kernel(idx_hbm, tab_hbm, coef_ref, spread_ref, y_ref, tab_vmem, stages, lhs, sems,
                     *idx_smem, tb, chunk):
    stages = [stages.at[n] for n in range(N_STAGE)]
    shape = (SLOT, 2 * STAGE_ROWS)
    sub = lax.broadcasted_iota(jnp.int32, shape, 0)
    col_slot = lax.broadcasted_iota(jnp.int32, shape, 1) & (SLOT - 1)
    low_slot = (sub & (ROW_SUB - 1)) * 2
    keep = (jnp.where(col_slot == low_slot, jnp.uint32(0x0000FFFF), jnp.uint32(0))
            | jnp.where(col_slot == low_slot + 1, jnp.uint32(0xFFFF0000), jnp.uint32(0)))
    first_term = sub < ROW_SUB

    def both_halves(a):
        bits = lax.bitcast_convert_type(a, jnp.uint32)
        return bits | (bits >> 16)

    def run_block(half, ids):
        base = half * tb

        def prepare(c, carry):
            r0 = pl.multiple_of(c * chunk, chunk)
            terms = _two_bf16(coef_ref[pl.ds(base + r0, chunk), :], 0)
            wide = jnp.dot(terms, spread_ref[...], preferred_element_type=F32)
            w_hi = both_halves(wide[0:chunk])
            w_lo = both_halves(wide[chunk:])
            for j in range(chunk):
                word = jnp.where(first_term, jnp.broadcast_to(w_hi[j:j + 1, :], shape),
                                 jnp.broadcast_to(w_lo[j:j + 1, :], shape))
                lhs[r0 + j] = word & keep
            return carry

        lax.fori_loop(0, tb // chunk, prepare, 0)

        def contract(t, rows):
            return jnp.dot(pltpu.bitcast(lhs[t], BF16), rows, preferred_element_type=F32)

        def store(t, pos, y):
            y_ref[base + t] = y[0:SLOT] + y[SLOT:]

        _for_each_token(tab_vmem, ids, stages, tb, contract, store)

    _for_both_blocks(tab_hbm, tab_vmem, idx_hbm, idx_smem, sems, run_block)


def peer_mix(idx_t, coef, tab, *, tb=256, chunk=32):
    m = coef.shape[0]
    n_j = 2 * STAGE_ROWS
    spread = (np.arange(PEER_KK)[:, None] == np.arange(n_j)[None, :] // SLOT).astype(np.float32)
    return pl.pallas_call(
        functools.partial(_peer_mix_kernel, tb=tb, chunk=chunk),
        grid=(m // (2 * tb),),
        in_specs=[pl.BlockSpec(memory_space=pl.ANY), pl.BlockSpec(memory_space=pl.ANY),
                  pl.BlockSpec((2 * tb, PEER_KK), lambda i: (i, 0)),
                  pl.BlockSpec((PEER_KK, n_j), lambda i: (0, 0))],
        out_specs=pl.BlockSpec((2 * tb, SLOT, 128), lambda i: (i, 0, 0)),
        out_shape=jax.ShapeDtypeStruct((m, SLOT, 128), F32),
        scratch_shapes=[pltpu.VMEM(tab.shape, jnp.uint32),
                        pltpu.VMEM((N_STAGE, STAGE_ROWS, 128), jnp.uint32),
                        pltpu.VMEM((tb, SLOT, n_j), jnp.uint32),
                        pltpu.SemaphoreType.DMA((1 + 2 * PEER_KK,))] + [pltpu.SMEM((tb,), jnp.int32)] * (2 * PEER_KK),
        compiler_params=_cparams(("arbitrary",), VMEM_LIMIT),
        name="peer_mix",
    )(idx_t.reshape(PEER_KK, m // tb, tb), tab, coef, jnp.asarray(spread, BF16))


def _ple_kernel(h_ref, y_ref, p_ref, g_ref, wg_ref, wp_ref, fg_ref, o_ref, *, final_norm):
    h = h_ref[...] + _from_slots(y_ref[...])
    gate = _sigmoid(jnp.dot(_rms(h, g_ref[...]).astype(BF16), wg_ref[...], preferred_element_type=F32))
    h = h + jnp.dot(p_ref[...].astype(BF16), wp_ref[...], preferred_element_type=F32) * gate
    if final_norm:
        h = _rms(h, fg_ref[...])
    o_ref[...] = h


def ple_residual(h, y, p, g, wg, wp, fg, *, final_norm, tm=512):
    m, d = h.shape
    row = lambda w: pl.BlockSpec((tm, w), lambda i: (i, 0))
    whole = lambda a: pl.BlockSpec(a.shape, lambda i: (0, 0))
    g2, fg2 = g.reshape(1, d), fg.reshape(1, d)
    return pl.pallas_call(
        functools.partial(_ple_kernel, final_norm=final_norm),
        grid=(m // tm,),
        in_specs=[row(d), row(d), row(p.shape[1]), whole(g2), whole(wg), whole(wp), whole(fg2)],
        out_specs=row(d),
        out_shape=jax.ShapeDtypeStruct((m, d), F32),
        compiler_params=_cparams(("parallel",), VMEM_LIMIT),
        name="ple_residual",
    )(h, y, p, g2, wg, wp, fg2)


def _mixer_conv_nsa(h, b, t, norm_g, w_in, conv_w, conv_b, ln_g, ln_b, cmp_pos, cmp_w1, cmp_w2, w_out):
    m = b * t
    n_q = NSA_KV_HEADS * NSA_HPG * HEAD_DIM
    kv_w = NSA_KV_HEADS * HEAD_DIM
    c0, c1, c2 = 2 * CONV_CH, 2 * CONV_CH + n_q, 2 * CONV_CH + n_q + 6 * kv_w
    wb = w_in.astype(BF16)
    vg, q, kv, gt = norm_matmul(h, norm_g, [wb[:, :c0], wb[:, c0:c1], wb[:, c1:c2], wb[:, c2:]])
    a_out = conformer_conv(vg.reshape(b, t, c0), conv_w, conv_b, ln_g, ln_b)

    kv = kv.reshape(b, t, 6, kv_w)
    kind = lambda c: kv[:, :, c]
    n_chunk = t // CMP_STRIDE
    cmp_src = jnp.stack([kind(0), kind(1)]).reshape(2, b, n_chunk, CMP_STRIDE * kv_w)
    cmp = compress_blocks(cmp_src, cmp_pos, cmp_w1, cmp_w2)

    lane_group = jnp.arange(kv_w) // HEAD_DIM

    def value_operand(v):
        per_group = []
        for g in range(NSA_KV_HEADS):
            other = jnp.where(jnp.arange(kv_w) == HEAD_DIM * (1 - g), 1.0, 0.0)
            per_group.append(jnp.where(lane_group == g, v, other).astype(BF16))
        return jnp.stack(per_group, axis=1)

    o = nsa_attention(q.reshape(b, t, n_q), cmp[0], cmp[1], kind(2).astype(BF16), value_operand(kind(3)),
                      kind(4).astype(BF16), value_operand(kind(5)), gt.reshape(b, t, -1))
    wo = w_out.astype(BF16)
    return mm2_residual(h, a_out.reshape(m, CONV_CH), o.reshape(m, n_q), wo[:CONV_CH], wo[CONV_CH:])


def _peer_ffn(h, norm_g, wq, subkeys, u_tab, v_tab):
    q, x_terms = norm_matmul(h, norm_g, [wq.astype(BF16)], emit_hn=True)
    zeros = jnp.zeros_like(subkeys[:, 0])
    keys_bd = jnp.concatenate([jnp.concatenate([subkeys[:, 0], zeros], axis=-1),
                               jnp.concatenate([zeros, subkeys[:, 1]], axis=-1)], axis=1)
    idx_t, gate_t = peer_route(q, keys_bd)
    m = h.shape[0]
    idx_t = idx_t.reshape(PEER_KK, m)
    gate = gate_t.reshape(PEER_KK, m).T
    coef = peer_coefficients(idx_t, pack_table(u_tab), x_terms.reshape(m, 2 * SLOT, 128), gate)
    return peer_mix(idx_t, coef, pack_table(v_tab)).reshape(m, SLOT * 128)


def kernel(x, p, mix_norm, ab_w_in, ab_conv_w, ab_conv_b, ab_conv_ln_g, ab_conv_ln_b, ab_cmp_pos, ab_cmp_w1,
           ab_cmp_w2, ab_w_out, pool_w, pool_scale, ffn_norm, peer_wq, peer_subkeys, peer_u, peer_v, ple_norm,
           ple_gate_w, ple_proj, final_norm):
    b, t, d = x.shape
    m = b * t
    depth = p.shape[0]
    h = x.reshape(m, d)
    for i in range(depth):
        j = i // 2
        if i % 2 == 0:
            h = _mixer_conv_nsa(h, b, t, mix_norm[i], ab_w_in[j], ab_conv_w[j], ab_conv_b[j], ab_conv_ln_g[j],
                                ab_conv_ln_b[j], ab_cmp_pos[j], ab_cmp_w1[j], ab_cmp_w2[j], ab_w_out[j])
        else:
            h = pool_mixer_residual(h.reshape(b, t, d), mix_norm[i], pool_w[j], pool_scale[j]).reshape(m, d)
        y = _peer_ffn(h, ffn_norm[i], peer_wq[i], peer_subkeys[i], peer_u[i], peer_v[i])
        h = ple_residual(h, y, p[i].reshape(m, -1), ple_norm[i], ple_gate_w[i].astype(BF16),
                         ple_proj[i].astype(BF16), final_norm, final_norm=(i == depth - 1))
    return h.reshape(b, t, d)
```

```python
import functools

import jax
import jax.numpy as jnp
import numpy as np
from jax import lax
from jax.experimental import pallas as pl
from jax.experimental.pallas import tpu as pltpu

F32 = jnp.float32
BF16 = jnp.bfloat16
HIGHEST = lax.Precision.HIGHEST

D_MODEL = 1024
NORM_EPS = 1e-6
NEG_INF = -1e30

CONV_CH = 512
CONV_WIDTH = 31
CONV_HALO = 32

HEAD_DIM = 64
NSA_KV_HEADS = 2
NSA_HPG = 4
CMP_LEN = 32
CMP_STRIDE = 16
CMP_HIDDEN = 256
SLC_BLOCK = 64
SLC_SHIFT = 6
SLC_TOP = 16
WINDOW = 512
FORCE_SCORE = 1e9

POOL_WINDOWS = (2, 4, 8, 16)
POOL_GROUP = 256
POOL_HALO = 16

PEER_HEADS = 8
N_KEYS = 128
PEER_TOPK = 16
PEER_KK = PEER_HEADS * PEER_TOPK
HALF = D_MODEL // 2
ROW_SUB = HALF // 128

VMEM_LIMIT = 56 * 1024 * 1024


def _cparams(sem, vmem=None):
    return pltpu.CompilerParams(dimension_semantics=sem, vmem_limit_bytes=vmem)


def _rms(x, g):
    return x * lax.rsqrt(jnp.mean(x * x, axis=-1, keepdims=True) + NORM_EPS) * g


def _gelu(x):
    return 0.5 * x * (1.0 + jnp.tanh(0.7978845608028654 * (x + 0.044715 * (x * x * x))))


def _sigmoid(x):
    return 1.0 / (1.0 + jnp.exp(-x))


def _norm_mm_kernel(x_ref, g_ref, *refs, n_w, emit_hn):
    w_refs = refs[:n_w]
    o_refs = refs[n_w:]
    y = _rms(x_ref[...], g_ref[...])
    yb = y.astype(BF16)
    for w_ref, o_ref in zip(w_refs, o_refs[:n_w]):
        o_ref[...] = jnp.dot(yb, w_ref[...], preferred_element_type=F32)
    if emit_hn:
        lo = (y - yb.astype(F32)).astype(BF16)
        blocks = [term[:, _slot_dims(c)] for term in (yb, lo) for c in range(SLOT)]
        o_refs[n_w][...] = jnp.concatenate(blocks, axis=-1)


def norm_matmul(x, g, ws, *, emit_hn=False, tm=512):
    m, d = x.shape
    n_w = len(ws)
    in_specs = [pl.BlockSpec((tm, d), lambda i: (i, 0)), pl.BlockSpec((1, d), lambda i: (0, 0))]
    in_specs += [pl.BlockSpec(w.shape, lambda i: (0, 0)) for w in ws]
    out_shape = [jax.ShapeDtypeStruct((m, w.shape[1]), F32) for w in ws]
    out_specs = [pl.BlockSpec((tm, w.shape[1]), lambda i: (i, 0)) for w in ws]
    if emit_hn:
        out_shape += [jax.ShapeDtypeStruct((m, 2 * d), BF16)]
        out_specs += [pl.BlockSpec((tm, 2 * d), lambda i: (i, 0))]
    return pl.pallas_call(
        functools.partial(_norm_mm_kernel, n_w=n_w, emit_hn=emit_hn),
        grid=(m // tm,),
        in_specs=in_specs, out_specs=out_specs, out_shape=out_shape,
        compiler_params=_cparams(("parallel",), VMEM_LIMIT),
        name="norm_matmul",
    )(x, g.reshape(1, d), *ws)


def _conv_kernel(cur_ref, halo_ref, w_ref, b_ref, g_ref, beta_ref, o_ref, buf, *, tt):
    t = pl.program_id(1)
    cur = cur_ref[0]
    halo = halo_ref[0]
    a_halo = halo[:, :CONV_CH] * _sigmoid(halo[:, CONV_CH:])
    buf[0:CONV_HALO, :] = jnp.where(t > 0, a_halo, 0.0)
    buf[CONV_HALO:, :] = cur[:, :CONV_CH] * _sigmoid(cur[:, CONV_CH:])
    acc = jnp.zeros((tt, CONV_CH), F32)
    first = CONV_HALO - (CONV_WIDTH - 1)
    for j in range(CONV_WIDTH):
        acc = acc + buf[pl.ds(first + j, tt), :] * w_ref[j:j + 1, :]
    y = acc + b_ref[...]
    mu = jnp.mean(y, axis=-1, keepdims=True)
    yc = y - mu
    var = jnp.mean(yc * yc, axis=-1, keepdims=True)
    y = yc * lax.rsqrt(var + NORM_EPS) * g_ref[...] + beta_ref[...]
    o_ref[0] = y * _sigmoid(y)


def conformer_conv(vg, conv_w, conv_b, ln_g, ln_b, *, tt=512):
    b, t, _ = vg.shape
    tt = min(tt, t)
    per = tt // CONV_HALO
    vec = lambda v: v.reshape(1, CONV_CH)
    vspec = pl.BlockSpec((1, CONV_CH), lambda bi, ti: (0, 0))
    return pl.pallas_call(
        functools.partial(_conv_kernel, tt=tt),
        grid=(b, t // tt),
        in_specs=[
            pl.BlockSpec((1, tt, 2 * CONV_CH), lambda bi, ti: (bi, ti, 0)),
            pl.BlockSpec((1, CONV_HALO, 2 * CONV_CH), lambda bi, ti: (bi, jnp.maximum(ti * per - 1, 0), 0)),
            pl.BlockSpec((CONV_WIDTH, CONV_CH), lambda bi, ti: (0, 0)),
            vspec, vspec, vspec,
        ],
        out_specs=pl.BlockSpec((1, tt, CONV_CH), lambda bi, ti: (bi, ti, 0)),
        out_shape=jax.ShapeDtypeStruct((b, t, CONV_CH), F32),
        scratch_shapes=[pltpu.VMEM((tt + CONV_HALO, CONV_CH), F32)],
        compiler_params=_cparams(("parallel", "parallel"), VMEM_LIMIT),
        name="conformer_conv",
    )(vg, vg, conv_w, vec(conv_b), vec(ln_g), vec(ln_b))


def _compress_kernel(c_ref, pos_ref, w1_ref, w2_ref, o_ref):
    c = c_ref[0, 0]
    n = c.shape[0]
    ca = c + pos_ref[0, 0]
    cb = c + pos_ref[0, 1]
    for g in range(NSA_KV_HEADS):
        ua = jnp.dot(ca, w1_ref[0, g, 0], preferred_element_type=F32, precision=HIGHEST)
        ub = jnp.dot(cb, w1_ref[0, g, 1], preferred_element_type=F32, precision=HIGHEST)
        hid = _gelu(ua + pltpu.roll(ub, n - 1, axis=0))
        o_ref[0, 0, g] = jnp.dot(hid, w2_ref[0], preferred_element_type=F32, precision=HIGHEST)


def compress_blocks(src, pos, w1, w2):
    _, b, n_chunk, width = src.shape
    groups = NSA_KV_HEADS
    posx = jnp.broadcast_to(pos.reshape(2, 2, CMP_STRIDE, 1, HEAD_DIM), (2, 2, CMP_STRIDE, groups, HEAD_DIM))
    posx = posx.reshape(2, 2, 1, width)
    w1r = w1.reshape(2, 1, 2, CMP_STRIDE, 1, HEAD_DIM, CMP_HIDDEN)
    own = (jnp.arange(groups)[:, None] == jnp.arange(groups)[None, :]).reshape(1, groups, 1, 1, groups, 1, 1)
    w1x = jnp.where(own, w1r, 0.0).reshape(2, groups, 2, width, CMP_HIDDEN)
    return pl.pallas_call(
        _compress_kernel,
        grid=(2, b),
        in_specs=[
            pl.BlockSpec((1, 1, n_chunk, width), lambda k, i: (k, i, 0, 0)),
            pl.BlockSpec((1, 2, 1, width), lambda k, i: (k, 0, 0, 0)),
            pl.BlockSpec((1, groups, 2, width, CMP_HIDDEN), lambda k, i: (k, 0, 0, 0, 0)),
            pl.BlockSpec((1, CMP_HIDDEN, HEAD_DIM), lambda k, i: (k, 0, 0)),
        ],
        out_specs=pl.BlockSpec((1, 1, groups, n_chunk, HEAD_DIM), lambda k, i: (k, i, 0, 0, 0)),
        out_shape=jax.ShapeDtypeStruct((2, b, groups, n_chunk, HEAD_DIM), F32),
        compiler_params=_cparams(("parallel", "parallel"), VMEM_LIMIT),
        name="nsa_compress",
    )(src, posx, w1x, w2)


def _sorting_network(n):
    pairs = []
    p = 1
    while p < n:
        k = p
        while k >= 1:
            for j in range(k % p, n - k, 2 * k):
                for i in range(min(k, n - j - k)):
                    if (i + j) // (2 * p) == (i + j + k) // (2 * p):
                        pairs.append((i + j, i + j + k))
            k //= 2
        p *= 2
    return pairs


def _topk_rows_by_columns(work, k, row=None):
    n = work.shape[0] // 8
    lanes = work.shape[1]
    vals = [work[8 * i:8 * (i + 1)] for i in range(n)]
    if row is None:
        sub = lax.broadcasted_iota(jnp.int32, (8, lanes), 0).astype(F32)
        ids = [sub + float(8 * i) for i in range(n)]
    else:
        ids = [jnp.broadcast_to(row[8 * i:8 * (i + 1)], (8, lanes)) for i in range(n)]
    for a, b in _sorting_network(pl.next_power_of_2(n)):
        if b >= n:
            continue
        first = (vals[a] > vals[b]) | ((vals[a] == vals[b]) & (ids[a] < ids[b]))
        vals[a], vals[b] = jnp.where(first, vals[a], vals[b]), jnp.where(first, vals[b], vals[a])
        ids[a], ids[b] = jnp.where(first, ids[a], ids[b]), jnp.where(first, ids[b], ids[a])
    out_v, out_i = [], []
    for r in range(k):
        m = jnp.max(vals[0], axis=0, keepdims=True)
        pick = jnp.min(jnp.where(vals[0] == m, ids[0], 3.0e38), axis=0, keepdims=True)
        out_v.append(m)
        out_i.append(pick)
        won = ids[0] == pick
        depth = k - 1 - r
        for i in range(min(depth, n - 1)):
            vals[i] = jnp.where(won, vals[i + 1], vals[i])
            ids[i] = jnp.where(won, ids[i + 1], ids[i])
        if 0 < n <= depth:
            vals[n - 1] = jnp.where(won, -jnp.inf, vals[n - 1])
    return out_v, out_i


def _masked_flash(qb, k_ref, v_ref, lo, hi, tk, mask_fn, first_group):
    rows = qb.shape[0]
    tq = rows // NSA_HPG

    def scores(j):
        kt = k_ref[0, pl.ds(pl.multiple_of(j * tk, tk), tk), :]
        s = lax.dot_general(qb, kt, (((1,), (1,)), ((), ())), preferred_element_type=F32)
        return (s.reshape(NSA_HPG, tq, tk) + mask_fn(j)[None]).reshape(rows, tk)

    def body(j, carry):
        m, acc, s = carry
        s_next = scores(jnp.minimum(j + 1, hi - 1))
        vt = v_ref[0, 0, pl.ds(pl.multiple_of(j * tk, tk), tk), :]
        m_new = jnp.maximum(m, jnp.max(s, axis=-1, keepdims=True))
        p = jnp.exp(s - m_new)
        acc = jnp.exp(m - m_new) * acc + jnp.dot(p.astype(BF16), vt, preferred_element_type=F32)
        return m_new, acc, s_next

    init = (jnp.full((rows, 1), NEG_INF, F32), jnp.zeros((rows, 2 * HEAD_DIM), F32), scores(lo))
    _, acc, _ = lax.fori_loop(lo, hi, body, init)
    left, right = acc[:, :HEAD_DIM], acc[:, HEAD_DIM:]
    return jnp.where(first_group, left / right[:, 0:1], right / left[:, 0:1])


def _nsa_kernel(q_ref, kc_ref, vc_ref, ks_ref, vs_ref, kw_ref, vw_ref, g_ref, ov_ref, o_ref, *, tq, tks, tkw):
    i = pl.program_id(2)
    first_group = pl.program_id(1) == 0
    t0 = i * tq
    rows = NSA_HPG * tq
    q_tile = q_ref[0]
    q = jnp.concatenate([q_tile[:, h * HEAD_DIM:(h + 1) * HEAD_DIM] for h in range(NSA_HPG)], axis=0)
    q = q * (HEAD_DIM ** -0.5)
    zero = jnp.zeros_like(q)
    qb = jnp.where(first_group, jnp.concatenate([q, zero], axis=-1),
                   jnp.concatenate([zero, q], axis=-1)).astype(BF16)
    t_q = t0 + lax.broadcasted_iota(jnp.int32, (tq, 1), 0)
    t_all = jnp.concatenate([t_q] * NSA_HPG, axis=0)

    kc = kc_ref[0, 0]
    n_cmp = kc.shape[0]
    s = lax.dot_general(q, kc, (((1,), (1,)), ((), ())), preferred_element_type=F32, precision=HIGHEST)
    cmp_end = lax.broadcasted_iota(jnp.int32, (1, n_cmp), 1) * CMP_STRIDE + (CMP_LEN - 1)
    ok = cmp_end <= t_all
    s = jnp.where(ok, s, NEG_INF)
    e = jnp.where(ok, jnp.exp(s - jnp.max(s, axis=-1, keepdims=True)), 0.0)
    den = jnp.sum(e, axis=-1, keepdims=True)
    p_cmp = e / jnp.where(den > 0.0, den, 1.0)
    o_cmp = jnp.dot(p_cmp.astype(BF16), vc_ref[0, 0].astype(BF16), preferred_element_type=F32)

    p_sum = p_cmp[0:tq]
    for h in range(1, NSA_HPG):
        p_sum = p_sum + p_cmp[h * tq:(h + 1) * tq]
    imp = jnp.dot(p_sum, ov_ref[...], preferred_element_type=F32, precision=HIGHEST)
    n_slc = ks_ref.shape[1] // SLC_BLOCK
    imp_t = imp.T[0:n_slc]
    blk = lax.broadcasted_iota(jnp.int32, (n_slc, tq), 0)
    t_lane = t0 + lax.broadcasted_iota(jnp.int32, (n_slc, tq), 1)
    cur = t_lane >> SLC_SHIFT
    forced = (blk == 0) | (blk == cur) | (blk == cur - 1)
    imp_t = jnp.where(forced, FORCE_SCORE, imp_t)
    imp_t = jnp.where(blk * SLC_BLOCK <= t_lane, imp_t, NEG_INF)
    _, picks = _topk_rows_by_columns(imp_t, min(SLC_TOP, n_slc))
    blk_f = blk.astype(F32)
    member = jnp.zeros((n_slc, tq), F32)
    for pk in picks:
        member = jnp.where(blk_f == pk, 1.0, member)
    if n_slc < 128:
        member = jnp.concatenate([member, jnp.zeros((128 - n_slc, tq), F32)], axis=0)
    member_q = member.T.astype(BF16)

    blocks_per_tile = tks // SLC_BLOCK

    def slc_mask(j):
        sel_row = lax.broadcasted_iota(jnp.int32, (128, tks), 0)
        key_blk = j * blocks_per_tile + (lax.broadcasted_iota(jnp.int32, (128, tks), 1) >> SLC_SHIFT)
        expand = jnp.where(sel_row == key_blk, 1.0, 0.0).astype(BF16)
        sel = jnp.dot(member_q, expand, preferred_element_type=F32) > 0.5
        kpos = j * tks + lax.broadcasted_iota(jnp.int32, (1, tks), 1)
        return jnp.where(sel & (kpos <= t_q), 0.0, NEG_INF)

    o_slc = _masked_flash(qb, ks_ref, vs_ref, 0, (t0 + tq + tks - 1) // tks, tks, slc_mask, first_group)

    def win_mask(j):
        kpos = j * tkw + lax.broadcasted_iota(jnp.int32, (1, tkw), 1)
        dist = t_q - kpos
        return jnp.where((dist >= 0) & (dist < WINDOW), 0.0, NEG_INF)

    lo = jnp.maximum(t0 - (WINDOW - 1), 0) // tkw
    o_win = _masked_flash(qb, kw_ref, vw_ref, lo, (t0 + tq + tkw - 1) // tkw, tkw, win_mask, first_group)

    gate = _sigmoid(g_ref[0])
    n_gate = 3 * NSA_HPG
    gate = jnp.where(first_group, gate[:, :n_gate], gate[:, n_gate:])
    heads = []
    for h in range(NSA_HPG):
        r = slice(h * tq, (h + 1) * tq)
        heads.append(gate[:, 3 * h:3 * h + 1] * o_cmp[r] + gate[:, 3 * h + 1:3 * h + 2] * o_slc[r]
                     + gate[:, 3 * h + 2:3 * h + 3] * o_win[r])
    o_ref[0] = jnp.concatenate(heads, axis=-1)


def nsa_attention(q, kc, vc, ks, vs, kw, vw, gates, *, tq=256, tks=256, tkw=128):
    b, t, _ = q.shape
    g = NSA_KV_HEADS
    n_chunk = kc.shape[2]
    n_slc = t // SLC_BLOCK
    n_cmp = (t - CMP_LEN) // CMP_STRIDE + 1
    tks = min(tks, t)
    cmp_start = np.arange(n_chunk) * CMP_STRIDE
    slc_start = np.arange(128) * SLC_BLOCK
    overlap = ((cmp_start[:, None] < slc_start[None, :] + SLC_BLOCK)
               & (cmp_start[:, None] + CMP_LEN > slc_start[None, :])
               & (np.arange(n_chunk)[:, None] < n_cmp) & (np.arange(128)[None, :] < n_slc))
    overlap = jnp.asarray(overlap.astype(np.float32))
    per_group = lambda a: pl.BlockSpec((1, 1) + a.shape[2:], lambda bi, gi, i: (bi, gi, 0, 0))
    shared = lambda a: pl.BlockSpec((1,) + a.shape[1:], lambda bi, gi, i: (bi, 0, 0))
    heads_w = NSA_HPG * HEAD_DIM
    return pl.pallas_call(
        functools.partial(_nsa_kernel, tq=tq, tks=tks, tkw=tkw),
        grid=(b, g, t // tq),
        in_specs=[
            pl.BlockSpec((1, tq, heads_w), lambda bi, gi, i: (bi, i, gi)),
            per_group(kc), per_group(vc), shared(ks), per_group(vs), shared(kw), per_group(vw),
            pl.BlockSpec((1, tq, g * 3 * NSA_HPG), lambda bi, gi, i: (bi, i, 0)),
            pl.BlockSpec((n_chunk, 128), lambda bi, gi, i: (0, 0)),
        ],
        out_specs=pl.BlockSpec((1, tq, heads_w), lambda bi, gi, i: (bi, i, gi)),
        out_shape=jax.ShapeDtypeStruct(q.shape, F32),
        compiler_params=_cparams(("parallel", "parallel", "arbitrary"), VMEM_LIMIT),
        name="nsa_attention",
    )(q, kc, vc, ks, vs, kw, vw, gates, overlap)


def _mm2_res_kernel(h_ref, a_ref, b_ref, wa_ref, wb_ref, o_ref):
    acc = jnp.dot(a_ref[...].astype(BF16), wa_ref[...], preferred_element_type=F32)
    acc = acc + jnp.dot(b_ref[...].astype(BF16), wb_ref[...], preferred_element_type=F32)
    o_ref[...] = h_ref[...] + acc


def mm2_residual(h, a, b, wa, wb, *, tm=512):
    m, d = h.shape
    row = lambda w: pl.BlockSpec((tm, w), lambda i: (i, 0))
    whole = lambda w: pl.BlockSpec(w.shape, lambda i: (0, 0))
    return pl.pallas_call(
        _mm2_res_kernel,
        grid=(m // tm,),
        in_specs=[row(d), row(a.shape[1]), row(b.shape[1]), whole(wa), whole(wb)],
        out_specs=row(d),
        out_shape=jax.ShapeDtypeStruct((m, d), F32),
        compiler_params=_cparams(("parallel",), VMEM_LIMIT),
        name="out_proj_residual",
    )(h, a, b, wa, wb)


def _pool_kernel(cur_ref, halo_ref, g_ref, w_ref, sc_ref, o_ref, buf, *, tt):
    t = pl.program_id(1)
    g = g_ref[...]
    cur = cur_ref[0]
    hn = _rms(cur, g)
    buf[0:POOL_HALO, :] = jnp.where(t > 0, _rms(halo_ref[0], g), 0.0)
    buf[POOL_HALO:, :] = hn
    pos = t * tt + lax.broadcasted_iota(jnp.int32, (tt, 1), 0)
    outs = []
    for gi, w in enumerate(POOL_WINDOWS):
        cols = slice(gi * POOL_GROUP, (gi + 1) * POOL_GROUP)
        tot = hn[:, cols]
        for j in range(1, w):
            tot = tot + buf[pl.ds(POOL_HALO - j, tt), cols]
        cnt = jnp.minimum(pos + 1, w).astype(F32)
        d = tot / cnt - hn[:, cols]
        outs.append(jnp.dot(d.astype(BF16), w_ref[gi], preferred_element_type=F32))
    y = jnp.concatenate(outs, axis=-1) * sc_ref[...]
    o_ref[0] = cur + y


def pool_mixer_residual(h, g, pool_w, pool_scale, *, tt=512):
    b, t, d = h.shape
    tt = min(tt, t)
    per = tt // POOL_HALO
    return pl.pallas_call(
        functools.partial(_pool_kernel, tt=tt),
        grid=(b, t // tt),
        in_specs=[
            pl.BlockSpec((1, tt, d), lambda bi, ti: (bi, ti, 0)),
            pl.BlockSpec((1, POOL_HALO, d), lambda bi, ti: (bi, jnp.maximum(ti * per - 1, 0), 0)),
            pl.BlockSpec((1, d), lambda bi, ti: (0, 0)),
            pl.BlockSpec(pool_w.shape, lambda bi, ti: (0, 0, 0)),
            pl.BlockSpec((1, d), lambda bi, ti: (0, 0)),
        ],
        out_specs=pl.BlockSpec((1, tt, d), lambda bi, ti: (bi, ti, 0)),
        out_shape=jax.ShapeDtypeStruct(h.shape, F32),
        scratch_shapes=[pltpu.VMEM((tt + POOL_HALO, d), F32)],
        compiler_params=_cparams(("parallel", "parallel"), VMEM_LIMIT),
        name="pool_mixer",
    )(h, h, g.reshape(1, d), pool_w.astype(BF16), pool_scale.reshape(1, d))


def _peer_route_kernel(q_ref, key_ref, idx_ref, gate_ref):
    q = q_ref[...]
    s = lax.dot_general(key_ref[0], q, (((1,), (1,)), ((), ())), preferred_element_type=F32,
                        precision=HIGHEST)
    v0, i0 = _topk_rows_by_columns(s[0:N_KEYS], PEER_TOPK)
    v1, i1 = _topk_rows_by_columns(s[N_KEYS:], PEER_TOPK)
    s0 = jnp.concatenate(v0, axis=0)
    s1 = jnp.concatenate(v1, axis=0)
    step = lax.broadcasted_iota(jnp.int32, (8, 1), 0)
    stepf = step.astype(F32)
    blocks, codes = [], []
    for a, b0 in ((0, 0), (0, 8), (1, 0), (2, 0), (3, 0)):
        blocks.append(v0[a] + s1[b0:b0 + 8])
        codes.append(stepf + float(a * PEER_TOPK + b0))
    for a0, b in ((8, 0), (0, 0), (0, 1), (0, 2)):
        dup = jnp.where((step < 4) & (a0 == 0), -jnp.inf, 0.0)
        blocks.append(s0[a0:a0 + 8] + v1[b] + dup)
        codes.append((stepf + float(a0)) * PEER_TOPK + float(b))
    cv, ci = _topk_rows_by_columns(jnp.concatenate(blocks, axis=0), PEER_TOPK, jnp.concatenate(codes, axis=0))
    c_top = jnp.concatenate(cv, axis=0)
    c_idx = jnp.concatenate(ci, axis=0).astype(jnp.int32)
    a_sel = c_idx >> 4
    b_sel = c_idx & (PEER_TOPK - 1)
    k1 = jnp.zeros_like(c_top)
    k2 = jnp.zeros_like(c_top)
    for r in range(PEER_TOPK):
        k1 = jnp.where(a_sel == r, i0[r], k1)
        k2 = jnp.where(b_sel == r, i1[r], k2)
    idx_ref[0] = ((k1 * N_KEYS + k2) * ROW_SUB).astype(jnp.int32)
    e = jnp.exp(c_top - jnp.max(c_top, axis=0, keepdims=True))
    gate_ref[0] = e / jnp.sum(e, axis=0, keepdims=True)


def peer_route(q, keys_bd, *, tb=512):
    m = q.shape[0]
    shp = (PEER_HEADS, PEER_TOPK, m)
    ospec = pl.BlockSpec((1, PEER_TOPK, tb), lambda i, h: (h, 0, i))
    return pl.pallas_call(
        _peer_route_kernel,
        grid=(m // tb, PEER_HEADS),
        in_specs=[pl.BlockSpec((tb, 2 * HEAD_DIM), lambda i, h: (i, h)),
                  pl.BlockSpec((1, 2 * N_KEYS, 2 * HEAD_DIM), lambda i, h: (h, 0, 0))],
        out_specs=[ospec, ospec],
        out_shape=[jax.ShapeDtypeStruct(shp, jnp.int32), jax.ShapeDtypeStruct(shp, F32)],
        compiler_params=_cparams(("parallel", "parallel"), VMEM_LIMIT),
        name="peer_route",
    )(q, keys_bd)


def pack_table(tab):
    bits = lax.bitcast_convert_type(tab.astype(BF16), jnp.uint16).astype(jnp.uint32)
    packed = (bits[:, :HALF] << 16) | bits[:, HALF:]
    return packed.reshape(tab.shape[0] * ROW_SUB, 128)


STAGE_ROWS = PEER_KK * ROW_SUB
ACT_STAGES = 8
MIX_STAGES = 4
SLOT = 2 * ROW_SUB


def _slot_dims(c):
    start = (c // 2) * 128 + (0 if c % 2 else HALF)
    return slice(start, start + 128)


def _from_slots(y):
    slot_of = {(_slot_dims(c).start // 128): c for c in range(SLOT)}
    return jnp.concatenate([y[:, slot_of[blk] * 128:(slot_of[blk] + 1) * 128] for blk in range(SLOT)], axis=-1)


def _two_bf16(a, axis):
    hi = a.astype(BF16)
    lo = (a - hi.astype(F32)).astype(BF16)
    return jnp.concatenate([hi, lo], axis=axis)


def _for_both_blocks(tab_hbm, tab_vmem, idx_hbm, idx_smem, sems, run_block):
    i = pl.program_id(0)
    sets = (idx_smem[:PEER_KK], idx_smem[PEER_KK:])

    def copies(block, s):
        return [pltpu.make_async_copy(idx_hbm.at[k, block], sets[s][k], sems.at[1 + s * PEER_KK + k])
                for k in range(PEER_KK)]

    @pl.when(i == 0)
    def _():
        cp = pltpu.make_async_copy(tab_hbm, tab_vmem, sems.at[0])
        cp.start()
        for c in copies(0, 0):
            c.start()
        cp.wait()

    for c in copies(2 * i + 1, 1):
        c.start()
    for c in copies(2 * i, 0):
        c.wait()
    run_block(0, sets[0])

    @pl.when(i + 1 < pl.num_programs(0))
    def _():
        for c in copies(2 * i + 2, 0):
            c.start()

    for c in copies(2 * i + 1, 1):
        c.wait()
    run_block(1, sets[1])


def _for_each_token(tab_vmem, idx_smem, stages, tb, contract, finish):
    def gather(stage, t):
        for k in range(PEER_KK):
            row = pl.multiple_of(idx_smem[k][t], ROW_SUB)
            stage[pl.ds(ROW_SUB * k, ROW_SUB), :] = tab_vmem[pl.ds(row, ROW_SUB), :]

    group = len(stages)
    for s in range(group):
        gather(stages[s], s)

    def trip(i, carry):
        t = group * i
        parts = [contract(t + s, pltpu.bitcast(stages[s][...], BF16)) for s in range(group)]
        for s in range(group):
            finish(t + s, s, parts[s])
        for s in range(group):
            gather(stages[s], jnp.minimum(t + group + s, tb - 1))
        return carry

    lax.fori_loop(0, tb // group, trip, 0)


def _slot_mask():
    j = lax.broadcasted_iota(jnp.int32, (SLOT, 2 * STAGE_ROWS), 1)
    r = lax.broadcasted_iota(jnp.int32, (SLOT, 2 * STAGE_ROWS), 0)
    return (j & (SLOT - 1)) == r


def _peer_act_kernel(idx_hbm, tab_hbm, x_ref, gate_ref, fold_ref, coef_ref, tab_vmem, stages, zbuf, sems,
                     *idx_smem, tb, chunk):
    stages = [stages.at[n] for n in range(stages.shape[0])]
    mask = _slot_mask()

    def run_block(half, ids):
        base = half * tb

        def contract(t, rows):
            return lax.dot_general(x_ref[base + t], rows, (((1,), (1,)), ((), ())), preferred_element_type=F32)

        def keep(t, pos, g):
            zbuf[t] = jnp.where(mask, g[0:SLOT] + g[SLOT:], 0.0)

        _for_each_token(tab_vmem, ids, stages, tb, contract, keep)

        def finish(c, carry):
            r0 = pl.multiple_of(c * chunk, chunk)
            z = zbuf[pl.ds(r0, chunk)].reshape(chunk * SLOT, 2 * STAGE_ROWS)
            part = jnp.dot(_two_bf16(z, 0), fold_ref[...], preferred_element_type=F32)
            rows = part[0:chunk * SLOT] + part[chunk * SLOT:]
            act = jnp.sum(rows.reshape(chunk, SLOT, PEER_KK), axis=1)
            coef_ref[pl.ds(base + r0, chunk), :] = gate_ref[pl.ds(base + r0, chunk), :] * _gelu(act)
            return carry

        lax.fori_loop(0, tb // chunk, finish, 0)

    _for_both_blocks(tab_hbm, tab_vmem, idx_hbm, idx_smem, sems, run_block)


def peer_coefficients(idx_t, tab, x_terms, gate, *, tb=256, chunk=32):
    m = gate.shape[0]
    n_j = 2 * STAGE_ROWS
    fold = (np.arange(n_j)[:, None] // SLOT == np.arange(PEER_KK)[None, :]).astype(np.float32)
    return pl.pallas_call(
        functools.partial(_peer_act_kernel, tb=tb, chunk=chunk),
        grid=(m // (2 * tb),),
        in_specs=[pl.BlockSpec(memory_space=pl.ANY), pl.BlockSpec(memory_space=pl.ANY),
                  pl.BlockSpec((2 * tb, 2 * SLOT, 128), lambda i: (i, 0, 0)),
                  pl.BlockSpec((2 * tb, PEER_KK), lambda i: (i, 0)),
                  pl.BlockSpec((n_j, PEER_KK), lambda i: (0, 0))],
        out_specs=pl.BlockSpec((2 * tb, PEER_KK), lambda i: (i, 0)),
        out_shape=jax.ShapeDtypeStruct((m, PEER_KK), F32),
        scratch_shapes=[pltpu.VMEM(tab.shape, jnp.uint32),
                        pltpu.VMEM((ACT_STAGES, STAGE_ROWS, 128), jnp.uint32),
                        pltpu.VMEM((tb, SLOT, n_j), F32),
                        pltpu.SemaphoreType.DMA((1 + 2 * PEER_KK,))] + [pltpu.SMEM((tb,), jnp.int32)] * (2 * PEER_KK),
        compiler_params=_cparams(("arbitrary",), VMEM_LIMIT),
        name="peer_coefficients",
    )(idx_t.reshape(PEER_KK, m // tb, tb), tab, x_terms, gate, jnp.asarray(fold, BF16))


def _peer_mix_kernel(idx_hbm, tab_hbm, coef_ref, spread_ref, y_ref, tab_vmem, stages, lhs, sems,
                     *idx_smem, tb, chunk):
    stages = [stages.at[n] for n in range(stages.shape[0])]
    shape = (SLOT, 2 * STAGE_ROWS)
    sub = lax.broadcasted_iota(jnp.int32, shape, 0)
    col_slot = lax.broadcasted_iota(jnp.int32, shape, 1) & (SLOT - 1)
    low_slot = (sub & (ROW_SUB - 1)) * 2
    keep = (jnp.where(col_slot == low_slot, jnp.uint32(0x0000FFFF), jnp.uint32(0))
            | jnp.where(col_slot == low_slot + 1, jnp.uint32(0xFFFF0000), jnp.uint32(0)))
    first_term = sub < ROW_SUB

    def both_halves(a):
        bits = lax.bitcast_convert_type(a, jnp.uint32)
        return bits | (bits >> 16)

    def run_block(half, ids):
        base = half * tb

        def prepare(c, carry):
            r0 = pl.multiple_of(c * chunk, chunk)
            terms = _two_bf16(coef_ref[pl.ds(base + r0, chunk), :], 0)
            wide = jnp.dot(terms, spread_ref[...], preferred_element_type=F32)
            w_hi = both_halves(wide[0:chunk])
            w_lo = both_halves(wide[chunk:])
            for j in range(chunk):
                word = jnp.where(first_term, jnp.broadcast_to(w_hi[j:j + 1, :], shape),
                                 jnp.broadcast_to(w_lo[j:j + 1, :], shape))
                lhs[r0 + j] = word & keep
            return carry

        lax.fori_loop(0, tb // chunk, prepare, 0)

        def contract(t, rows):
            return jnp.dot(pltpu.bitcast(lhs[t], BF16), rows, preferred_element_type=F32)

        def store(t, pos, y):
            y_ref[base + t] = y[0:SLOT] + y[SLOT:]

        _for_each_token(tab_vmem, ids, stages, tb, contract, store)

    _for_both_blocks(tab_hbm, tab_vmem, idx_hbm, idx_smem, sems, run_block)


def peer_mix(idx_t, coef, tab, *, tb=256, chunk=32):
    m = coef.shape[0]
    n_j = 2 * STAGE_ROWS
    spread = (np.arange(PEER_KK)[:, None] == np.arange(n_j)[None, :] // SLOT).astype(np.float32)
    return pl.pallas_call(
        functools.partial(_peer_mix_kernel, tb=tb, chunk=chunk),
        grid=(m // (2 * tb),),
        in_specs=[pl.BlockSpec(memory_space=pl.ANY), pl.BlockSpec(memory_space=pl.ANY),
                  pl.BlockSpec((2 * tb, PEER_KK), lambda i: (i, 0)),
                  pl.BlockSpec((PEER_KK, n_j), lambda i: (0, 0))],
        out_specs=pl.BlockSpec((2 * tb, SLOT, 128), lambda i: (i, 0, 0)),
        out_shape=jax.ShapeDtypeStruct((m, SLOT, 128), F32),
        scratch_shapes=[pltpu.VMEM(tab.shape, jnp.uint32),
                        pltpu.VMEM((MIX_STAGES, STAGE_ROWS, 128), jnp.uint32),
                        pltpu.VMEM((tb, SLOT, n_j), jnp.uint32),
                        pltpu.SemaphoreType.DMA((1 + 2 * PEER_KK,))] + [pltpu.SMEM((tb,), jnp.int32)] * (2 * PEER_KK),
        compiler_params=_cparams(("arbitrary",), VMEM_LIMIT),
        name="peer_mix",
    )(idx_t.reshape(PEER_KK, m // tb, tb), tab, coef, jnp.asarray(spread, BF16))


def _ple_kernel(h_ref, y_ref, p_ref, g_ref, wg_ref, wp_ref, fg_ref, o_ref, *, final_norm):
    h = h_ref[...] + _from_slots(y_ref[...])
    gate = _sigmoid(jnp.dot(_rms(h, g_ref[...]).astype(BF16), wg_ref[...], preferred_element_type=F32))
    h = h + jnp.dot(p_ref[...].astype(BF16), wp_ref[...], preferred_element_type=F32) * gate
    if final_norm:
        h = _rms(h, fg_ref[...])
    o_ref[...] = h


def ple_residual(h, y, p, g, wg, wp, fg, *, final_norm, tm=512):
    m, d = h.shape
    row = lambda w: pl.BlockSpec((tm, w), lambda i: (i, 0))
    whole = lambda a: pl.BlockSpec(a.shape, lambda i: (0, 0))
    g2, fg2 = g.reshape(1, d), fg.reshape(1, d)
    return pl.pallas_call(
        functools.partial(_ple_kernel, final_norm=final_norm),
        grid=(m // tm,),
        in_specs=[row(d), row(d), row(p.shape[1]), whole(g2), whole(wg), whole(wp), whole(fg2)],
        out_specs=row(d),
        out_shape=jax.ShapeDtypeStruct((m, d), F32),
        compiler_params=_cparams(("parallel",), VMEM_LIMIT),
        name="ple_residual",
    )(h, y, p, g2, wg, wp, fg2)


def _mixer_conv_nsa(h, b, t, norm_g, w_in, conv_w, conv_b, ln_g, ln_b, cmp_pos, cmp_w1, cmp_w2, w_out):
    m = b * t
    n_q = NSA_KV_HEADS * NSA_HPG * HEAD_DIM
    kv_w = NSA_KV_HEADS * HEAD_DIM
    c0, c1, c2 = 2 * CONV_CH, 2 * CONV_CH + n_q, 2 * CONV_CH + n_q + 6 * kv_w
    wb = w_in.astype(BF16)
    vg, q, kv, gt = norm_matmul(h, norm_g, [wb[:, :c0], wb[:, c0:c1], wb[:, c1:c2], wb[:, c2:]])
    a_out = conformer_conv(vg.reshape(b, t, c0), conv_w, conv_b, ln_g, ln_b)

    kv = kv.reshape(b, t, 6, kv_w)
    kind = lambda c: kv[:, :, c]
    n_chunk = t // CMP_STRIDE
    cmp_src = jnp.stack([kind(0), kind(1)]).reshape(2, b, n_chunk, CMP_STRIDE * kv_w)
    cmp = compress_blocks(cmp_src, cmp_pos, cmp_w1, cmp_w2)

    lane_group = jnp.arange(kv_w) // HEAD_DIM

    def value_operand(v):
        per_group = []
        for g in range(NSA_KV_HEADS):
            other = jnp.where(jnp.arange(kv_w) == HEAD_DIM * (1 - g), 1.0, 0.0)
            per_group.append(jnp.where(lane_group == g, v, other).astype(BF16))
        return jnp.stack(per_group, axis=1)

    o = nsa_attention(q.reshape(b, t, n_q), cmp[0], cmp[1], kind(2).astype(BF16), value_operand(kind(3)),
                      kind(4).astype(BF16), value_operand(kind(5)), gt.reshape(b, t, -1))
    wo = w_out.astype(BF16)
    return mm2_residual(h, a_out.reshape(m, CONV_CH), o.reshape(m, n_q), wo[:CONV_CH], wo[CONV_CH:])


def _peer_ffn(h, norm_g, wq, subkeys, u_tab, v_tab):
    q, x_terms = norm_matmul(h, norm_g, [wq.astype(BF16)], emit_hn=True)
    zeros = jnp.zeros_like(subkeys[:, 0])
    keys_bd = jnp.concatenate([jnp.concatenate([subkeys[:, 0], zeros], axis=-1),
                               jnp.concatenate([zeros, subkeys[:, 1]], axis=-1)], axis=1)
    idx_t, gate_t = peer_route(q, keys_bd)
    m = h.shape[0]
    idx_t = idx_t.reshape(PEER_KK, m)
    gate = gate_t.reshape(PEER_KK, m).T
    coef = peer_coefficients(idx_t, pack_table(u_tab), x_terms.reshape(m, 2 * SLOT, 128), gate)
    return peer_mix(idx_t, coef, pack_table(v_tab)).reshape(m, SLOT * 128)


def kernel(x, p, mix_norm, ab_w_in, ab_conv_w, ab_conv_b, ab_conv_ln_g, ab_conv_ln_b, ab_cmp_pos, ab_cmp_w1,
           ab_cmp_w2, ab_w_out, pool_w, pool_scale, ffn_norm, peer_wq, peer_subkeys, peer_u, peer_v, ple_norm,
           ple_gate_w, ple_proj, final_norm):
    b, t, d = x.shape
    m = b * t
    depth = p.shape[0]
    h = x.reshape(m, d)
    for i in range(depth):
        j = i // 2
        if i % 2 == 0:
            h = _mixer_conv_nsa(h, b, t, mix_norm[i], ab_w_in[j], ab_conv_w[j], ab_conv_b[j], ab_conv_ln_g[j],
                                ab_conv_ln_b[j], ab_cmp_pos[j], ab_cmp_w1[j], ab_cmp_w2[j], ab_w_out[j])
        else:
            h = pool_mixer_residual(h.reshape(b, t, d), mix_norm[i], pool_w[j], pool_scale[j]).reshape(m, d)
        y = _peer_ffn(h, ffn_norm[i], peer_wq[i], peer_subkeys[i], peer_u[i], peer_v[i])
        h = ple_residual(h, y, p[i].reshape(m, -1), ple_norm[i], ple_gate_w[i].astype(BF16),
                         ple_proj[i].astype(BF16), final_norm, final_norm=(i == depth - 1))
    return h.reshape(b, t, d)
```

```python
import functools

import jax
import jax.numpy as jnp
import numpy as np
from jax import lax
from jax.experimental import pallas as pl
from jax.experimental.pallas import tpu as pltpu

F32 = jnp.float32
BF16 = jnp.bfloat16
HIGHEST = lax.Precision.HIGHEST

D_MODEL = 1024
NORM_EPS = 1e-6
NEG_INF = -1e30

CONV_CH = 512
CONV_WIDTH = 31
CONV_HALO = 32

HEAD_DIM = 64
NSA_KV_HEADS = 2
NSA_HPG = 4
CMP_LEN = 32
CMP_STRIDE = 16
CMP_HIDDEN = 256
SLC_BLOCK = 64
SLC_SHIFT = 6
SLC_TOP = 16
WINDOW = 512
FORCE_SCORE = 1e9

POOL_WINDOWS = (2, 4, 8, 16)
POOL_GROUP = 256
POOL_HALO = 16

PEER_HEADS = 8
N_KEYS = 128
PEER_TOPK = 16
PEER_KK = PEER_HEADS * PEER_TOPK
HALF = D_MODEL // 2
ROW_SUB = HALF // 128

VMEM_LIMIT = 56 * 1024 * 1024


def _cparams(sem, vmem=None):
    return pltpu.CompilerParams(dimension_semantics=sem, vmem_limit_bytes=vmem)


def _rms(x, g):
    return x * lax.rsqrt(jnp.mean(x * x, axis=-1, keepdims=True) + NORM_EPS) * g


def _gelu(x):
    return 0.5 * x * (1.0 + jnp.tanh(0.7978845608028654 * (x + 0.044715 * (x * x * x))))


def _sigmoid(x):
    return 1.0 / (1.0 + jnp.exp(-x))


def _norm_mm_kernel(x_ref, g_ref, *refs, n_w, emit_hn):
    w_refs = refs[:n_w]
    o_refs = refs[n_w:]
    y = _rms(x_ref[...], g_ref[...])
    yb = y.astype(BF16)
    for w_ref, o_ref in zip(w_refs, o_refs[:n_w]):
        o_ref[...] = jnp.dot(yb, w_ref[...], preferred_element_type=F32)
    if emit_hn:
        lo = (y - yb.astype(F32)).astype(BF16)
        blocks = [term[:, _slot_dims(c)] for term in (yb, lo) for c in range(SLOT)]
        o_refs[n_w][...] = jnp.concatenate(blocks, axis=-1)


def norm_matmul(x, g, ws, *, emit_hn=False, tm=512):
    m, d = x.shape
    n_w = len(ws)
    in_specs = [pl.BlockSpec((tm, d), lambda i: (i, 0)), pl.BlockSpec((1, d), lambda i: (0, 0))]
    in_specs += [pl.BlockSpec(w.shape, lambda i: (0, 0)) for w in ws]
    out_shape = [jax.ShapeDtypeStruct((m, w.shape[1]), F32) for w in ws]
    out_specs = [pl.BlockSpec((tm, w.shape[1]), lambda i: (i, 0)) for w in ws]
    if emit_hn:
        out_shape += [jax.ShapeDtypeStruct((m, 2 * d), BF16)]
        out_specs += [pl.BlockSpec((tm, 2 * d), lambda i: (i, 0))]
    return pl.pallas_call(
        functools.partial(_norm_mm_kernel, n_w=n_w, emit_hn=emit_hn),
        grid=(m // tm,),
        in_specs=in_specs, out_specs=out_specs, out_shape=out_shape,
        compiler_params=_cparams(("parallel",), VMEM_LIMIT),
        name="norm_matmul",
    )(x, g.reshape(1, d), *ws)


def _conv_kernel(cur_ref, halo_ref, w_ref, b_ref, g_ref, beta_ref, o_ref, buf, *, tt):
    t = pl.program_id(1)
    cur = cur_ref[0]
    halo = halo_ref[0]
    a_halo = halo[:, :CONV_CH] * _sigmoid(halo[:, CONV_CH:])
    buf[0:CONV_HALO, :] = jnp.where(t > 0, a_halo, 0.0)
    buf[CONV_HALO:, :] = cur[:, :CONV_CH] * _sigmoid(cur[:, CONV_CH:])
    acc = jnp.zeros((tt, CONV_CH), F32)
    first = CONV_HALO - (CONV_WIDTH - 1)
    for j in range(CONV_WIDTH):
        acc = acc + buf[pl.ds(first + j, tt), :] * w_ref[j:j + 1, :]
    y = acc + b_ref[...]
    mu = jnp.mean(y, axis=-1, keepdims=True)
    yc = y - mu
    var = jnp.mean(yc * yc, axis=-1, keepdims=True)
    y = yc * lax.rsqrt(var + NORM_EPS) * g_ref[...] + beta_ref[...]
    o_ref[0] = y * _sigmoid(y)


def conformer_conv(vg, conv_w, conv_b, ln_g, ln_b, *, tt=512):
    b, t, _ = vg.shape
    tt = min(tt, t)
    per = tt // CONV_HALO
    vec = lambda v: v.reshape(1, CONV_CH)
    vspec = pl.BlockSpec((1, CONV_CH), lambda bi, ti: (0, 0))
    return pl.pallas_call(
        functools.partial(_conv_kernel, tt=tt),
        grid=(b, t // tt),
        in_specs=[
            pl.BlockSpec((1, tt, 2 * CONV_CH), lambda bi, ti: (bi, ti, 0)),
            pl.BlockSpec((1, CONV_HALO, 2 * CONV_CH), lambda bi, ti: (bi, jnp.maximum(ti * per - 1, 0), 0)),
            pl.BlockSpec((CONV_WIDTH, CONV_CH), lambda bi, ti: (0, 0)),
            vspec, vspec, vspec,
        ],
        out_specs=pl.BlockSpec((1, tt, CONV_CH), lambda bi, ti: (bi, ti, 0)),
        out_shape=jax.ShapeDtypeStruct((b, t, CONV_CH), F32),
        scratch_shapes=[pltpu.VMEM((tt + CONV_HALO, CONV_CH), F32)],
        compiler_params=_cparams(("parallel", "parallel"), VMEM_LIMIT),
        name="conformer_conv",
    )(vg, vg, conv_w, vec(conv_b), vec(ln_g), vec(ln_b))


def _compress_kernel(c_ref, pos_ref, w1_ref, w2_ref, o_ref):
    c = c_ref[0, 0]
    n = c.shape[0]
    ca = c + pos_ref[0, 0]
    cb = c + pos_ref[0, 1]
    for g in range(NSA_KV_HEADS):
        ua = jnp.dot(ca, w1_ref[0, g, 0], preferred_element_type=F32, precision=HIGHEST)
        ub = jnp.dot(cb, w1_ref[0, g, 1], preferred_element_type=F32, precision=HIGHEST)
        hid = _gelu(ua + pltpu.roll(ub, n - 1, axis=0))
        o_ref[0, 0, g] = jnp.dot(hid, w2_ref[0], preferred_element_type=F32, precision=HIGHEST)


def compress_blocks(src, pos, w1, w2):
    _, b, n_chunk, width = src.shape
    groups = NSA_KV_HEADS
    posx = jnp.broadcast_to(pos.reshape(2, 2, CMP_STRIDE, 1, HEAD_DIM), (2, 2, CMP_STRIDE, groups, HEAD_DIM))
    posx = posx.reshape(2, 2, 1, width)
    w1r = w1.reshape(2, 1, 2, CMP_STRIDE, 1, HEAD_DIM, CMP_HIDDEN)
    own = (jnp.arange(groups)[:, None] == jnp.arange(groups)[None, :]).reshape(1, groups, 1, 1, groups, 1, 1)
    w1x = jnp.where(own, w1r, 0.0).reshape(2, groups, 2, width, CMP_HIDDEN)
    return pl.pallas_call(
        _compress_kernel,
        grid=(2, b),
        in_specs=[
            pl.BlockSpec((1, 1, n_chunk, width), lambda k, i: (k, i, 0, 0)),
            pl.BlockSpec((1, 2, 1, width), lambda k, i: (k, 0, 0, 0)),
            pl.BlockSpec((1, groups, 2, width, CMP_HIDDEN), lambda k, i: (k, 0, 0, 0, 0)),
            pl.BlockSpec((1, CMP_HIDDEN, HEAD_DIM), lambda k, i: (k, 0, 0)),
        ],
        out_specs=pl.BlockSpec((1, 1, groups, n_chunk, HEAD_DIM), lambda k, i: (k, i, 0, 0, 0)),
        out_shape=jax.ShapeDtypeStruct((2, b, groups, n_chunk, HEAD_DIM), F32),
        compiler_params=_cparams(("parallel", "parallel"), VMEM_LIMIT),
        name="nsa_compress",
    )(src, posx, w1x, w2)


def _sorting_network(n):
    pairs = []
    p = 1
    while p < n:
        k = p
        while k >= 1:
            for j in range(k % p, n - k, 2 * k):
                for i in range(min(k, n - j - k)):
                    if (i + j) // (2 * p) == (i + j + k) // (2 * p):
                        pairs.append((i + j, i + j + k))
            k //= 2
        p *= 2
    return pairs


def _topk_rows_by_columns(work, k, row=None):
    n = work.shape[0] // 8
    lanes = work.shape[1]
    vals = [work[8 * i:8 * (i + 1)] for i in range(n)]
    if row is None:
        sub = lax.broadcasted_iota(jnp.int32, (8, lanes), 0).astype(F32)
        ids = [sub + float(8 * i) for i in range(n)]
    else:
        ids = [jnp.broadcast_to(row[8 * i:8 * (i + 1)], (8, lanes)) for i in range(n)]
    for a, b in _sorting_network(pl.next_power_of_2(n)):
        if b >= n:
            continue
        first = (vals[a] > vals[b]) | ((vals[a] == vals[b]) & (ids[a] < ids[b]))
        vals[a], vals[b] = jnp.where(first, vals[a], vals[b]), jnp.where(first, vals[b], vals[a])
        ids[a], ids[b] = jnp.where(first, ids[a], ids[b]), jnp.where(first, ids[b], ids[a])
    out_v, out_i = [], []
    for r in range(k):
        m = jnp.max(vals[0], axis=0, keepdims=True)
        pick = jnp.min(jnp.where(vals[0] == m, ids[0], 3.0e38), axis=0, keepdims=True)
        out_v.append(m)
        out_i.append(pick)
        won = ids[0] == pick
        depth = k - 1 - r
        for i in range(min(depth, n - 1)):
            vals[i] = jnp.where(won, vals[i + 1], vals[i])
            ids[i] = jnp.where(won, ids[i + 1], ids[i])
        if 0 < n <= depth:
            vals[n - 1] = jnp.where(won, -jnp.inf, vals[n - 1])
    return out_v, out_i


def _masked_flash(qb, k_ref, v_ref, lo, hi, tk, mask_fn, first_group):
    rows = qb.shape[0]
    tq = rows // NSA_HPG

    def scores(j):
        kt = k_ref[0, pl.ds(pl.multiple_of(j * tk, tk), tk), :]
        s = lax.dot_general(qb, kt, (((1,), (1,)), ((), ())), preferred_element_type=F32)
        return (s.reshape(NSA_HPG, tq, tk) + mask_fn(j)[None]).reshape(rows, tk)

    def body(j, carry):
        m, acc, s = carry
        s_next = scores(jnp.minimum(j + 1, hi - 1))
        vt = v_ref[0, 0, pl.ds(pl.multiple_of(j * tk, tk), tk), :]
        m_new = jnp.maximum(m, jnp.max(s, axis=-1, keepdims=True))
        p = jnp.exp(s - m_new)
        acc = jnp.exp(m - m_new) * acc + jnp.dot(p.astype(BF16), vt, preferred_element_type=F32)
        return m_new, acc, s_next

    init = (jnp.full((rows, 1), NEG_INF, F32), jnp.zeros((rows, 2 * HEAD_DIM), F32), scores(lo))
    _, acc, _ = lax.fori_loop(lo, hi, body, init)
    left, right = acc[:, :HEAD_DIM], acc[:, HEAD_DIM:]
    return jnp.where(first_group, left / right[:, 0:1], right / left[:, 0:1])


def _nsa_kernel(q_ref, kc_ref, vc_ref, ks_ref, vs_ref, kw_ref, vw_ref, g_ref, ov_ref, o_ref, *, tq, tks, tkw):
    i = pl.program_id(2)
    first_group = pl.program_id(1) == 0
    t0 = i * tq
    rows = NSA_HPG * tq
    q_tile = q_ref[0]
    q = jnp.concatenate([q_tile[:, h * HEAD_DIM:(h + 1) * HEAD_DIM] for h in range(NSA_HPG)], axis=0)
    q = q * (HEAD_DIM ** -0.5)
    zero = jnp.zeros_like(q)
    qb = jnp.where(first_group, jnp.concatenate([q, zero], axis=-1),
                   jnp.concatenate([zero, q], axis=-1)).astype(BF16)
    t_q = t0 + lax.broadcasted_iota(jnp.int32, (tq, 1), 0)
    t_all = jnp.concatenate([t_q] * NSA_HPG, axis=0)

    kc = kc_ref[0, 0]
    n_cmp = kc.shape[0]
    s = lax.dot_general(q, kc, (((1,), (1,)), ((), ())), preferred_element_type=F32, precision=HIGHEST)
    cmp_end = lax.broadcasted_iota(jnp.int32, (1, n_cmp), 1) * CMP_STRIDE + (CMP_LEN - 1)
    ok = cmp_end <= t_all
    s = jnp.where(ok, s, NEG_INF)
    e = jnp.where(ok, jnp.exp(s - jnp.max(s, axis=-1, keepdims=True)), 0.0)
    den = jnp.sum(e, axis=-1, keepdims=True)
    p_cmp = e / jnp.where(den > 0.0, den, 1.0)
    o_cmp = jnp.dot(p_cmp.astype(BF16), vc_ref[0, 0].astype(BF16), preferred_element_type=F32)

    p_sum = p_cmp[0:tq]
    for h in range(1, NSA_HPG):
        p_sum = p_sum + p_cmp[h * tq:(h + 1) * tq]
    imp = jnp.dot(p_sum, ov_ref[...], preferred_element_type=F32, precision=HIGHEST)
    n_slc = ks_ref.shape[1] // SLC_BLOCK
    imp_t = imp.T[0:n_slc]
    blk = lax.broadcasted_iota(jnp.int32, (n_slc, tq), 0)
    t_lane = t0 + lax.broadcasted_iota(jnp.int32, (n_slc, tq), 1)
    cur = t_lane >> SLC_SHIFT
    forced = (blk == 0) | (blk == cur) | (blk == cur - 1)
    imp_t = jnp.where(forced, FORCE_SCORE, imp_t)
    imp_t = jnp.where(blk * SLC_BLOCK <= t_lane, imp_t, NEG_INF)
    _, picks = _topk_rows_by_columns(imp_t, min(SLC_TOP, n_slc))
    blk_f = blk.astype(F32)
    member = jnp.zeros((n_slc, tq), F32)
    for pk in picks:
        member = jnp.where(blk_f == pk, 1.0, member)
    if n_slc < 128:
        member = jnp.concatenate([member, jnp.zeros((128 - n_slc, tq), F32)], axis=0)
    member_q = member.T.astype(BF16)

    blocks_per_tile = tks // SLC_BLOCK

    def slc_mask(j):
        sel_row = lax.broadcasted_iota(jnp.int32, (128, tks), 0)
        key_blk = j * blocks_per_tile + (lax.broadcasted_iota(jnp.int32, (128, tks), 1) >> SLC_SHIFT)
        expand = jnp.where(sel_row == key_blk, 1.0, 0.0).astype(BF16)
        sel = jnp.dot(member_q, expand, preferred_element_type=F32) > 0.5
        kpos = j * tks + lax.broadcasted_iota(jnp.int32, (1, tks), 1)
        return jnp.where(sel & (kpos <= t_q), 0.0, NEG_INF)

    o_slc = _masked_flash(qb, ks_ref, vs_ref, 0, (t0 + tq + tks - 1) // tks, tks, slc_mask, first_group)

    def win_mask(j):
        kpos = j * tkw + lax.broadcasted_iota(jnp.int32, (1, tkw), 1)
        dist = t_q - kpos
        return jnp.where((dist >= 0) & (dist < WINDOW), 0.0, NEG_INF)

    lo = jnp.maximum(t0 - (WINDOW - 1), 0) // tkw
    o_win = _masked_flash(qb, kw_ref, vw_ref, lo, (t0 + tq + tkw - 1) // tkw, tkw, win_mask, first_group)

    gate = _sigmoid(g_ref[0])
    n_gate = 3 * NSA_HPG
    gate = jnp.where(first_group, gate[:, :n_gate], gate[:, n_gate:])
    heads = []
    for h in range(NSA_HPG):
        r = slice(h * tq, (h + 1) * tq)
        heads.append(gate[:, 3 * h:3 * h + 1] * o_cmp[r] + gate[:, 3 * h + 1:3 * h + 2] * o_slc[r]
                     + gate[:, 3 * h + 2:3 * h + 3] * o_win[r])
    o_ref[0] = jnp.concatenate(heads, axis=-1)


def nsa_attention(q, kc, vc, ks, vs, kw, vw, gates, *, tq=256, tks=256, tkw=128):
    b, t, _ = q.shape
    g = NSA_KV_HEADS
    n_chunk = kc.shape[2]
    n_slc = t // SLC_BLOCK
    n_cmp = (t - CMP_LEN) // CMP_STRIDE + 1
    tks = min(tks, t)
    cmp_start = np.arange(n_chunk) * CMP_STRIDE
    slc_start = np.arange(128) * SLC_BLOCK
    overlap = ((cmp_start[:, None] < slc_start[None, :] + SLC_BLOCK)
               & (cmp_start[:, None] + CMP_LEN > slc_start[None, :])
               & (np.arange(n_chunk)[:, None] < n_cmp) & (np.arange(128)[None, :] < n_slc))
    overlap = jnp.asarray(overlap.astype(np.float32))
    per_group = lambda a: pl.BlockSpec((1, 1) + a.shape[2:], lambda bi, gi, i: (bi, gi, 0, 0))
    shared = lambda a: pl.BlockSpec((1,) + a.shape[1:], lambda bi, gi, i: (bi, 0, 0))
    heads_w = NSA_HPG * HEAD_DIM
    return pl.pallas_call(
        functools.partial(_nsa_kernel, tq=tq, tks=tks, tkw=tkw),
        grid=(b, g, t // tq),
        in_specs=[
            pl.BlockSpec((1, tq, heads_w), lambda bi, gi, i: (bi, i, gi)),
            per_group(kc), per_group(vc), shared(ks), per_group(vs), shared(kw), per_group(vw),
            pl.BlockSpec((1, tq, g * 3 * NSA_HPG), lambda bi, gi, i: (bi, i, 0)),
            pl.BlockSpec((n_chunk, 128), lambda bi, gi, i: (0, 0)),
        ],
        out_specs=pl.BlockSpec((1, tq, heads_w), lambda bi, gi, i: (bi, i, gi)),
        out_shape=jax.ShapeDtypeStruct(q.shape, F32),
        compiler_params=_cparams(("parallel", "parallel", "arbitrary"), VMEM_LIMIT),
        name="nsa_attention",
    )(q, kc, vc, ks, vs, kw, vw, gates, overlap)


def _mm2_res_kernel(h_ref, a_ref, b_ref, wa_ref, wb_ref, o_ref):
    acc = jnp.dot(a_ref[...].astype(BF16), wa_ref[...], preferred_element_type=F32)
    acc = acc + jnp.dot(b_ref[...].astype(BF16), wb_ref[...], preferred_element_type=F32)
    o_ref[...] = h_ref[...] + acc


def mm2_residual(h, a, b, wa, wb, *, tm=512):
    m, d = h.shape
    row = lambda w: pl.BlockSpec((tm, w), lambda i: (i, 0))
    whole = lambda w: pl.BlockSpec(w.shape, lambda i: (0, 0))
    return pl.pallas_call(
        _mm2_res_kernel,
        grid=(m // tm,),
        in_specs=[row(d), row(a.shape[1]), row(b.shape[1]), whole(wa), whole(wb)],
        out_specs=row(d),
        out_shape=jax.ShapeDtypeStruct((m, d), F32),
        compiler_params=_cparams(("parallel",), VMEM_LIMIT),
        name="out_proj_residual",
    )(h, a, b, wa, wb)


def _pool_kernel(cur_ref, halo_ref, g_ref, w_ref, sc_ref, o_ref, buf, *, tt):
    t = pl.program_id(1)
    g = g_ref[...]
    cur = cur_ref[0]
    hn = _rms(cur, g)
    buf[0:POOL_HALO, :] = jnp.where(t > 0, _rms(halo_ref[0], g), 0.0)
    buf[POOL_HALO:, :] = hn
    pos = t * tt + lax.broadcasted_iota(jnp.int32, (tt, 1), 0)
    outs = []
    for gi, w in enumerate(POOL_WINDOWS):
        cols = slice(gi * POOL_GROUP, (gi + 1) * POOL_GROUP)
        tot = hn[:, cols]
        for j in range(1, w):
            tot = tot + buf[pl.ds(POOL_HALO - j, tt), cols]
        cnt = jnp.minimum(pos + 1, w).astype(F32)
        d = tot / cnt - hn[:, cols]
        outs.append(jnp.dot(d.astype(BF16), w_ref[gi], preferred_element_type=F32))
    y = jnp.concatenate(outs, axis=-1) * sc_ref[...]
    o_ref[0] = cur + y


def pool_mixer_residual(h, g, pool_w, pool_scale, *, tt=512):
    b, t, d = h.shape
    tt = min(tt, t)
    per = tt // POOL_HALO
    return pl.pallas_call(
        functools.partial(_pool_kernel, tt=tt),
        grid=(b, t // tt),
        in_specs=[
            pl.BlockSpec((1, tt, d), lambda bi, ti: (bi, ti, 0)),
            pl.BlockSpec((1, POOL_HALO, d), lambda bi, ti: (bi, jnp.maximum(ti * per - 1, 0), 0)),
            pl.BlockSpec((1, d), lambda bi, ti: (0, 0)),
            pl.BlockSpec(pool_w.shape, lambda bi, ti: (0, 0, 0)),
            pl.BlockSpec((1, d), lambda bi, ti: (0, 0)),
        ],
        out_specs=pl.BlockSpec((1, tt, d), lambda bi, ti: (bi, ti, 0)),
        out_shape=jax.ShapeDtypeStruct(h.shape, F32),
        scratch_shapes=[pltpu.VMEM((tt + POOL_HALO, d), F32)],
        compiler_params=_cparams(("parallel", "parallel"), VMEM_LIMIT),
        name="pool_mixer",
    )(h, h, g.reshape(1, d), pool_w.astype(BF16), pool_scale.reshape(1, d))


def _peer_route_kernel(q_ref, key_ref, idx_ref, gate_ref):
    q = q_ref[...]
    s = lax.dot_general(key_ref[0], q, (((1,), (1,)), ((), ())), preferred_element_type=F32,
                        precision=HIGHEST)
    v0, i0 = _topk_rows_by_columns(s[0:N_KEYS], PEER_TOPK)
    v1, i1 = _topk_rows_by_columns(s[N_KEYS:], PEER_TOPK)
    s0 = jnp.concatenate(v0, axis=0)
    s1 = jnp.concatenate(v1, axis=0)
    step = lax.broadcasted_iota(jnp.int32, (8, 1), 0)
    stepf = step.astype(F32)
    blocks, codes = [], []
    for a, b0 in ((0, 0), (0, 8), (1, 0), (2, 0), (3, 0)):
        blocks.append(v0[a] + s1[b0:b0 + 8])
        codes.append(stepf + float(a * PEER_TOPK + b0))
    for a0, b in ((8, 0), (0, 0), (0, 1), (0, 2)):
        dup = jnp.where((step < 4) & (a0 == 0), -jnp.inf, 0.0)
        blocks.append(s0[a0:a0 + 8] + v1[b] + dup)
        codes.append((stepf + float(a0)) * PEER_TOPK + float(b))
    cv, ci = _topk_rows_by_columns(jnp.concatenate(blocks, axis=0), PEER_TOPK, jnp.concatenate(codes, axis=0))
    c_top = jnp.concatenate(cv, axis=0)
    c_idx = jnp.concatenate(ci, axis=0).astype(jnp.int32)
    a_sel = c_idx >> 4
    b_sel = c_idx & (PEER_TOPK - 1)
    k1 = jnp.zeros_like(c_top)
    k2 = jnp.zeros_like(c_top)
    for r in range(PEER_TOPK):
        k1 = jnp.where(a_sel == r, i0[r], k1)
        k2 = jnp.where(b_sel == r, i1[r], k2)
    idx_ref[0] = ((k1 * N_KEYS + k2) * ROW_SUB).astype(jnp.int32)
    e = jnp.exp(c_top - jnp.max(c_top, axis=0, keepdims=True))
    gate_ref[0] = e / jnp.sum(e, axis=0, keepdims=True)


def peer_route(q, keys_bd, *, tb=512):
    m = q.shape[0]
    shp = (PEER_HEADS, PEER_TOPK, m)
    ospec = pl.BlockSpec((1, PEER_TOPK, tb), lambda i, h: (h, 0, i))
    return pl.pallas_call(
        _peer_route_kernel,
        grid=(m // tb, PEER_HEADS),
        in_specs=[pl.BlockSpec((tb, 2 * HEAD_DIM), lambda i, h: (i, h)),
                  pl.BlockSpec((1, 2 * N_KEYS, 2 * HEAD_DIM), lambda i, h: (h, 0, 0))],
        out_specs=[ospec, ospec],
        out_shape=[jax.ShapeDtypeStruct(shp, jnp.int32), jax.ShapeDtypeStruct(shp, F32)],
        compiler_params=_cparams(("parallel", "parallel"), VMEM_LIMIT),
        name="peer_route",
    )(q, keys_bd)


def pack_table(tab):
    bits = lax.bitcast_convert_type(tab.astype(BF16), jnp.uint16).astype(jnp.uint32)
    packed = (bits[:, :HALF] << 16) | bits[:, HALF:]
    return packed.reshape(tab.shape[0] * ROW_SUB, 128)


STAGE_ROWS = PEER_KK * ROW_SUB
ACT_STAGES = 16
MIX_STAGES = 4
SLOT = 2 * ROW_SUB


def _slot_dims(c):
    start = (c // 2) * 128 + (0 if c % 2 else HALF)
    return slice(start, start + 128)


def _from_slots(y):
    slot_of = {(_slot_dims(c).start // 128): c for c in range(SLOT)}
    return jnp.concatenate([y[:, slot_of[blk] * 128:(slot_of[blk] + 1) * 128] for blk in range(SLOT)], axis=-1)


def _two_bf16(a, axis):
    hi = a.astype(BF16)
    lo = (a - hi.astype(F32)).astype(BF16)
    return jnp.concatenate([hi, lo], axis=axis)


def _for_both_blocks(tab_hbm, tab_vmem, idx_hbm, idx_smem, sems, run_block):
    i = pl.program_id(0)
    sets = (idx_smem[:PEER_KK], idx_smem[PEER_KK:])

    def copies(block, s):
        return [pltpu.make_async_copy(idx_hbm.at[k, block], sets[s][k], sems.at[1 + s * PEER_KK + k])
                for k in range(PEER_KK)]

    @pl.when(i == 0)
    def _():
        cp = pltpu.make_async_copy(tab_hbm, tab_vmem, sems.at[0])
        cp.start()
        for c in copies(0, 0):
            c.start()
        cp.wait()

    for c in copies(2 * i + 1, 1):
        c.start()
    for c in copies(2 * i, 0):
        c.wait()
    run_block(0, sets[0])

    @pl.when(i + 1 < pl.num_programs(0))
    def _():
        for c in copies(2 * i + 2, 0):
            c.start()

    for c in copies(2 * i + 1, 1):
        c.wait()
    run_block(1, sets[1])


def _for_each_token(tab_vmem, idx_smem, stages, tb, contract, finish):
    def gather(stage, t):
        for k in range(PEER_KK):
            row = pl.multiple_of(idx_smem[k][t], ROW_SUB)
            stage[pl.ds(ROW_SUB * k, ROW_SUB), :] = tab_vmem[pl.ds(row, ROW_SUB), :]

    group = len(stages)
    for s in range(group):
        gather(stages[s], s)

    def trip(i, carry):
        t = group * i
        parts = [contract(t + s, pltpu.bitcast(stages[s][...], BF16)) for s in range(group)]
        for s in range(group):
            finish(t + s, s, parts[s])
        for s in range(group):
            gather(stages[s], jnp.minimum(t + group + s, tb - 1))
        return carry

    lax.fori_loop(0, tb // group, trip, 0)


def _slot_mask():
    j = lax.broadcasted_iota(jnp.int32, (SLOT, 2 * STAGE_ROWS), 1)
    r = lax.broadcasted_iota(jnp.int32, (SLOT, 2 * STAGE_ROWS), 0)
    return (j & (SLOT - 1)) == r


def _peer_act_kernel(idx_hbm, tab_hbm, x_ref, gate_ref, fold_ref, coef_ref, tab_vmem, stages, zbuf, sems,
                     *idx_smem, tb, chunk):
    stages = [stages.at[n] for n in range(stages.shape[0])]
    mask = _slot_mask()

    def run_block(half, ids):
        base = half * tb

        def contract(t, rows):
            return lax.dot_general(x_ref[base + t], rows, (((1,), (1,)), ((), ())), preferred_element_type=F32)

        def keep(t, pos, g):
            zbuf[t] = jnp.where(mask, g[0:SLOT] + g[SLOT:], 0.0)

        _for_each_token(tab_vmem, ids, stages, tb, contract, keep)

        def finish(c, carry):
            r0 = pl.multiple_of(c * chunk, chunk)
            z = zbuf[pl.ds(r0, chunk)].reshape(chunk * SLOT, 2 * STAGE_ROWS)
            part = jnp.dot(_two_bf16(z, 0), fold_ref[...], preferred_element_type=F32)
            rows = part[0:chunk * SLOT] + part[chunk * SLOT:]
            act = jnp.sum(rows.reshape(chunk, SLOT, PEER_KK), axis=1)
            coef_ref[pl.ds(base + r0, chunk), :] = gate_ref[pl.ds(base + r0, chunk), :] * _gelu(act)
            return carry

        lax.fori_loop(0, tb // chunk, finish, 0)

    _for_both_blocks(tab_hbm, tab_vmem, idx_hbm, idx_smem, sems, run_block)


def peer_coefficients(idx_t, tab, x_terms, gate, *, tb=256, chunk=64):
    m = gate.shape[0]
    n_j = 2 * STAGE_ROWS
    fold = (np.arange(n_j)[:, None] // SLOT == np.arange(PEER_KK)[None, :]).astype(np.float32)
    return pl.pallas_call(
        functools.partial(_peer_act_kernel, tb=tb, chunk=chunk),
        grid=(m // (2 * tb),),
        in_specs=[pl.BlockSpec(memory_space=pl.ANY), pl.BlockSpec(memory_space=pl.ANY),
                  pl.BlockSpec((2 * tb, 2 * SLOT, 128), lambda i: (i, 0, 0)),
                  pl.BlockSpec((2 * tb, PEER_KK), lambda i: (i, 0)),
                  pl.BlockSpec((n_j, PEER_KK), lambda i: (0, 0))],
        out_specs=pl.BlockSpec((2 * tb, PEER_KK), lambda i: (i, 0)),
        out_shape=jax.ShapeDtypeStruct((m, PEER_KK), F32),
        scratch_shapes=[pltpu.VMEM(tab.shape, jnp.uint32),
                        pltpu.VMEM((ACT_STAGES, STAGE_ROWS, 128), jnp.uint32),
                        pltpu.VMEM((tb, SLOT, n_j), F32),
                        pltpu.SemaphoreType.DMA((1 + 2 * PEER_KK,))] + [pltpu.SMEM((tb,), jnp.int32)] * (2 * PEER_KK),
        compiler_params=_cparams(("arbitrary",), VMEM_LIMIT),
        name="peer_coefficients",
    )(idx_t.reshape(PEER_KK, m // tb, tb), tab, x_terms, gate, jnp.asarray(fold, BF16))


def _peer_mix_kernel(idx_hbm, tab_hbm, coef_ref, spread_ref, y_ref, tab_vmem, stages, lhs, sems,
                     *idx_smem, tb, chunk):
    stages = [stages.at[n] for n in range(stages.shape[0])]
    shape = (SLOT, 2 * STAGE_ROWS)
    sub = lax.broadcasted_iota(jnp.int32, shape, 0)
    col_slot = lax.broadcasted_iota(jnp.int32, shape, 1) & (SLOT - 1)
    low_slot = (sub & (ROW_SUB - 1)) * 2
    keep = (jnp.where(col_slot == low_slot, jnp.uint32(0x0000FFFF), jnp.uint32(0))
            | jnp.where(col_slot == low_slot + 1, jnp.uint32(0xFFFF0000), jnp.uint32(0)))
    first_term = sub < ROW_SUB

    def both_halves(a):
        bits = lax.bitcast_convert_type(a, jnp.uint32)
        return bits | (bits >> 16)

    def run_block(half, ids):
        base = half * tb

        def prepare(c, carry):
            r0 = pl.multiple_of(c * chunk, chunk)
            terms = _two_bf16(coef_ref[pl.ds(base + r0, chunk), :], 0)
            wide = jnp.dot(terms, spread_ref[...], preferred_element_type=F32)
            w_hi = both_halves(wide[0:chunk])
            w_lo = both_halves(wide[chunk:])
            for j in range(chunk):
                word = jnp.where(first_term, jnp.broadcast_to(w_hi[j:j + 1, :], shape),
                                 jnp.broadcast_to(w_lo[j:j + 1, :], shape))
                lhs[r0 + j] = word & keep
            return carry

        lax.fori_loop(0, tb // chunk, prepare, 0)

        def contract(t, rows):
            return jnp.dot(pltpu.bitcast(lhs[t], BF16), rows, preferred_element_type=F32)

        def store(t, pos, y):
            y_ref[base + t] = y[0:SLOT] + y[SLOT:]

        _for_each_token(tab_vmem, ids, stages, tb, contract, store)

    _for_both_blocks(tab_hbm, tab_vmem, idx_hbm, idx_smem, sems, run_block)


def peer_mix(idx_t, coef, tab, *, tb=256, chunk=64):
    m = coef.shape[0]
    n_j = 2 * STAGE_ROWS
    spread = (np.arange(PEER_KK)[:, None] == np.arange(n_j)[None, :] // SLOT).astype(np.float32)
    return pl.pallas_call(
        functools.partial(_peer_mix_kernel, tb=tb, chunk=chunk),
        grid=(m // (2 * tb),),
        in_specs=[pl.BlockSpec(memory_space=pl.ANY), pl.BlockSpec(memory_space=pl.ANY),
                  pl.BlockSpec((2 * tb, PEER_KK), lambda i: (i, 0)),
                  pl.BlockSpec((PEER_KK, n_j), lambda i: (0, 0))],
        out_specs=pl.BlockSpec((2 * tb, SLOT, 128), lambda i: (i, 0, 0)),
        out_shape=jax.ShapeDtypeStruct((m, SLOT, 128), F32),
        scratch_shapes=[pltpu.VMEM(tab.shape, jnp.uint32),
                        pltpu.VMEM((MIX_STAGES, STAGE_ROWS, 128), jnp.uint32),
                        pltpu.VMEM((tb, SLOT, n_j), jnp.uint32),
                        pltpu.SemaphoreType.DMA((1 + 2 * PEER_KK,))] + [pltpu.SMEM((tb,), jnp.int32)] * (2 * PEER_KK),
        compiler_params=_cparams(("arbitrary",), VMEM_LIMIT),
        name="peer_mix",
    )(idx_t.reshape(PEER_KK, m // tb, tb), tab, coef, jnp.asarray(spread, BF16))


def _ple_kernel(h_ref, y_ref, p_ref, g_ref, wg_ref, wp_ref, fg_ref, o_ref, *, final_norm):
    h = h_ref[...] + _from_slots(y_ref[...])
    gate = _sigmoid(jnp.dot(_rms(h, g_ref[...]).astype(BF16), wg_ref[...], preferred_element_type=F32))
    h = h + jnp.dot(p_ref[...].astype(BF16), wp_ref[...], preferred_element_type=F32) * gate
    if final_norm:
        h = _rms(h, fg_ref[...])
    o_ref[...] = h


def ple_residual(h, y, p, g, wg, wp, fg, *, final_norm, tm=512):
    m, d = h.shape
    row = lambda w: pl.BlockSpec((tm, w), lambda i: (i, 0))
    whole = lambda a: pl.BlockSpec(a.shape, lambda i: (0, 0))
    g2, fg2 = g.reshape(1, d), fg.reshape(1, d)
    return pl.pallas_call(
        functools.partial(_ple_kernel, final_norm=final_norm),
        grid=(m // tm,),
        in_specs=[row(d), row(d), row(p.shape[1]), whole(g2), whole(wg), whole(wp), whole(fg2)],
        out_specs=row(d),
        out_shape=jax.ShapeDtypeStruct((m, d), F32),
        compiler_params=_cparams(("parallel",), VMEM_LIMIT),
        name="ple_residual",
    )(h, y, p, g2, wg, wp, fg2)


def _mixer_conv_nsa(h, b, t, norm_g, w_in, conv_w, conv_b, ln_g, ln_b, cmp_pos, cmp_w1, cmp_w2, w_out):
    m = b * t
    n_q = NSA_KV_HEADS * NSA_HPG * HEAD_DIM
    kv_w = NSA_KV_HEADS * HEAD_DIM
    c0, c1, c2 = 2 * CONV_CH, 2 * CONV_CH + n_q, 2 * CONV_CH + n_q + 6 * kv_w
    wb = w_in.astype(BF16)
    vg, q, kv, gt = norm_matmul(h, norm_g, [wb[:, :c0], wb[:, c0:c1], wb[:, c1:c2], wb[:, c2:]])
    a_out = conformer_conv(vg.reshape(b, t, c0), conv_w, conv_b, ln_g, ln_b)

    kv = kv.reshape(b, t, 6, kv_w)
    kind = lambda c: kv[:, :, c]
    n_chunk = t // CMP_STRIDE
    cmp_src = jnp.stack([kind(0), kind(1)]).reshape(2, b, n_chunk, CMP_STRIDE * kv_w)
    cmp = compress_blocks(cmp_src, cmp_pos, cmp_w1, cmp_w2)

    lane_group = jnp.arange(kv_w) // HEAD_DIM

    def value_operand(v):
        per_group = []
        for g in range(NSA_KV_HEADS):
            other = jnp.where(jnp.arange(kv_w) == HEAD_DIM * (1 - g), 1.0, 0.0)
            per_group.append(jnp.where(lane_group == g, v, other).astype(BF16))
        return jnp.stack(per_group, axis=1)

    o = nsa_attention(q.reshape(b, t, n_q), cmp[0], cmp[1], kind(2).astype(BF16), value_operand(kind(3)),
                      kind(4).astype(BF16), value_operand(kind(5)), gt.reshape(b, t, -1))
    wo = w_out.astype(BF16)
    return mm2_residual(h, a_out.reshape(m, CONV_CH), o.reshape(m, n_q), wo[:CONV_CH], wo[CONV_CH:])


def _peer_ffn(h, norm_g, wq, subkeys, u_tab, v_tab):
    q, x_terms = norm_matmul(h, norm_g, [wq.astype(BF16)], emit_hn=True)
    zeros = jnp.zeros_like(subkeys[:, 0])
    keys_bd = jnp.concatenate([jnp.concatenate([subkeys[:, 0], zeros], axis=-1),
                               jnp.concatenate([zeros, subkeys[:, 1]], axis=-1)], axis=1)
    idx_t, gate_t = peer_route(q, keys_bd)
    m = h.shape[0]
    idx_t = idx_t.reshape(PEER_KK, m)
    gate = gate_t.reshape(PEER_KK, m).T
    coef = peer_coefficients(idx_t, pack_table(u_tab), x_terms.reshape(m, 2 * SLOT, 128), gate)
    return peer_mix(idx_t, coef, pack_table(v_tab)).reshape(m, SLOT * 128)


def kernel(x, p, mix_norm, ab_w_in, ab_conv_w, ab_conv_b, ab_conv_ln_g, ab_conv_ln_b, ab_cmp_pos, ab_cmp_w1,
           ab_cmp_w2, ab_w_out, pool_w, pool_scale, ffn_norm, peer_wq, peer_subkeys, peer_u, peer_v, ple_norm,
           ple_gate_w, ple_proj, final_norm):
    b, t, d = x.shape
    m = b * t
    depth = p.shape[0]
    h = x.reshape(m, d)
    for i in range(depth):
        j = i // 2
        if i % 2 == 0:
            h = _mixer_conv_nsa(h, b, t, mix_norm[i], ab_w_in[j], ab_conv_w[j], ab_conv_b[j], ab_conv_ln_g[j],
                                ab_conv_ln_b[j], ab_cmp_pos[j], ab_cmp_w1[j], ab_cmp_w2[j], ab_w_out[j])
        else:
            h = pool_mixer_residual(h.reshape(b, t, d), mix_norm[i], pool_w[j], pool_scale[j]).reshape(m, d)
        y = _peer_ffn(h, ffn_norm[i], peer_wq[i], peer_subkeys[i], peer_u[i], peer_v[i])
        h = ple_residual(h, y, p[i].reshape(m, -1), ple_norm[i], ple_gate_w[i].astype(BF16),
                         ple_proj[i].astype(BF16), final_norm, final_norm=(i == depth - 1))
    return h.reshape(b, t, d)
```

```python
import functools

import jax
import jax.numpy as jnp
import numpy as np
from jax import lax
from jax.experimental import pallas as pl
from jax.experimental.pallas import tpu as pltpu

F32 = jnp.float32
BF16 = jnp.bfloat16
HIGHEST = lax.Precision.HIGHEST

D_MODEL = 1024
NORM_EPS = 1e-6
NEG_INF = -1e30

CONV_CH = 512
CONV_WIDTH = 31
CONV_HALO = 32

HEAD_DIM = 64
NSA_KV_HEADS = 2
NSA_HPG = 4
CMP_LEN = 32
CMP_STRIDE = 16
CMP_HIDDEN = 256
SLC_BLOCK = 64
SLC_SHIFT = 6
SLC_TOP = 16
WINDOW = 512
FORCE_SCORE = 1e9

POOL_WINDOWS = (2, 4, 8, 16)
POOL_GROUP = 256
POOL_HALO = 16

PEER_HEADS = 8
N_KEYS = 128
PEER_TOPK = 16
PEER_KK = PEER_HEADS * PEER_TOPK
HALF = D_MODEL // 2
ROW_SUB = HALF // 128

VMEM_LIMIT = 56 * 1024 * 1024


def _cparams(sem, vmem=None):
    return pltpu.CompilerParams(dimension_semantics=sem, vmem_limit_bytes=vmem)


def _rms(x, g):
    return x * lax.rsqrt(jnp.mean(x * x, axis=-1, keepdims=True) + NORM_EPS) * g


def _gelu(x):
    return 0.5 * x * (1.0 + jnp.tanh(0.7978845608028654 * (x + 0.044715 * (x * x * x))))


def _sigmoid(x):
    return 1.0 / (1.0 + jnp.exp(-x))


def _norm_mm_kernel(x_ref, g_ref, *refs, n_w, emit_hn):
    w_refs = refs[:n_w]
    o_refs = refs[n_w:]
    y = _rms(x_ref[...], g_ref[...])
    yb = y.astype(BF16)
    for w_ref, o_ref in zip(w_refs, o_refs[:n_w]):
        o_ref[...] = jnp.dot(yb, w_ref[...], preferred_element_type=F32)
    if emit_hn:
        lo = (y - yb.astype(F32)).astype(BF16)
        blocks = [term[:, _slot_dims(c)] for term in (yb, lo) for c in range(SLOT)]
        o_refs[n_w][...] = jnp.concatenate(blocks, axis=-1)


def norm_matmul(x, g, ws, *, emit_hn=False, tm=512):
    m, d = x.shape
    n_w = len(ws)
    in_specs = [pl.BlockSpec((tm, d), lambda i: (i, 0)), pl.BlockSpec((1, d), lambda i: (0, 0))]
    in_specs += [pl.BlockSpec(w.shape, lambda i: (0, 0)) for w in ws]
    out_shape = [jax.ShapeDtypeStruct((m, w.shape[1]), F32) for w in ws]
    out_specs = [pl.BlockSpec((tm, w.shape[1]), lambda i: (i, 0)) for w in ws]
    if emit_hn:
        out_shape += [jax.ShapeDtypeStruct((m, 2 * d), BF16)]
        out_specs += [pl.BlockSpec((tm, 2 * d), lambda i: (i, 0))]
    return pl.pallas_call(
        functools.partial(_norm_mm_kernel, n_w=n_w, emit_hn=emit_hn),
        grid=(m // tm,),
        in_specs=in_specs, out_specs=out_specs, out_shape=out_shape,
        compiler_params=_cparams(("parallel",), VMEM_LIMIT),
        name="norm_matmul",
    )(x, g.reshape(1, d), *ws)


def _conv_kernel(cur_ref, halo_ref, w_ref, b_ref, g_ref, beta_ref, o_ref, buf, *, tt):
    t = pl.program_id(1)
    cur = cur_ref[0]
    halo = halo_ref[0]
    a_halo = halo[:, :CONV_CH] * _sigmoid(halo[:, CONV_CH:])
    buf[0:CONV_HALO, :] = jnp.where(t > 0, a_halo, 0.0)
    buf[CONV_HALO:, :] = cur[:, :CONV_CH] * _sigmoid(cur[:, CONV_CH:])
    acc = jnp.zeros((tt, CONV_CH), F32)
    first = CONV_HALO - (CONV_WIDTH - 1)
    for j in range(CONV_WIDTH):
        acc = acc + buf[pl.ds(first + j, tt), :] * w_ref[j:j + 1, :]
    y = acc + b_ref[...]
    mu = jnp.mean(y, axis=-1, keepdims=True)
    yc = y - mu
    var = jnp.mean(yc * yc, axis=-1, keepdims=True)
    y = yc * lax.rsqrt(var + NORM_EPS) * g_ref[...] + beta_ref[...]
    o_ref[0] = y * _sigmoid(y)


def conformer_conv(vg, conv_w, conv_b, ln_g, ln_b, *, tt=512):
    b, t, _ = vg.shape
    tt = min(tt, t)
    per = tt // CONV_HALO
    vec = lambda v: v.reshape(1, CONV_CH)
    vspec = pl.BlockSpec((1, CONV_CH), lambda bi, ti: (0, 0))
    return pl.pallas_call(
        functools.partial(_conv_kernel, tt=tt),
        grid=(b, t // tt),
        in_specs=[
            pl.BlockSpec((1, tt, 2 * CONV_CH), lambda bi, ti: (bi, ti, 0)),
            pl.BlockSpec((1, CONV_HALO, 2 * CONV_CH), lambda bi, ti: (bi, jnp.maximum(ti * per - 1, 0), 0)),
            pl.BlockSpec((CONV_WIDTH, CONV_CH), lambda bi, ti: (0, 0)),
            vspec, vspec, vspec,
        ],
        out_specs=pl.BlockSpec((1, tt, CONV_CH), lambda bi, ti: (bi, ti, 0)),
        out_shape=jax.ShapeDtypeStruct((b, t, CONV_CH), F32),
        scratch_shapes=[pltpu.VMEM((tt + CONV_HALO, CONV_CH), F32)],
        compiler_params=_cparams(("parallel", "parallel"), VMEM_LIMIT),
        name="conformer_conv",
    )(vg, vg, conv_w, vec(conv_b), vec(ln_g), vec(ln_b))


def _compress_kernel(c_ref, pos_ref, w1_ref, w2_ref, o_ref):
    c = c_ref[0, 0]
    n = c.shape[0]
    ca = c + pos_ref[0, 0]
    cb = c + pos_ref[0, 1]
    for g in range(NSA_KV_HEADS):
        ua = jnp.dot(ca, w1_ref[0, g, 0], preferred_element_type=F32, precision=HIGHEST)
        ub = jnp.dot(cb, w1_ref[0, g, 1], preferred_element_type=F32, precision=HIGHEST)
        hid = _gelu(ua + pltpu.roll(ub, n - 1, axis=0))
        o_ref[0, 0, g] = jnp.dot(hid, w2_ref[0], preferred_element_type=F32, precision=HIGHEST)


def compress_blocks(src, pos, w1, w2):
    _, b, n_chunk, width = src.shape
    groups = NSA_KV_HEADS
    posx = jnp.broadcast_to(pos.reshape(2, 2, CMP_STRIDE, 1, HEAD_DIM), (2, 2, CMP_STRIDE, groups, HEAD_DIM))
    posx = posx.reshape(2, 2, 1, width)
    w1r = w1.reshape(2, 1, 2, CMP_STRIDE, 1, HEAD_DIM, CMP_HIDDEN)
    own = (jnp.arange(groups)[:, None] == jnp.arange(groups)[None, :]).reshape(1, groups, 1, 1, groups, 1, 1)
    w1x = jnp.where(own, w1r, 0.0).reshape(2, groups, 2, width, CMP_HIDDEN)
    return pl.pallas_call(
        _compress_kernel,
        grid=(2, b),
        in_specs=[
            pl.BlockSpec((1, 1, n_chunk, width), lambda k, i: (k, i, 0, 0)),
            pl.BlockSpec((1, 2, 1, width), lambda k, i: (k, 0, 0, 0)),
            pl.BlockSpec((1, groups, 2, width, CMP_HIDDEN), lambda k, i: (k, 0, 0, 0, 0)),
            pl.BlockSpec((1, CMP_HIDDEN, HEAD_DIM), lambda k, i: (k, 0, 0)),
        ],
        out_specs=pl.BlockSpec((1, 1, groups, n_chunk, HEAD_DIM), lambda k, i: (k, i, 0, 0, 0)),
        out_shape=jax.ShapeDtypeStruct((2, b, groups, n_chunk, HEAD_DIM), F32),
        compiler_params=_cparams(("parallel", "parallel"), VMEM_LIMIT),
        name="nsa_compress",
    )(src, posx, w1x, w2)


def _sorting_network(n):
    pairs = []
    p = 1
    while p < n:
        k = p
        while k >= 1:
            for j in range(k % p, n - k, 2 * k):
                for i in range(min(k, n - j - k)):
                    if (i + j) // (2 * p) == (i + j + k) // (2 * p):
                        pairs.append((i + j, i + j + k))
            k //= 2
        p *= 2
    return pairs


def _topk_rows_by_columns(work, k, row=None):
    n = work.shape[0] // 8
    lanes = work.shape[1]
    vals = [work[8 * i:8 * (i + 1)] for i in range(n)]
    if row is None:
        sub = lax.broadcasted_iota(jnp.int32, (8, lanes), 0).astype(F32)
        ids = [sub + float(8 * i) for i in range(n)]
    else:
        ids = [jnp.broadcast_to(row[8 * i:8 * (i + 1)], (8, lanes)) for i in range(n)]
    for a, b in _sorting_network(pl.next_power_of_2(n)):
        if b >= n:
            continue
        first = (vals[a] > vals[b]) | ((vals[a] == vals[b]) & (ids[a] < ids[b]))
        vals[a], vals[b] = jnp.where(first, vals[a], vals[b]), jnp.where(first, vals[b], vals[a])
        ids[a], ids[b] = jnp.where(first, ids[a], ids[b]), jnp.where(first, ids[b], ids[a])
    out_v, out_i = [], []
    for r in range(k):
        m = jnp.max(vals[0], axis=0, keepdims=True)
        pick = jnp.min(jnp.where(vals[0] == m, ids[0], 3.0e38), axis=0, keepdims=True)
        out_v.append(m)
        out_i.append(pick)
        won = ids[0] == pick
        depth = k - 1 - r
        for i in range(min(depth, n - 1)):
            vals[i] = jnp.where(won, vals[i + 1], vals[i])
            ids[i] = jnp.where(won, ids[i + 1], ids[i])
        if 0 < n <= depth:
            vals[n - 1] = jnp.where(won, -jnp.inf, vals[n - 1])
    return out_v, out_i


def _masked_flash(qb, k_ref, v_ref, lo, hi, tk, mask_fn, first_group):
    rows = qb.shape[0]
    tq = rows // NSA_HPG

    def scores(j):
        kt = k_ref[0, pl.ds(pl.multiple_of(j * tk, tk), tk), :]
        s = lax.dot_general(qb, kt, (((1,), (1,)), ((), ())), preferred_element_type=F32)
        return (s.reshape(NSA_HPG, tq, tk) + mask_fn(j)[None]).reshape(rows, tk)

    def body(j, carry):
        m, acc, s = carry
        s_next = scores(jnp.minimum(j + 1, hi - 1))
        vt = v_ref[0, 0, pl.ds(pl.multiple_of(j * tk, tk), tk), :]
        m_new = jnp.maximum(m, jnp.max(s, axis=-1, keepdims=True))
        p = jnp.exp(s - m_new)
        acc = jnp.exp(m - m_new) * acc + jnp.dot(p.astype(BF16), vt, preferred_element_type=F32)
        return m_new, acc, s_next

    init = (jnp.full((rows, 1), NEG_INF, F32), jnp.zeros((rows, 2 * HEAD_DIM), F32), scores(lo))
    _, acc, _ = lax.fori_loop(lo, hi, body, init)
    left, right = acc[:, :HEAD_DIM], acc[:, HEAD_DIM:]
    return jnp.where(first_group, left / right[:, 0:1], right / left[:, 0:1])


def _nsa_kernel(q_ref, kc_ref, vc_ref, ks_ref, vs_ref, kw_ref, vw_ref, g_ref, ov_ref, o_ref, *, tq, tks, tkw):
    i = pl.program_id(2)
    first_group = pl.program_id(1) == 0
    t0 = i * tq
    rows = NSA_HPG * tq
    q_tile = q_ref[0]
    q = jnp.concatenate([q_tile[:, h * HEAD_DIM:(h + 1) * HEAD_DIM] for h in range(NSA_HPG)], axis=0)
    q = q * (HEAD_DIM ** -0.5)
    zero = jnp.zeros_like(q)
    qb = jnp.where(first_group, jnp.concatenate([q, zero], axis=-1),
                   jnp.concatenate([zero, q], axis=-1)).astype(BF16)
    t_q = t0 + lax.broadcasted_iota(jnp.int32, (tq, 1), 0)
    t_all = jnp.concatenate([t_q] * NSA_HPG, axis=0)

    kc = kc_ref[0, 0]
    n_cmp = kc.shape[0]
    s = lax.dot_general(q, kc, (((1,), (1,)), ((), ())), preferred_element_type=F32, precision=HIGHEST)
    cmp_end = lax.broadcasted_iota(jnp.int32, (1, n_cmp), 1) * CMP_STRIDE + (CMP_LEN - 1)
    ok = cmp_end <= t_all
    s = jnp.where(ok, s, NEG_INF)
    e = jnp.where(ok, jnp.exp(s - jnp.max(s, axis=-1, keepdims=True)), 0.0)
    den = jnp.sum(e, axis=-1, keepdims=True)
    p_cmp = e / jnp.where(den > 0.0, den, 1.0)
    o_cmp = jnp.dot(p_cmp.astype(BF16), vc_ref[0, 0].astype(BF16), preferred_element_type=F32)

    p_sum = p_cmp[0:tq]
    for h in range(1, NSA_HPG):
        p_sum = p_sum + p_cmp[h * tq:(h + 1) * tq]
    imp = jnp.dot(p_sum, ov_ref[...], preferred_element_type=F32, precision=HIGHEST)
    n_slc = ks_ref.shape[1] // SLC_BLOCK
    imp_t = imp.T[0:n_slc]
    blk = lax.broadcasted_iota(jnp.int32, (n_slc, tq), 0)
    t_lane = t0 + lax.broadcasted_iota(jnp.int32, (n_slc, tq), 1)
    cur = t_lane >> SLC_SHIFT
    forced = (blk == 0) | (blk == cur) | (blk == cur - 1)
    imp_t = jnp.where(forced, FORCE_SCORE, imp_t)
    imp_t = jnp.where(blk * SLC_BLOCK <= t_lane, imp_t, NEG_INF)
    _, picks = _topk_rows_by_columns(imp_t, min(SLC_TOP, n_slc))
    blk_f = blk.astype(F32)
    member = jnp.zeros((n_slc, tq), F32)
    for pk in picks:
        member = jnp.where(blk_f == pk, 1.0, member)
    if n_slc < 128:
        member = jnp.concatenate([member, jnp.zeros((128 - n_slc, tq), F32)], axis=0)
    member_q = member.T.astype(BF16)

    blocks_per_tile = tks // SLC_BLOCK

    def slc_mask(j):
        sel_row = lax.broadcasted_iota(jnp.int32, (128, tks), 0)
        key_blk = j * blocks_per_tile + (lax.broadcasted_iota(jnp.int32, (128, tks), 1) >> SLC_SHIFT)
        expand = jnp.where(sel_row == key_blk, 1.0, 0.0).astype(BF16)
        sel = jnp.dot(member_q, expand, preferred_element_type=F32) > 0.5
        kpos = j * tks + lax.broadcasted_iota(jnp.int32, (1, tks), 1)
        return jnp.where(sel & (kpos <= t_q), 0.0, NEG_INF)

    o_slc = _masked_flash(qb, ks_ref, vs_ref, 0, (t0 + tq + tks - 1) // tks, tks, slc_mask, first_group)

    def win_mask(j):
        kpos = j * tkw + lax.broadcasted_iota(jnp.int32, (1, tkw), 1)
        dist = t_q - kpos
        return jnp.where((dist >= 0) & (dist < WINDOW), 0.0, NEG_INF)

    lo = jnp.maximum(t0 - (WINDOW - 1), 0) // tkw
    o_win = _masked_flash(qb, kw_ref, vw_ref, lo, (t0 + tq + tkw - 1) // tkw, tkw, win_mask, first_group)

    gate = _sigmoid(g_ref[0])
    n_gate = 3 * NSA_HPG
    gate = jnp.where(first_group, gate[:, :n_gate], gate[:, n_gate:])
    heads = []
    for h in range(NSA_HPG):
        r = slice(h * tq, (h + 1) * tq)
        heads.append(gate[:, 3 * h:3 * h + 1] * o_cmp[r] + gate[:, 3 * h + 1:3 * h + 2] * o_slc[r]
                     + gate[:, 3 * h + 2:3 * h + 3] * o_win[r])
    o_ref[0] = jnp.concatenate(heads, axis=-1)


def nsa_attention(q, kc, vc, ks, vs, kw, vw, gates, *, tq=256, tks=256, tkw=128):
    b, t, _ = q.shape
    g = NSA_KV_HEADS
    n_chunk = kc.shape[2]
    n_slc = t // SLC_BLOCK
    n_cmp = (t - CMP_LEN) // CMP_STRIDE + 1
    tks = min(tks, t)
    cmp_start = np.arange(n_chunk) * CMP_STRIDE
    slc_start = np.arange(128) * SLC_BLOCK
    overlap = ((cmp_start[:, None] < slc_start[None, :] + SLC_BLOCK)
               & (cmp_start[:, None] + CMP_LEN > slc_start[None, :])
               & (np.arange(n_chunk)[:, None] < n_cmp) & (np.arange(128)[None, :] < n_slc))
    overlap = jnp.asarray(overlap.astype(np.float32))
    per_group = lambda a: pl.BlockSpec((1, 1) + a.shape[2:], lambda bi, gi, i: (bi, gi, 0, 0))
    shared = lambda a: pl.BlockSpec((1,) + a.shape[1:], lambda bi, gi, i: (bi, 0, 0))
    heads_w = NSA_HPG * HEAD_DIM
    return pl.pallas_call(
        functools.partial(_nsa_kernel, tq=tq, tks=tks, tkw=tkw),
        grid=(b, g, t // tq),
        in_specs=[
            pl.BlockSpec((1, tq, heads_w), lambda bi, gi, i: (bi, i, gi)),
            per_group(kc), per_group(vc), shared(ks), per_group(vs), shared(kw), per_group(vw),
            pl.BlockSpec((1, tq, g * 3 * NSA_HPG), lambda bi, gi, i: (bi, i, 0)),
            pl.BlockSpec((n_chunk, 128), lambda bi, gi, i: (0, 0)),
        ],
        out_specs=pl.BlockSpec((1, tq, heads_w), lambda bi, gi, i: (bi, i, gi)),
        out_shape=jax.ShapeDtypeStruct(q.shape, F32),
        compiler_params=_cparams(("parallel", "parallel", "arbitrary"), VMEM_LIMIT),
        name="nsa_attention",
    )(q, kc, vc, ks, vs, kw, vw, gates, overlap)


def _mm2_res_kernel(h_ref, a_ref, b_ref, wa_ref, wb_ref, o_ref):
    acc = jnp.dot(a_ref[...].astype(BF16), wa_ref[...], preferred_element_type=F32)
    acc = acc + jnp.dot(b_ref[...].astype(BF16), wb_ref[...], preferred_element_type=F32)
    o_ref[...] = h_ref[...] + acc


def mm2_residual(h, a, b, wa, wb, *, tm=512):
    m, d = h.shape
    row = lambda w: pl.BlockSpec((tm, w), lambda i: (i, 0))
    whole = lambda w: pl.BlockSpec(w.shape, lambda i: (0, 0))
    return pl.pallas_call(
        _mm2_res_kernel,
        grid=(m // tm,),
        in_specs=[row(d), row(a.shape[1]), row(b.shape[1]), whole(wa), whole(wb)],
        out_specs=row(d),
        out_shape=jax.ShapeDtypeStruct((m, d), F32),
        compiler_params=_cparams(("parallel",), VMEM_LIMIT),
        name="out_proj_residual",
    )(h, a, b, wa, wb)


def _pool_kernel(cur_ref, halo_ref, g_ref, w_ref, sc_ref, o_ref, buf, *, tt):
    t = pl.program_id(1)
    g = g_ref[...]
    cur = cur_ref[0]
    hn = _rms(cur, g)
    buf[0:POOL_HALO, :] = jnp.where(t > 0, _rms(halo_ref[0], g), 0.0)
    buf[POOL_HALO:, :] = hn
    pos = t * tt + lax.broadcasted_iota(jnp.int32, (tt, 1), 0)
    outs = []
    for gi, w in enumerate(POOL_WINDOWS):
        cols = slice(gi * POOL_GROUP, (gi + 1) * POOL_GROUP)
        tot = hn[:, cols]
        for j in range(1, w):
            tot = tot + buf[pl.ds(POOL_HALO - j, tt), cols]
        cnt = jnp.minimum(pos + 1, w).astype(F32)
        d = tot / cnt - hn[:, cols]
        outs.append(jnp.dot(d.astype(BF16), w_ref[gi], preferred_element_type=F32))
    y = jnp.concatenate(outs, axis=-1) * sc_ref[...]
    o_ref[0] = cur + y


def pool_mixer_residual(h, g, pool_w, pool_scale, *, tt=512):
    b, t, d = h.shape
    tt = min(tt, t)
    per = tt // POOL_HALO
    return pl.pallas_call(
        functools.partial(_pool_kernel, tt=tt),
        grid=(b, t // tt),
        in_specs=[
            pl.BlockSpec((1, tt, d), lambda bi, ti: (bi, ti, 0)),
            pl.BlockSpec((1, POOL_HALO, d), lambda bi, ti: (bi, jnp.maximum(ti * per - 1, 0), 0)),
            pl.BlockSpec((1, d), lambda bi, ti: (0, 0)),
            pl.BlockSpec(pool_w.shape, lambda bi, ti: (0, 0, 0)),
            pl.BlockSpec((1, d), lambda bi, ti: (0, 0)),
        ],
        out_specs=pl.BlockSpec((1, tt, d), lambda bi, ti: (bi, ti, 0)),
        out_shape=jax.ShapeDtypeStruct(h.shape, F32),
        scratch_shapes=[pltpu.VMEM((tt + POOL_HALO, d), F32)],
        compiler_params=_cparams(("parallel", "parallel"), VMEM_LIMIT),
        name="pool_mixer",
    )(h, h, g.reshape(1, d), pool_w.astype(BF16), pool_scale.reshape(1, d))


def _peer_route_kernel(q_ref, key_ref, idx_ref, gate_ref):
    q = q_ref[...]
    s = lax.dot_general(key_ref[0], q, (((1,), (1,)), ((), ())), preferred_element_type=F32,
                        precision=HIGHEST)
    v0, i0 = _topk_rows_by_columns(s[0:N_KEYS], PEER_TOPK)
    v1, i1 = _topk_rows_by_columns(s[N_KEYS:], PEER_TOPK)
    s0 = jnp.concatenate(v0, axis=0)
    s1 = jnp.concatenate(v1, axis=0)
    step = lax.broadcasted_iota(jnp.int32, (8, 1), 0)
    stepf = step.astype(F32)
    blocks, codes = [], []
    for a, b0 in ((0, 0), (0, 8), (1, 0), (2, 0), (3, 0)):
        blocks.append(v0[a] + s1[b0:b0 + 8])
        codes.append(stepf + float(a * PEER_TOPK + b0))
    for a0, b in ((8, 0), (0, 0), (0, 1), (0, 2)):
        dup = jnp.where((step < 4) & (a0 == 0), -jnp.inf, 0.0)
        blocks.append(s0[a0:a0 + 8] + v1[b] + dup)
        codes.append((stepf + float(a0)) * PEER_TOPK + float(b))
    cv, ci = _topk_rows_by_columns(jnp.concatenate(blocks, axis=0), PEER_TOPK, jnp.concatenate(codes, axis=0))
    c_top = jnp.concatenate(cv, axis=0)
    c_idx = jnp.concatenate(ci, axis=0).astype(jnp.int32)
    a_sel = c_idx >> 4
    b_sel = c_idx & (PEER_TOPK - 1)
    k1 = jnp.zeros_like(c_top)
    k2 = jnp.zeros_like(c_top)
    for r in range(PEER_TOPK):
        k1 = jnp.where(a_sel == r, i0[r], k1)
        k2 = jnp.where(b_sel == r, i1[r], k2)
    idx_ref[0] = ((k1 * N_KEYS + k2) * ROW_SUB).astype(jnp.int32)
    e = jnp.exp(c_top - jnp.max(c_top, axis=0, keepdims=True))
    gate_ref[0] = e / jnp.sum(e, axis=0, keepdims=True)


def peer_route(q, keys_bd, *, tb=512):
    m = q.shape[0]
    shp = (PEER_HEADS, PEER_TOPK, m)
    ospec = pl.BlockSpec((1, PEER_TOPK, tb), lambda i, h: (h, 0, i))
    return pl.pallas_call(
        _peer_route_kernel,
        grid=(m // tb, PEER_HEADS),
        in_specs=[pl.BlockSpec((tb, 2 * HEAD_DIM), lambda i, h: (i, h)),
                  pl.BlockSpec((1, 2 * N_KEYS, 2 * HEAD_DIM), lambda i, h: (h, 0, 0))],
        out_specs=[ospec, ospec],
        out_shape=[jax.ShapeDtypeStruct(shp, jnp.int32), jax.ShapeDtypeStruct(shp, F32)],
        compiler_params=_cparams(("parallel", "parallel"), VMEM_LIMIT),
        name="peer_route",
    )(q, keys_bd)


def pack_table(tab):
    bits = lax.bitcast_convert_type(tab, jnp.uint32)
    rounded = bits + jnp.uint32(0x7FFF) + ((bits >> 16) & jnp.uint32(1))
    packed = (rounded[:, :HALF] & jnp.uint32(0xFFFF0000)) | (rounded[:, HALF:] >> 16)
    return packed.reshape(tab.shape[0] * ROW_SUB, 128)


STAGE_ROWS = PEER_KK * ROW_SUB
ACT_STAGES = 16
MIX_STAGES = 4
SLOT = 2 * ROW_SUB


def _slot_dims(c):
    start = (c // 2) * 128 + (0 if c % 2 else HALF)
    return slice(start, start + 128)


def _from_slots(y):
    slot_of = {(_slot_dims(c).start // 128): c for c in range(SLOT)}
    return jnp.concatenate([y[:, slot_of[blk] * 128:(slot_of[blk] + 1) * 128] for blk in range(SLOT)], axis=-1)


def _two_bf16(a, axis):
    hi = a.astype(BF16)
    lo = (a - hi.astype(F32)).astype(BF16)
    return jnp.concatenate([hi, lo], axis=axis)


def _for_both_blocks(tab_hbm, tab_vmem, idx_hbm, idx_smem, sems, run_block):
    i = pl.program_id(0)
    sets = (idx_smem[:PEER_KK], idx_smem[PEER_KK:])

    def copies(block, s):
        return [pltpu.make_async_copy(idx_hbm.at[k, block], sets[s][k], sems.at[1 + s * PEER_KK + k])
                for k in range(PEER_KK)]

    @pl.when(i == 0)
    def _():
        cp = pltpu.make_async_copy(tab_hbm, tab_vmem, sems.at[0])
        cp.start()
        for c in copies(0, 0):
            c.start()
        cp.wait()

    for c in copies(2 * i + 1, 1):
        c.start()
    for c in copies(2 * i, 0):
        c.wait()
    run_block(0, sets[0])

    @pl.when(i + 1 < pl.num_programs(0))
    def _():
        for c in copies(2 * i + 2, 0):
            c.start()

    for c in copies(2 * i + 1, 1):
        c.wait()
    run_block(1, sets[1])


def _for_each_token(tab_vmem, idx_smem, stages, tb, contract, finish):
    def gather(stage, t):
        for k in range(PEER_KK):
            row = pl.multiple_of(idx_smem[k][t], ROW_SUB)
            stage[pl.ds(ROW_SUB * k, ROW_SUB), :] = tab_vmem[pl.ds(row, ROW_SUB), :]

    group = len(stages)
    for s in range(group):
        gather(stages[s], s)

    def trip(i, carry):
        t = group * i
        parts = [contract(t + s, pltpu.bitcast(stages[s][...], BF16)) for s in range(group)]
        for s in range(group):
            finish(t + s, s, parts[s])
        for s in range(group):
            gather(stages[s], jnp.minimum(t + group + s, tb - 1))
        return carry

    lax.fori_loop(0, tb // group, trip, 0)


def _slot_mask():
    j = lax.broadcasted_iota(jnp.int32, (SLOT, 2 * STAGE_ROWS), 1)
    r = lax.broadcasted_iota(jnp.int32, (SLOT, 2 * STAGE_ROWS), 0)
    return (j & (SLOT - 1)) == r


def _peer_act_kernel(idx_hbm, tab_hbm, x_ref, gate_ref, fold_ref, coef_ref, tab_vmem, stages, zbuf, sems,
                     *idx_smem, tb, chunk):
    stages = [stages.at[n] for n in range(stages.shape[0])]
    mask = _slot_mask()

    def run_block(half, ids):
        base = half * tb

        def contract(t, rows):
            return lax.dot_general(x_ref[base + t], rows, (((1,), (1,)), ((), ())), preferred_element_type=F32)

        def keep(t, pos, g):
            zbuf[t] = jnp.where(mask, g[0:SLOT] + g[SLOT:], 0.0)

        _for_each_token(tab_vmem, ids, stages, tb, contract, keep)

        def finish(c, carry):
            r0 = pl.multiple_of(c * chunk, chunk)
            z = zbuf[pl.ds(r0, chunk)].reshape(chunk * SLOT, 2 * STAGE_ROWS)
            part = jnp.dot(_two_bf16(z, 0), fold_ref[...], preferred_element_type=F32)
            rows = part[0:chunk * SLOT] + part[chunk * SLOT:]
            act = jnp.sum(rows.reshape(chunk, SLOT, PEER_KK), axis=1)
            coef_ref[pl.ds(base + r0, chunk), :] = gate_ref[pl.ds(base + r0, chunk), :] * _gelu(act)
            return carry

        lax.fori_loop(0, tb // chunk, finish, 0)

    _for_both_blocks(tab_hbm, tab_vmem, idx_hbm, idx_smem, sems, run_block)


def peer_coefficients(idx_t, tab, x_terms, gate, *, tb=256, chunk=64):
    m = gate.shape[0]
    n_j = 2 * STAGE_ROWS
    fold = (np.arange(n_j)[:, None] // SLOT == np.arange(PEER_KK)[None, :]).astype(np.float32)
    return pl.pallas_call(
        functools.partial(_peer_act_kernel, tb=tb, chunk=chunk),
        grid=(m // (2 * tb),),
        in_specs=[pl.BlockSpec(memory_space=pl.ANY), pl.BlockSpec(memory_space=pl.ANY),
                  pl.BlockSpec((2 * tb, 2 * SLOT, 128), lambda i: (i, 0, 0)),
                  pl.BlockSpec((2 * tb, PEER_KK), lambda i: (i, 0)),
                  pl.BlockSpec((n_j, PEER_KK), lambda i: (0, 0))],
        out_specs=pl.BlockSpec((2 * tb, PEER_KK), lambda i: (i, 0)),
        out_shape=jax.ShapeDtypeStruct((m, PEER_KK), F32),
        scratch_shapes=[pltpu.VMEM(tab.shape, jnp.uint32),
                        pltpu.VMEM((ACT_STAGES, STAGE_ROWS, 128), jnp.uint32),
                        pltpu.VMEM((tb, SLOT, n_j), F32),
                        pltpu.SemaphoreType.DMA((1 + 2 * PEER_KK,))] + [pltpu.SMEM((tb,), jnp.int32)] * (2 * PEER_KK),
        compiler_params=_cparams(("arbitrary",), VMEM_LIMIT),
        name="peer_coefficients",
    )(idx_t.reshape(PEER_KK, m // tb, tb), tab, x_terms, gate, jnp.asarray(fold, BF16))


def _peer_mix_kernel(idx_hbm, tab_hbm, coef_ref, spread_ref, y_ref, tab_vmem, stages, lhs, sems,
                     *idx_smem, tb, chunk):
    stages = [stages.at[n] for n in range(stages.shape[0])]
    shape = (SLOT, 2 * STAGE_ROWS)
    sub = lax.broadcasted_iota(jnp.int32, shape, 0)
    col_slot = lax.broadcasted_iota(jnp.int32, shape, 1) & (SLOT - 1)
    low_slot = (sub & (ROW_SUB - 1)) * 2
    keep = (jnp.where(col_slot == low_slot, jnp.uint32(0x0000FFFF), jnp.uint32(0))
            | jnp.where(col_slot == low_slot + 1, jnp.uint32(0xFFFF0000), jnp.uint32(0)))
    first_term = sub < ROW_SUB

    def both_halves(a):
        bits = lax.bitcast_convert_type(a, jnp.uint32)
        return bits | (bits >> 16)

    def run_block(half, ids):
        base = half * tb

        def prepare(c, carry):
            r0 = pl.multiple_of(c * chunk, chunk)
            terms = _two_bf16(coef_ref[pl.ds(base + r0, chunk), :], 0)
            wide = jnp.dot(terms, spread_ref[...], preferred_element_type=F32)
            w_hi = both_halves(wide[0:chunk])
            w_lo = both_halves(wide[chunk:])
            for j in range(chunk):
                word = jnp.where(first_term, jnp.broadcast_to(w_hi[j:j + 1, :], shape),
                                 jnp.broadcast_to(w_lo[j:j + 1, :], shape))
                lhs[r0 + j] = word & keep
            return carry

        lax.fori_loop(0, tb // chunk, prepare, 0)

        def contract(t, rows):
            return jnp.dot(pltpu.bitcast(lhs[t], BF16), rows, preferred_element_type=F32)

        def store(t, pos, y):
            y_ref[base + t] = y[0:SLOT] + y[SLOT:]

        _for_each_token(tab_vmem, ids, stages, tb, contract, store)

    _for_both_blocks(tab_hbm, tab_vmem, idx_hbm, idx_smem, sems, run_block)


def peer_mix(idx_t, coef, tab, *, tb=256, chunk=64):
    m = coef.shape[0]
    n_j = 2 * STAGE_ROWS
    spread = (np.arange(PEER_KK)[:, None] == np.arange(n_j)[None, :] // SLOT).astype(np.float32)
    return pl.pallas_call(
        functools.partial(_peer_mix_kernel, tb=tb, chunk=chunk),
        grid=(m // (2 * tb),),
        in_specs=[pl.BlockSpec(memory_space=pl.ANY), pl.BlockSpec(memory_space=pl.ANY),
                  pl.BlockSpec((2 * tb, PEER_KK), lambda i: (i, 0)),
                  pl.BlockSpec((PEER_KK, n_j), lambda i: (0, 0))],
        out_specs=pl.BlockSpec((2 * tb, SLOT, 128), lambda i: (i, 0, 0)),
        out_shape=jax.ShapeDtypeStruct((m, SLOT, 128), F32),
        scratch_shapes=[pltpu.VMEM(tab.shape, jnp.uint32),
                        pltpu.VMEM((MIX_STAGES, STAGE_ROWS, 128), jnp.uint32),
                        pltpu.VMEM((tb, SLOT, n_j), jnp.uint32),
                        pltpu.SemaphoreType.DMA((1 + 2 * PEER_KK,))] + [pltpu.SMEM((tb,), jnp.int32)] * (2 * PEER_KK),
        compiler_params=_cparams(("arbitrary",), VMEM_LIMIT),
        name="peer_mix",
    )(idx_t.reshape(PEER_KK, m // tb, tb), tab, coef, jnp.asarray(spread, BF16))


def _ple_kernel(h_ref, y_ref, p_ref, g_ref, wg_ref, wp_ref, fg_ref, o_ref, *, final_norm):
    h = h_ref[...] + _from_slots(y_ref[...])
    gate = _sigmoid(jnp.dot(_rms(h, g_ref[...]).astype(BF16), wg_ref[...], preferred_element_type=F32))
    h = h + jnp.dot(p_ref[...].astype(BF16), wp_ref[...], preferred_element_type=F32) * gate
    if final_norm:
        h = _rms(h, fg_ref[...])
    o_ref[...] = h


def ple_residual(h, y, p, g, wg, wp, fg, *, final_norm, tm=512):
    m, d = h.shape
    row = lambda w: pl.BlockSpec((tm, w), lambda i: (i, 0))
    whole = lambda a: pl.BlockSpec(a.shape, lambda i: (0, 0))
    g2, fg2 = g.reshape(1, d), fg.reshape(1, d)
    return pl.pallas_call(
        functools.partial(_ple_kernel, final_norm=final_norm),
        grid=(m // tm,),
        in_specs=[row(d), row(d), row(p.shape[1]), whole(g2), whole(wg), whole(wp), whole(fg2)],
        out_specs=row(d),
        out_shape=jax.ShapeDtypeStruct((m, d), F32),
        compiler_params=_cparams(("parallel",), VMEM_LIMIT),
        name="ple_residual",
    )(h, y, p, g2, wg, wp, fg2)


def _mixer_conv_nsa(h, b, t, norm_g, w_in, conv_w, conv_b, ln_g, ln_b, cmp_pos, cmp_w1, cmp_w2, w_out):
    m = b * t
    n_q = NSA_KV_HEADS * NSA_HPG * HEAD_DIM
    kv_w = NSA_KV_HEADS * HEAD_DIM
    c0, c1, c2 = 2 * CONV_CH, 2 * CONV_CH + n_q, 2 * CONV_CH + n_q + 6 * kv_w
    wb = w_in.astype(BF16)
    vg, q, kv, gt = norm_matmul(h, norm_g, [wb[:, :c0], wb[:, c0:c1], wb[:, c1:c2], wb[:, c2:]])
    a_out = conformer_conv(vg.reshape(b, t, c0), conv_w, conv_b, ln_g, ln_b)

    kv = kv.reshape(b, t, 6, kv_w)
    kind = lambda c: kv[:, :, c]
    n_chunk = t // CMP_STRIDE
    cmp_src = jnp.stack([kind(0), kind(1)]).reshape(2, b, n_chunk, CMP_STRIDE * kv_w)
    cmp = compress_blocks(cmp_src, cmp_pos, cmp_w1, cmp_w2)

    lane_group = jnp.arange(kv_w) // HEAD_DIM

    def value_operand(v):
        per_group = []
        for g in range(NSA_KV_HEADS):
            other = jnp.where(jnp.arange(kv_w) == HEAD_DIM * (1 - g), 1.0, 0.0)
            per_group.append(jnp.where(lane_group == g, v, other).astype(BF16))
        return jnp.stack(per_group, axis=1)

    o = nsa_attention(q.reshape(b, t, n_q), cmp[0], cmp[1], kind(2).astype(BF16), value_operand(kind(3)),
                      kind(4).astype(BF16), value_operand(kind(5)), gt.reshape(b, t, -1))
    wo = w_out.astype(BF16)
    return mm2_residual(h, a_out.reshape(m, CONV_CH), o.reshape(m, n_q), wo[:CONV_CH], wo[CONV_CH:])


def _peer_ffn(h, norm_g, wq, subkeys, u_tab, v_tab):
    q, x_terms = norm_matmul(h, norm_g, [wq.astype(BF16)], emit_hn=True)
    zeros = jnp.zeros_like(subkeys[:, 0])
    keys_bd = jnp.concatenate([jnp.concatenate([subkeys[:, 0], zeros], axis=-1),
                               jnp.concatenate([zeros, subkeys[:, 1]], axis=-1)], axis=1)
    idx_t, gate_t = peer_route(q, keys_bd)
    m = h.shape[0]
    idx_t = idx_t.reshape(PEER_KK, m)
    gate = gate_t.reshape(PEER_KK, m).T
    coef = peer_coefficients(idx_t, pack_table(u_tab), x_terms.reshape(m, 2 * SLOT, 128), gate)
    return peer_mix(idx_t, coef, pack_table(v_tab)).reshape(m, SLOT * 128)


def kernel(x, p, mix_norm, ab_w_in, ab_conv_w, ab_conv_b, ab_conv_ln_g, ab_conv_ln_b, ab_cmp_pos, ab_cmp_w1,
           ab_cmp_w2, ab_w_out, pool_w, pool_scale, ffn_norm, peer_wq, peer_subkeys, peer_u, peer_v, ple_norm,
           ple_gate_w, ple_proj, final_norm):
    b, t, d = x.shape
    m = b * t
    depth = p.shape[0]
    h = x.reshape(m, d)
    for i in range(depth):
        j = i // 2
        if i % 2 == 0:
            h = _mixer_conv_nsa(h, b, t, mix_norm[i], ab_w_in[j], ab_conv_w[j], ab_conv_b[j], ab_conv_ln_g[j],
                                ab_conv_ln_b[j], ab_cmp_pos[j], ab_cmp_w1[j], ab_cmp_w2[j], ab_w_out[j])
        else:
            h = pool_mixer_residual(h.reshape(b, t, d), mix_norm[i], pool_w[j], pool_scale[j]).reshape(m, d)
        y = _peer_ffn(h, ffn_norm[i], peer_wq[i], peer_subkeys[i], peer_u[i], peer_v[i])
        h = ple_residual(h, y, p[i].reshape(m, -1), ple_norm[i], ple_gate_w[i].astype(BF16),
                         ple_proj[i].astype(BF16), final_norm, final_norm=(i == depth - 1))
    return h.reshape(b, t, d)
```
